```python
import math
import jax, jax.numpy as jnp
from jax import lax
import numpy as np

D_MODEL = 1024
BATCH = 16
SEQ = 256
DEPTH = 4
DEC_BATCH = 4
DEC_SEQ = 1024
PAST_LEN = 512

GRID_W = 64
N_MIXERS = 3
N_ATTN_LAYERS = (DEPTH + 2) // 3
N_SSM_LAYERS = (DEPTH + 1) // 3
N_GMLP_LAYERS = DEPTH // 3
DEEPNORM_ALPHA = (2 * DEPTH) ** 0.25
DEEPNORM_BETA = (8 * DEPTH) ** -0.25
LN_EPS = 1e-5
ATTN_HEADS = 8
ATTN_HEAD_DIM = 64
ATTN_WIDTH = ATTN_HEADS * 2 * ATTN_HEAD_DIM
ROPE_AXIS_DIM = ATTN_HEAD_DIM // 2
ROPE_BASE = 10000.0
Q_BLOCK = 128
SSM_INNER = 2 * D_MODEL
SSM_HEAD_DIM = 64
SSM_HEADS = SSM_INNER // SSM_HEAD_DIM
SSM_GROUPS = 8
SSM_HPG = SSM_HEADS // SSM_GROUPS
SSM_STATE = 128
SSM_CONV = 5
SSM_CONV_CH = SSM_INNER + 2 * SSM_GROUPS * SSM_STATE
SSM_IN_WIDTH = SSM_INNER + SSM_CONV_CH + 2 * SSM_HEADS
SSD_CHUNK = 128
GMLP_HALF = 2 * D_MODEL
GMLP_GROUPS = 8
GMLP_GROUP_DIM = GMLP_HALF // GMLP_GROUPS
GMLP_CHUNK = 128
PEER_HEADS = 8
PEER_KEYS = 128
PEER_EXPERTS = PEER_KEYS * PEER_KEYS
PEER_QUERY_DIM = 256
PEER_HALF = PEER_QUERY_DIM // 2
PEER_TOPK = 16
PEER_BLOCK = 128

kernel_name = 'hybrid_diffusion_diffattn_ssd_gmlp_peer_step'


def layer_norm(x, g, b):
    xf = x.astype(jnp.float32)
    mu = jnp.mean(xf, -1, keepdims=True)
    var = jnp.mean(jnp.square(xf - mu), -1, keepdims=True)
    y = (xf - mu) * lax.rsqrt(var + LN_EPS) * g.astype(jnp.float32) + b.astype(jnp.float32)
    return y.astype(x.dtype)


def rms_norm(x, g):
    xf = x.astype(jnp.float32)
    y = xf * lax.rsqrt(jnp.mean(jnp.square(xf), -1, keepdims=True) + LN_EPS)
    return (y * g.astype(jnp.float32)).astype(x.dtype)


def axial_rope_tables(n_tokens):
    rows = n_tokens // GRID_W
    row_pos = jnp.repeat(jnp.arange(rows, dtype=jnp.float32), GRID_W)
    col_pos = jnp.tile(jnp.arange(GRID_W, dtype=jnp.float32), rows)
    inv_freq = ROPE_BASE ** (-jnp.arange(0, ROPE_AXIS_DIM, 2, dtype=jnp.float32) / ROPE_AXIS_DIM)
    def tab(pos):
        ang = pos[:, None] * inv_freq[None, :]
        ang = jnp.concatenate([ang, ang], -1)
        return jnp.cos(ang), jnp.sin(ang)
    cr, sr = tab(row_pos)
    cc, sc = tab(col_pos)
    return jnp.concatenate([cr, cc], -1), jnp.concatenate([sr, sc], -1)


def rotate_half_axial(x):
    xr = x.reshape(x.shape[:-1] + (2, 2, ROPE_AXIS_DIM // 2))
    return jnp.stack([-xr[..., 1, :], xr[..., 0, :]], -2).reshape(x.shape)


def apply_axial_rope(x, cos, sin):
    cos = cos[None, :, None, None, :].astype(x.dtype)
    sin = sin[None, :, None, None, :].astype(x.dtype)
    return x * cos + rotate_half_axial(x) * sin


def diff_lambda(lam_p, lam_init):
    lp = lam_p.astype(jnp.float32)
    return jnp.exp(jnp.sum(lp[0] * lp[1])) - jnp.exp(jnp.sum(lp[2] * lp[3])) + lam_init


def split_qkv(h, w_qkv):
    b, l, _ = h.shape
    q, k, v = jnp.split(h @ w_qkv, 3, axis=-1)
    q = q.reshape(b, l, ATTN_HEADS, 2, ATTN_HEAD_DIM)
    k = k.reshape(b, l, ATTN_HEADS, 2, ATTN_HEAD_DIM)
    v = v.reshape(b, l, ATTN_HEADS, 2 * ATTN_HEAD_DIM)
    return q, k, v


def diff_attention(q, k, v, lam):
    b, lq = q.shape[:2]
    nb = lq // Q_BLOCK
    qb = jnp.moveaxis(q.reshape((b, nb, Q_BLOCK) + q.shape[2:]), 1, 0)
    scale = ATTN_HEAD_DIM ** -0.5
    def block(qblk):
        s = jnp.einsum('bqhmd,bkhmd->bhmqk', qblk, k).astype(jnp.float32) * scale
        p = jax.nn.softmax(s, axis=-1)
        a = p[:, :, 0] - lam * p[:, :, 1]
        return jnp.einsum('bhqk,bkhe->bqhe', a.astype(v.dtype), v)
    o = lax.map(block, qb)
    return jnp.moveaxis(o, 0, 1).reshape(b, lq, ATTN_HEADS, 2 * ATTN_HEAD_DIM)


def attn_out(o, subln_g, lam_init, w_o):
    b, l = o.shape[:2]
    o = rms_norm(o, subln_g) * (1.0 - lam_init)
    return o.reshape(b, l, ATTN_WIDTH) @ w_o


def ssd_scan(x, dt, a, bm, cm, h0):
    b, l = x.shape[:2]
    nc = l // SSD_CHUNK
    def chunk(t):
        return t.astype(jnp.float32).reshape((b, nc, SSD_CHUNK) + t.shape[2:])
    xc, dtc, bc, cc = chunk(x), chunk(dt), chunk(bm), chunk(cm)
    acum = jnp.cumsum(dtc * a, axis=2)
    lower = jnp.tril(jnp.ones((SSD_CHUNK, SSD_CHUNK), dtype=bool))
    seg = acum[:, :, :, None] - acum[:, :, None, :]
    lmask = jnp.exp(jnp.where(lower[None, None, :, :, None, None], seg, -jnp.inf))
    xdt = xc * dtc[..., None]
    cb = jnp.einsum('bcign,bcjgn->bcijg', cc, bc)
    y_intra = jnp.einsum('bcijg,bcijgr,bcjgrp->bcigrp', cb, lmask, xdt)
    decay_end = jnp.exp(acum[:, :, -1:] - acum)
    states = jnp.einsum('bcjgn,bcjgr,bcjgrp->bcgrpn', bc, decay_end, xdt)
    chunk_decay = jnp.exp(acum[:, :, -1])
    def step(h, inp):
        dec, st = inp
        return dec[..., None, None] * h + st, h
    h_final, h_starts = lax.scan(step, h0, (jnp.moveaxis(chunk_decay, 1, 0), jnp.moveaxis(states, 1, 0)))
    h_starts = jnp.moveaxis(h_starts, 0, 1)
    y_inter = jnp.einsum('bcign,bcigr,bcgrpn->bcigrp', cc, jnp.exp(acum), h_starts)
    return (y_intra + y_inter).reshape(x.shape), h_final


def mamba_mixer(h, w_in, conv_w, conv_b, dt_bias, a_log, d_skip, norm_g, w_out, h0_fwd, h0_bwd):
    b, l, _ = h.shape
    z, xbc, dt_raw = jnp.split(h @ w_in, [SSM_INNER, SSM_INNER + SSM_CONV_CH], axis=-1)
    xbc = lax.conv_general_dilated(xbc, conv_w[:, None, :], window_strides=(1,),
                                   padding=[(SSM_CONV // 2, SSM_CONV // 2)],
                                   dimension_numbers=('NWC', 'WIO', 'NWC'),
                                   feature_group_count=SSM_CONV_CH) + conv_b
    xbc = jax.nn.silu(xbc)
    xs, bm, cm = jnp.split(xbc, [SSM_INNER, SSM_INNER + SSM_GROUPS * SSM_STATE], axis=-1)
    xs = xs.reshape(b, l, SSM_GROUPS, SSM_HPG, SSM_HEAD_DIM)
    bm = bm.reshape(b, l, SSM_GROUPS, SSM_STATE)
    cm = cm.reshape(b, l, SSM_GROUPS, SSM_STATE)
    dt = jax.nn.softplus(dt_raw.astype(jnp.float32).reshape(b, l, 2, SSM_GROUPS, SSM_HPG)
                         + dt_bias.astype(jnp.float32).reshape(2, SSM_GROUPS, SSM_HPG))
    a = -jnp.exp(a_log.astype(jnp.float32)).reshape(2, SSM_GROUPS, SSM_HPG)
    st_shape = (b, SSM_GROUPS, SSM_HPG, SSM_HEAD_DIM, SSM_STATE)
    y_f, s_f = ssd_scan(xs, dt[:, :, 0], a[0], bm, cm, h0_fwd.astype(jnp.float32).reshape(st_shape))
    flip = lambda t: jnp.flip(t, axis=1)
    y_b, s_b = ssd_scan(flip(xs), flip(dt[:, :, 1]), a[1], flip(bm), flip(cm),
                        h0_bwd.astype(jnp.float32).reshape(st_shape))
    y = y_f + flip(y_b) + d_skip.astype(jnp.float32).reshape(SSM_GROUPS, SSM_HPG)[..., None] * xs.astype(jnp.float32)
    y = y.reshape(b, l, SSM_INNER).astype(h.dtype) * jax.nn.silu(z)
    y = rms_norm(y, norm_g)
    out_state = (b, SSM_HEADS, SSM_HEAD_DIM, SSM_STATE)
    return y @ w_out, s_f.reshape(out_state).astype(h.dtype), s_b.reshape(out_state).astype(h.dtype)


def chunk_gmlp(h, w_in, b_in, ln_g, ln_b, w_s, b_s, w_out):
    b, l, _ = h.shape
    nc = l // GMLP_CHUNK
    uv = jax.nn.gelu(h @ w_in + b_in)
    u, v = jnp.split(uv, 2, axis=-1)
    v = layer_norm(v, ln_g, ln_b).reshape(b, nc, GMLP_CHUNK, GMLP_GROUPS, GMLP_GROUP_DIM)
    sv = jnp.einsum('gij,bcjgd->bcigd', w_s, v) + b_s.T[:, :, None]
    return (u * sv.reshape(b, l, GMLP_HALF)) @ w_out


def peer_ffn(h, w_q, sub_keys, u_tab, v_tab):
    shp = h.shape
    t = h.reshape(-1, D_MODEL)
    n = t.shape[0]
    q = (t @ w_q).reshape(n, PEER_HEADS, 2, PEER_HALF)
    s = jnp.einsum('thcd,hckd->thck', q, sub_keys).astype(jnp.float32)
    top_s, top_i = lax.top_k(s, PEER_TOPK)
    cand_s = (top_s[:, :, 0, :, None] + top_s[:, :, 1, None, :]).reshape(n, PEER_HEADS, PEER_TOPK * PEER_TOPK)
    cand_i = (top_i[:, :, 0, :, None] * PEER_KEYS + top_i[:, :, 1, None, :]).reshape(n, PEER_HEADS, PEER_TOPK * PEER_TOPK)
    best_s, pos = lax.top_k(cand_s, PEER_TOPK)
    idx = jnp.take_along_axis(cand_i, pos, axis=-1)
    gate = jax.nn.softmax(best_s, axis=-1).astype(h.dtype)
    nb = n // PEER_BLOCK
    def block(args):
        tb, ib, gb = args
        act = jax.nn.gelu(jnp.einsum('td,thkd->thk', tb, u_tab[ib])) * gb
        return jnp.einsum('thk,thkd->td', act, v_tab[ib])
    out = lax.map(block, (t.reshape(nb, PEER_BLOCK, D_MODEL),
                          idx.reshape(nb, PEER_BLOCK, PEER_HEADS, PEER_TOPK),
                          gate.reshape(nb, PEER_BLOCK, PEER_HEADS, PEER_TOPK)))
    return out.reshape(shp)


def setup_inputs(seed: int = 0) -> dict:
    key = jax.random.key(seed)
    ks = jax.random.split(key, 40)
    def nrm(k, shape, scale):
        return jax.random.normal(k, shape, jnp.float32) * scale
    D = D_MODEL
    inv = D ** -0.5
    beta = DEEPNORM_BETA
    dt0 = jnp.exp(jax.random.uniform(ks[20], (N_SSM_LAYERS, 2, SSM_HEADS), jnp.float32,
                                     math.log(1e-3), math.log(1e-1)))
    return {
        'x_prompt': nrm(ks[0], (BATCH, SEQ, D), 1.0),
        'x_sample': nrm(ks[1], (DEC_BATCH, DEC_SEQ, D), 1.0),
        'cache_k': nrm(ks[2], (DEC_BATCH, N_ATTN_LAYERS, PAST_LEN, ATTN_HEADS, 2, ATTN_HEAD_DIM), 1.0),
        'cache_v': nrm(ks[3], (DEC_BATCH, N_ATTN_LAYERS, PAST_LEN, ATTN_HEADS, 2 * ATTN_HEAD_DIM), 1.0),
        'state_ssm': nrm(ks[4], (DEC_BATCH, N_SSM_LAYERS, 2, SSM_HEADS, SSM_HEAD_DIM, SSM_STATE), 1.0),
        'c': nrm(ks[5], (DEC_BATCH, D), 1.0),
        'c_ctx': nrm(ks[6], (D,), 1.0),
        'w_mod': nrm(ks[7], (DEPTH, D, 6 * D), 0.5 * inv),
        'b_mod': nrm(ks[8], (DEPTH, 6 * D), 0.01),
        'ln_g': 1.0 + nrm(ks[9], (DEPTH, 2, D), 0.01),
        'ln_b': nrm(ks[10], (DEPTH, 2, D), 0.01),
        'attn_w_qkv': nrm(ks[11], (N_ATTN_LAYERS, D, 3 * ATTN_WIDTH), inv),
        'attn_w_o': nrm(ks[12], (N_ATTN_LAYERS, ATTN_WIDTH, D), ATTN_WIDTH ** -0.5 * beta),
        'attn_lambda': nrm(ks[13], (N_ATTN_LAYERS, 4, ATTN_HEAD_DIM), 0.1),
        'attn_subln_g': 1.0 + nrm(ks[14], (N_ATTN_LAYERS, 2 * ATTN_HEAD_DIM), 0.01),
        'ssm_w_in': nrm(ks[15], (N_SSM_LAYERS, D, SSM_IN_WIDTH), inv),
        'ssm_conv_w': nrm(ks[16], (N_SSM_LAYERS, SSM_CONV, SSM_CONV_CH), SSM_CONV ** -0.5),
        'ssm_conv_b': nrm(ks[17], (N_SSM_LAYERS, SSM_CONV_CH), 0.01),
        'ssm_dt_bias': dt0 + jnp.log(-jnp.expm1(-dt0)),
        'ssm_a_log': jnp.log(jax.random.uniform(ks[18], (N_SSM_LAYERS, 2, SSM_HEADS), jnp.float32, 1.0, 16.0)),
        'ssm_d': 1.0 + nrm(ks[19], (N_SSM_LAYERS, SSM_HEADS), 0.1),
        'ssm_norm_g': 1.0 + nrm(ks[21], (N_SSM_LAYERS, SSM_INNER), 0.01),
        'ssm_w_out': nrm(ks[22], (N_SSM_LAYERS, SSM_INNER, D), SSM_INNER ** -0.5 * beta),
        'gmlp_w_in': nrm(ks[23], (N_GMLP_LAYERS, D, 2 * GMLP_HALF), inv),
        'gmlp_b_in': nrm(ks[24], (N_GMLP_LAYERS, 2 * GMLP_HALF), 0.01),
        'gmlp_ln_g': 1.0 + nrm(ks[25], (N_GMLP_LAYERS, GMLP_HALF), 0.01),
        'gmlp_ln_b': nrm(ks[26], (N_GMLP_LAYERS, GMLP_HALF), 0.01),
        'gmlp_w_s': nrm(ks[27], (N_GMLP_LAYERS, GMLP_GROUPS, GMLP_CHUNK, GMLP_CHUNK), GMLP_CHUNK ** -0.5),
        'gmlp_b_s': 1.0 + nrm(ks[28], (N_GMLP_LAYERS, GMLP_GROUPS, GMLP_CHUNK), 0.1),
        'gmlp_w_out': nrm(ks[29], (N_GMLP_LAYERS, GMLP_HALF, D), GMLP_HALF ** -0.5 * beta),
        'peer_w_q': nrm(ks[30], (DEPTH, D, PEER_HEADS * PEER_QUERY_DIM), inv),
        'peer_sub_keys': nrm(ks[31], (DEPTH, PEER_HEADS, 2, PEER_KEYS, PEER_HALF), PEER_HALF ** -0.5),
        'peer_u': nrm(ks[32], (DEPTH, PEER_EXPERTS, D), inv),
        'peer_v': nrm(ks[33], (DEPTH, PEER_EXPERTS, D), beta),
    }


def reference(x_prompt, x_sample, cache_k, cache_v, state_ssm, c, c_ctx, w_mod, b_mod, ln_g, ln_b,
              attn_w_qkv, attn_w_o, attn_lambda, attn_subln_g,
              ssm_w_in, ssm_conv_w, ssm_conv_b, ssm_dt_bias, ssm_a_log, ssm_d, ssm_norm_g, ssm_w_out,
              gmlp_w_in, gmlp_b_in, gmlp_ln_g, gmlp_ln_b, gmlp_w_s, gmlp_b_s, gmlp_w_out,
              peer_w_q, peer_sub_keys, peer_u, peer_v):
    ctx, lat = x_prompt, x_sample
    bp = ctx.shape[0]
    rope_cos, rope_sin = axial_rope_tables(lat.shape[1])
    new_k, new_v, new_s = [], [], []
    for i in range(DEPTH):
        kind, j = i % N_MIXERS, i // N_MIXERS
        m_ctx = jnp.split(jax.nn.silu(c_ctx) @ w_mod[i] + b_mod[i], 6, axis=-1)
        m_lat = [m[:, None, :] for m in jnp.split(jax.nn.silu(c) @ w_mod[i] + b_mod[i], 6, axis=-1)]
        h_ctx = ctx * (1 + m_ctx[1]) + m_ctx[0]
        h_lat = lat * (1 + m_lat[1]) + m_lat[0]
        if kind == 0:
            lam_init = 0.8 - 0.6 * math.exp(-0.3 * i)
            lam = diff_lambda(attn_lambda[j], lam_init)
            q_c, k_c, v_c = split_qkv(h_ctx, attn_w_qkv[j])
            o_ctx = attn_out(diff_attention(q_c, k_c, v_c, lam), attn_subln_g[j], lam_init, attn_w_o[j])
            q_l, k_l, v_l = split_qkv(h_lat, attn_w_qkv[j])
            q_l = apply_axial_rope(q_l, rope_cos, rope_sin)
            k_l = apply_axial_rope(k_l, rope_cos, rope_sin)
            k_all = jnp.concatenate([cache_k[:, j], k_l], axis=1)
            v_all = jnp.concatenate([cache_v[:, j], v_l], axis=1)
            o_lat = attn_out(diff_attention(q_l, k_all, v_all, lam), attn_subln_g[j], lam_init, attn_w_o[j])
            new_k.append(k_c)
            new_v.append(v_c)
        elif kind == 1:
            zeros = jnp.zeros((bp, SSM_HEADS, SSM_HEAD_DIM, SSM_STATE), jnp.float32)
            o_ctx, s_f, s_b = mamba_mixer(h_ctx, ssm_w_in[j], ssm_conv_w[j], ssm_conv_b[j], ssm_dt_bias[j],
                                          ssm_a_log[j], ssm_d[j], ssm_norm_g[j], ssm_w_out[j], zeros, zeros)
            o_lat, _, _ = mamba_mixer(h_lat, ssm_w_in[j], ssm_conv_w[j], ssm_conv_b[j], ssm_dt_bias[j],
                                      ssm_a_log[j], ssm_d[j], ssm_norm_g[j], ssm_w_out[j],
                                      state_ssm[:, j, 0], state_ssm[:, j, 1])
            new_s.append(jnp.stack([s_f, s_b], axis=1))
        else:
            o_ctx = chunk_gmlp(h_ctx, gmlp_w_in[j], gmlp_b_in[j], gmlp_ln_g[j], gmlp_ln_b[j],
                               gmlp_w_s[j], gmlp_b_s[j], gmlp_w_out[j])
            o_lat = chunk_gmlp(h_lat, gmlp_w_in[j], gmlp_b_in[j], gmlp_ln_g[j], gmlp_ln_b[j],
                               gmlp_w_s[j], gmlp_b_s[j], gmlp_w_out[j])
        ctx = layer_norm(DEEPNORM_ALPHA * ctx + m_ctx[2] * o_ctx, ln_g[i, 0], ln_b[i, 0])
        lat = layer_norm(DEEPNORM_ALPHA * lat + m_lat[2] * o_lat, ln_g[i, 0], ln_b[i, 0])
        h_ctx = ctx * (1 + m_ctx[4]) + m_ctx[3]
        h_lat = lat * (1 + m_lat[4]) + m_lat[3]
        f_ctx = peer_ffn(h_ctx, peer_w_q[i], peer_sub_keys[i], peer_u[i], peer_v[i])
        f_lat = peer_ffn(h_lat, peer_w_q[i], peer_sub_keys[i], peer_u[i], peer_v[i])
        ctx = layer_norm(DEEPNORM_ALPHA * ctx + m_ctx[5] * f_ctx, ln_g[i, 1], ln_b[i, 1])
        lat = layer_norm(DEEPNORM_ALPHA * lat + m_lat[5] * f_lat, ln_g[i, 1], ln_b[i, 1])
    new_cache_k = jnp.stack(new_k, axis=1)
    new_cache_v = jnp.stack(new_v, axis=1)
    new_state_ssm = jnp.stack(new_s, axis=1)
    return (ctx, lat, new_cache_k, new_cache_v, new_state_ssm)
```

```python
import functools
import math

import jax
import jax.numpy as jnp
from jax import lax
from jax.experimental import pallas as pl
from jax.experimental.pallas import tpu as pltpu

F32 = jnp.float32
BF16 = jnp.bfloat16

D_MODEL = 1024
LN_EPS = 1e-5
GRID_W = 64
ATTN_HEADS = 8
ATTN_HEAD_DIM = 64
ROPE_AXIS_DIM = ATTN_HEAD_DIM // 2
ROPE_BASE = 10000.0
SSM_INNER = 2 * D_MODEL
SSM_HEAD_DIM = 64
SSM_GROUPS = 8
SSM_HPG = 4
SSM_STATE = 128
SSM_CONV = 5
SSD_CHUNK = 128
GMLP_HALF = 2 * D_MODEL
GMLP_GROUPS = 8
GMLP_GROUP_DIM = GMLP_HALF // GMLP_GROUPS
GMLP_CHUNK = 128
PEER_HEADS = 8
PEER_KEYS = 128
PEER_HALF = 128
PEER_TOPK = 16
N_MOD_GROUPS = 8
NEG_BIG = -1e30

VMEM_LIMIT = 56 * 1024 * 1024
ROW_TILE = 256
MM_ROW_TILE = 512
MM_COL_TILE = 1024
PEER_TOKEN_TILE = 256
PEER_SUPER_TILE = 1024
PEER_SUB_TILE = 256
PEER_ROWS_PER_BLOCK = 4


def _params(n_axes):
    return pltpu.CompilerParams(dimension_semantics=("arbitrary",) * n_axes,
                                vmem_limit_bytes=VMEM_LIMIT)


def _bdot(a, b):
    return jnp.dot(a.astype(BF16), b.astype(BF16), preferred_element_type=F32)


def _bdot_nt(a, b):
    return lax.dot_general(a.astype(BF16), b.astype(BF16), (((1,), (1,)), ((), ())),
                           preferred_element_type=F32)


def _split3(a):
    hi = a.astype(BF16)
    r = a - hi.astype(F32)
    mid = r.astype(BF16)
    lo = (r - mid.astype(F32)).astype(BF16)
    return hi, mid, lo


def _dot_x3(a, b):
    ah, am, _ = _split3(a)
    bh, bm, _ = _split3(b)
    d = functools.partial(jnp.dot, preferred_element_type=F32)
    return d(ah, bh) + (d(am, bh) + d(ah, bm))


def _silu(x):
    return x * jax.nn.sigmoid(x)


def _gelu(x):
    return 0.5 * x * (1.0 + jnp.tanh(math.sqrt(2.0 / math.pi) * (x + 0.044715 * (x * x * x))))


def _group_index(i, tm, t_ctx, l_lat):
    start = i * tm
    return jnp.where(start < t_ctx, 0, 1 + (start - t_ctx) // l_lat)


def _mod_body(c_ref, w_ref, b_ref, o_ref):
    o_ref[0] = _dot_x3(_silu(c_ref[...]), w_ref[0]) + b_ref[0]


def _modulation(cond, w_mod, b_mod):
    depth, d, n = w_mod.shape
    tn = 1536
    return pl.pallas_call(
        _mod_body,
        grid=(depth, n // tn),
        in_specs=[pl.BlockSpec((N_MOD_GROUPS, d), lambda l, j: (0, 0)),
                  pl.BlockSpec((1, d, tn), lambda l, j: (l, 0, j)),
                  pl.BlockSpec((1, 1, tn), lambda l, j: (l, 0, j))],
        out_specs=pl.BlockSpec((1, N_MOD_GROUPS, tn), lambda l, j: (l, 0, j)),
        out_shape=jax.ShapeDtypeStruct((depth, N_MOD_GROUPS, n), F32),
        compiler_params=_params(2),
        name="modulation",
    )(cond, w_mod, b_mod.reshape(depth, 1, n))


def _modulate_body(x_ref, m_ref, h_ref, *, shift, scale):
    m = m_ref[0]
    h_ref[...] = (x_ref[...] * (1.0 + m[scale:scale + 1]) + m[shift:shift + 1]).astype(BF16)


def _modulate(x, mod, t_ctx, l_lat, shift, scale):
    t, d = x.shape
    tm = ROW_TILE
    return pl.pallas_call(
        functools.partial(_modulate_body, shift=shift, scale=scale),
        grid=(t // tm,),
        in_specs=[pl.BlockSpec((tm, d), lambda i: (i, 0)),
                  pl.BlockSpec((1, 6, d), lambda i: (_group_index(i, tm, t_ctx, l_lat), 0, 0))],
        out_specs=pl.BlockSpec((tm, d), lambda i: (i, 0)),
        out_shape=jax.ShapeDtypeStruct((t, d), BF16),
        compiler_params=_params(1),
        name="modulate",
    )(x, mod)


def _res_ln_body(x_ref, o_ref, m_ref, mn_ref, g_ref, b_ref, *outs,
                 alpha, gate, nshift, nscale, o_transposed, emit_h, emit_ht):
    o = o_ref[...]
    if o_transposed:
        o = o.T
    m = m_ref[0]
    y = alpha * x_ref[...] + m[gate:gate + 1] * o
    mu = jnp.mean(y, axis=-1, keepdims=True)
    yc = y - mu
    var = jnp.mean(yc * yc, axis=-1, keepdims=True)
    xn = yc * lax.rsqrt(var + LN_EPS) * g_ref[...] + b_ref[...]
    outs[0][...] = xn
    if emit_h or emit_ht:
        mn = mn_ref[0]
        h = xn * (1.0 + mn[nscale:nscale + 1]) + mn[nshift:nshift + 1]
        k = 1
        if emit_h:
            outs[k][...] = h.astype(BF16)
            k += 1
        if emit_ht:
            outs[k][...] = h.T.astype(BF16)


def _res_ln(x, o, mod, mod_next, ln_g, ln_b, t_ctx, l_lat, *, alpha, gate, nshift=0, nscale=1,
            o_transposed=False, emit_h=False, emit_ht=False):
    t, d = x.shape
    tm = ROW_TILE
    grp = lambda i: (_group_index(i, tm, t_ctx, l_lat), 0, 0)
    o_spec = (pl.BlockSpec((d, tm), lambda i: (0, i)) if o_transposed
              else pl.BlockSpec((tm, d), lambda i: (i, 0)))
    out_specs = [pl.BlockSpec((tm, d), lambda i: (i, 0))]
    out_shape = [jax.ShapeDtypeStruct((t, d), F32)]
    if emit_h:
        out_specs.append(pl.BlockSpec((tm, d), lambda i: (i, 0)))
        out_shape.append(jax.ShapeDtypeStruct((t, d), BF16))
    if emit_ht:
        out_specs.append(pl.BlockSpec((d, tm), lambda i: (0, i)))
        out_shape.append(jax.ShapeDtypeStruct((d, t), BF16))
    return pl.pallas_call(
        functools.partial(_res_ln_body, alpha=alpha, gate=gate, nshift=nshift, nscale=nscale,
                          o_transposed=o_transposed, emit_h=emit_h, emit_ht=emit_ht),
        grid=(t // tm,),
        in_specs=[pl.BlockSpec((tm, d), lambda i: (i, 0)), o_spec,
                  pl.BlockSpec((1, 6, d), grp), pl.BlockSpec((1, 6, d), grp),
                  pl.BlockSpec((1, d), lambda i: (0, 0)), pl.BlockSpec((1, d), lambda i: (0, 0))],
        out_specs=out_specs,
        out_shape=out_shape,
        compiler_params=_params(1),
        name="residual_layernorm",
    )(x, o, mod, mod_next, ln_g.reshape(1, d), ln_b.reshape(1, d))


def _linear_body(h_ref, w_ref, *refs, has_bias, act):
    if has_bias:
        b_ref, o_ref, wbf = refs
    else:
        o_ref, wbf = refs

    @pl.when(pl.program_id(1) == 0)
    def _():
        wbf[...] = w_ref[...].astype(BF16)

    y = jnp.dot(h_ref[...], wbf[...], preferred_element_type=F32)
    if has_bias:
        y = y + b_ref[...]
    if act == "gelu":
        y = _gelu(y)
    o_ref[...] = y


def _linear(h, w_stack, layer, n_out, bias=None, act=None):
    t, k = h.shape
    tm, tn = MM_ROW_TILE, min(MM_COL_TILE, n_out)
    in_specs = [pl.BlockSpec((tm, k), lambda n, i: (i, 0)),
                pl.BlockSpec((None, k, tn), lambda n, i: (layer, 0, n))]
    args = [h, w_stack]
    if bias is not None:
        in_specs.append(pl.BlockSpec((1, tn), lambda n, i: (0, n)))
        args.append(bias.reshape(1, n_out))
    return pl.pallas_call(
        functools.partial(_linear_body, has_bias=bias is not None, act=act),
        grid=(n_out // tn, t // tm),
        in_specs=in_specs,
        out_specs=pl.BlockSpec((tm, tn), lambda n, i: (i, n)),
        out_shape=jax.ShapeDtypeStruct((t, n_out), F32),
        scratch_shapes=[pltpu.VMEM((k, tn), BF16)],
        compiler_params=_params(2),
        name="linear",
    )(*args)


def _qkv_body(h_ref, w_ref, cos_ref, sa_ref, sb_ref, o_ref, wbf, *, n_ctx_tiles):
    n, i = pl.program_id(0), pl.program_id(1)

    @pl.when(i == 0)
    def _():
        wbf[...] = w_ref[...].astype(BF16)

    y = jnp.dot(h_ref[...], wbf[...], preferred_element_type=F32)
    rope = jnp.logical_and(n < 2, i >= n_ctx_tiles)
    half = ROPE_AXIS_DIM // 2

    @pl.when(rope)
    def _():
        up = pltpu.roll(y, y.shape[1] - half, 1)
        dn = pltpu.roll(y, half, 1)
        o_ref[...] = y * cos_ref[...] + up * sa_ref[...] + dn * sb_ref[...]

    @pl.when(jnp.logical_not(rope))
    def _():
        o_ref[...] = y


def _qkv_rope(h, w_stack, layer, tables, t_ctx, l_lat):
    t, k = h.shape
    n_out = w_stack.shape[2]
    tm, tn = MM_ROW_TILE, MM_COL_TILE
    n_ctx_tiles = t_ctx // tm
    pos = lambda n, i: (jnp.maximum(i - n_ctx_tiles, 0) % (l_lat // tm), 0)
    return pl.pallas_call(
        functools.partial(_qkv_body, n_ctx_tiles=n_ctx_tiles),
        grid=(n_out // tn, t // tm),
        in_specs=[pl.BlockSpec((tm, k), lambda n, i: (i, 0)),
                  pl.BlockSpec((None, k, tn), lambda n, i: (layer, 0, n)),
                  pl.BlockSpec((tm, tn), pos), pl.BlockSpec((tm, tn), pos), pl.BlockSpec((tm, tn), pos)],
        out_specs=pl.BlockSpec((tm, tn), lambda n, i: (i, n)),
        out_shape=jax.ShapeDtypeStruct((t, n_out), F32),
        scratch_shapes=[pltpu.VMEM((k, tn), BF16)],
        compiler_params=_params(2),
        name="qkv_rope",
    )(h, w_stack, *tables)


def _rope_tables(l_lat, width):
    rows = l_lat // GRID_W
    row_pos = jnp.repeat(jnp.arange(rows, dtype=F32), GRID_W)
    col_pos = jnp.tile(jnp.arange(GRID_W, dtype=F32), rows)
    inv_freq = ROPE_BASE ** (-jnp.arange(0, ROPE_AXIS_DIM, 2, dtype=F32) / ROPE_AXIS_DIM)

    def tab(p):
        ang = p[:, None] * inv_freq[None, :]
        ang = jnp.concatenate([ang, ang], -1)
        return jnp.cos(ang), jnp.sin(ang)

    cr, sr = tab(row_pos)
    cc, sc = tab(col_pos)
    cos = jnp.concatenate([cr, cc], -1)
    sin = jnp.concatenate([sr, sc], -1)
    first_half = (jnp.arange(ATTN_HEAD_DIM) % ROPE_AXIS_DIM) < (ROPE_AXIS_DIM // 2)
    sin_a = jnp.where(first_half, -sin, 0.0)
    sin_b = jnp.where(first_half, 0.0, sin)
    rep = width // ATTN_HEAD_DIM
    return tuple(jnp.tile(a, (1, rep)) for a in (cos, sin_a, sin_b))


def _attn_body(lam_ref, g_ref, q_ref, k_ref, v_ref, *refs, lam_init, has_cache):
    if has_cache:
        kc_ref, vc_ref, o_ref = refs
    else:
        (o_ref,) = refs
    lp = lam_ref[...]
    lam = (jnp.exp(jnp.sum(lp[0:1] * lp[1:2], axis=-1, keepdims=True))
           - jnp.exp(jnp.sum(lp[2:3] * lp[3:4], axis=-1, keepdims=True)) + lam_init)
    q = q_ref[...].astype(BF16)
    k = k_ref[...]
    v = v_ref[...]
    if has_cache:
        k = jnp.concatenate([kc_ref[...], k], axis=0)
        v = jnp.concatenate([vc_ref[...], v], axis=0)
    k = k.astype(BF16)
    v = v.astype(BF16)
    dh = ATTN_HEAD_DIM
    scale = dh ** -0.5

    def softmax_map(m):
        s = _bdot_nt(q[:, m * dh:(m + 1) * dh], k[:, m * dh:(m + 1) * dh]) * scale
        e = jnp.exp(s - jnp.max(s, axis=-1, keepdims=True))
        return e / jnp.sum(e, axis=-1, keepdims=True)

    a = softmax_map(0) - lam * softmax_map(1)
    o = _bdot(a, v)
    o = o * lax.rsqrt(jnp.mean(o * o, axis=-1, keepdims=True) + LN_EPS) * g_ref[...]
    o_ref[...] = o * (1.0 - lam_init)


def _diff_attention(qkv, lam_p, subln_g, layer, lam_init, row0, n_batch, seq, cache=None):
    t = qkv.shape[0]
    width = 2 * ATTN_HEAD_DIM
    tq = 128
    nq = seq // tq
    q_base, kv_base = row0 // tq, row0 // seq
    in_specs = [pl.BlockSpec((None, 4, ATTN_HEAD_DIM), lambda b, h, i: (layer, 0, 0)),
                pl.BlockSpec((None, 1, width), lambda b, h, i: (layer, 0, 0)),
                pl.BlockSpec((tq, width), lambda b, h, i: (q_base + b * nq + i, h)),
                pl.BlockSpec((seq, width), lambda b, h, i: (kv_base + b, ATTN_HEADS + h)),
                pl.BlockSpec((seq, width), lambda b, h, i: (kv_base + b, 2 * ATTN_HEADS + h))]
    args = [lam_p, subln_g.reshape(subln_g.shape[0], 1, width), qkv, qkv, qkv]
    if cache is not None:
        ck, cv = cache
        past = ck.shape[2]
        spec = pl.BlockSpec((None, None, past, width), lambda b, h, i: (b, layer, 0, h))
        in_specs += [spec, spec]
        args += [ck, cv]
    return pl.pallas_call(
        functools.partial(_attn_body, lam_init=lam_init, has_cache=cache is not None),
        grid=(n_batch, ATTN_HEADS, nq),
        in_specs=in_specs,
        out_specs=pl.BlockSpec((tq, width), lambda b, h, i: (b * nq + i, h)),
        out_shape=jax.ShapeDtypeStruct((n_batch * seq, ATTN_HEADS * width), F32),
        compiler_params=_params(3),
        name="diff_attention",
    )(*args)


def _conv_silu(x, w, b):
    n = x.shape[0]
    row = lax.broadcasted_iota(jnp.int32, x.shape, 0)
    acc = x * w[SSM_CONV // 2:SSM_CONV // 2 + 1] + b
    for k in range(SSM_CONV):
        sh = k - SSM_CONV // 2
        if sh == 0:
            continue
        shifted = pltpu.roll(x, (-sh) % n, 0)
        valid = jnp.logical_and(row + sh >= 0, row + sh < n)
        acc = acc + jnp.where(valid, shifted, 0.0) * w[k:k + 1]
    return _silu(acc)


def _softplus(x):
    return jnp.maximum(x, 0.0) + jnp.log1p(jnp.exp(-jnp.abs(x)))


def _ssd_body(*refs, has_h0, emit_state, n_chunks):
    (xs_ref, bm_ref, cm_ref, dt_ref, wx_ref, bx_ref, wb_ref, bb_ref, wc_ref, bc_ref,
     dtb_ref, a_ref, d_ref) = refs[:13]
    refs = refs[13:]
    if has_h0:
        h0_ref, refs = refs[0], refs[1:]
    y_ref, refs = refs[0], refs[1:]
    if emit_state:
        st_ref, refs = refs[0], refs[1:]
    xs_s, b_s, c_s, dt_s, da_s, h_s = refs
    q = SSD_CHUNK
    p = SSM_HEAD_DIM

    xs_s[...] = _conv_silu(xs_ref[...], wx_ref[...], bx_ref[...])
    b_s[...] = _conv_silu(bm_ref[...], wb_ref[...], bb_ref[...])
    c_s[...] = _conv_silu(cm_ref[...], wc_ref[...], bc_ref[...])
    dt = _softplus(dt_ref[...] + dtb_ref[...])
    dt_s[...] = dt
    da_s[...] = dt * a_ref[...]

    row = lax.broadcasted_iota(jnp.int32, (q, q), 0)
    col = lax.broadcasted_iota(jnp.int32, (q, q), 1)

    for direction in range(2):
        mask = (row >= col) if direction == 0 else (row <= col)
        tri = jnp.where(mask, 1.0, 0.0).astype(BF16)
        end = q - 1 if direction == 0 else 0
        for r in range(SSM_HPG):
            if has_h0:
                h_s[r] = h0_ref[direction, r]
            else:
                h_s[r] = jnp.zeros((p, SSM_STATE), F32)

        def chunk(ci, carry, direction=direction, mask=mask, tri=tri, end=end):
            c = ci if direction == 0 else n_chunks - 1 - ci
            rows = pl.ds(pl.multiple_of(c * q, q), q)
            xc = xs_s[rows, :]
            bc = b_s[rows, :].astype(BF16)
            cc = c_s[rows, :].astype(BF16)
            dtc = dt_s[rows, :]
            d_hi, d_mid, d_lo = _split3(da_s[rows, :])
            dot = functools.partial(jnp.dot, preferred_element_type=F32)
            acum = dot(tri, d_hi) + (dot(tri, d_mid) + dot(tri, d_lo))
            acum_t = acum.T
            dt_t = dtc.T
            x_t = xc.T
            cb = _bdot_nt(cc, bc)
            a_end = acum[end:end + 1, :]
            ys = []
            for r in range(SSM_HPG):
                hl = direction * SSM_HPG + r
                a_col = acum[:, hl:hl + 1]
                a_row = acum_t[hl:hl + 1, :]
                decay = jnp.exp(jnp.where(mask, a_col - a_row, NEG_BIG))
                xdt = xc[:, r * p:(r + 1) * p] * dtc[:, hl:hl + 1]
                h = h_s[r]
                y = _bdot(cb * decay, xdt) + jnp.exp(a_col) * _bdot_nt(cc, h)
                w = dt_t[hl:hl + 1, :] * jnp.exp(a_end[:, hl:hl + 1] - a_row)
                h_s[r] = jnp.exp(a_end[:, hl:hl + 1]) * h + _bdot(x_t[r * p:(r + 1) * p, :] * w, bc)
                ys.append(y)
            y = jnp.concatenate(ys, axis=1)
            if direction == 0:
                y_ref[rows, :] = y + d_ref[...] * xc
            else:
                y_ref[rows, :] = y_ref[rows, :] + y
            return carry

        lax.fori_loop(0, n_chunks, chunk, 0)
        if emit_state:
            for r in range(SSM_HPG):
                st_ref[direction, r] = h_s[r]


def _ssd(zxbc, dtp, conv_w, conv_b, dt_bias_l, a_l, d_l, row0, n_batch, seq, h0=None, layer=0,
         emit_state=False):
    inner, gs, hp = SSM_INNER, SSM_GROUPS * SSM_STATE, SSM_HPG * SSM_HEAD_DIM
    base = row0 // seq
    g_of = lambda off, w: (lambda b, g: (0, off // w + g))
    row_blk = lambda off, w: (lambda b, g: (base + b, off // w + g))
    in_specs = [pl.BlockSpec((seq, hp), row_blk(inner, hp)),
                pl.BlockSpec((seq, SSM_STATE), row_blk(2 * inner, SSM_STATE)),
                pl.BlockSpec((seq, SSM_STATE), row_blk(2 * inner + gs, SSM_STATE)),
                pl.BlockSpec((seq, 128), row_blk(0, 128)),
                pl.BlockSpec((SSM_CONV, hp), g_of(0, hp)), pl.BlockSpec((1, hp), g_of(0, hp)),
                pl.BlockSpec((SSM_CONV, SSM_STATE), g_of(inner, SSM_STATE)),
                pl.BlockSpec((1, SSM_STATE), g_of(inner, SSM_STATE)),
                pl.BlockSpec((SSM_CONV, SSM_STATE), g_of(inner + gs, SSM_STATE)),
                pl.BlockSpec((1, SSM_STATE), g_of(inner + gs, SSM_STATE)),
                pl.BlockSpec((1, 128), g_of(0, 128)), pl.BlockSpec((1, 128), g_of(0, 128)),
                pl.BlockSpec((1, hp), g_of(0, hp))]
    args = [zxbc, zxbc, zxbc, dtp, conv_w, conv_b, conv_w, conv_b, conv_w, conv_b, dt_bias_l, a_l, d_l]
    if h0 is not None:
        in_specs.append(pl.BlockSpec((None, None, 2, SSM_HPG, SSM_HEAD_DIM, SSM_STATE),
                                     lambda b, g: (b, layer, 0, g, 0, 0)))
        args.append(h0)
    out_specs = [pl.BlockSpec((seq, hp), lambda b, g: (b, g))]
    out_shape = [jax.ShapeDtypeStruct((n_batch * seq, inner), F32)]
    if emit_state:
        out_specs.append(pl.BlockSpec((None, 2, SSM_HPG, SSM_HEAD_DIM, SSM_STATE),
                                      lambda b, g: (b, 0, g, 0, 0)))
        out_shape.append(jax.ShapeDtypeStruct(
            (n_batch, 2, SSM_GROUPS * SSM_HPG, SSM_HEAD_DIM, SSM_STATE), F32))
    return pl.pallas_call(
        functools.partial(_ssd_body, has_h0=h0 is not None, emit_state=emit_state,
                          n_chunks=seq // SSD_CHUNK),
        grid=(n_batch, SSM_GROUPS),
        in_specs=in_specs,
        out_specs=out_specs,
        out_shape=out_shape,
        scratch_shapes=[pltpu.VMEM((seq, hp), F32), pltpu.VMEM((seq, SSM_STATE), F32),
                        pltpu.VMEM((seq, SSM_STATE), F32), pltpu.VMEM((seq, 128), F32),
                        pltpu.VMEM((seq, 128), F32),
                        pltpu.VMEM((SSM_HPG, SSM_HEAD_DIM, SSM_STATE), F32)],
        compiler_params=_params(2),
        name="ssd_scan",
    )(*args)


def _ssm_out_body(y_ref, z_ref, g_ref, w_ref, o_ref, wbf):
    @pl.when(pl.program_id(0) == 0)
    def _():
        wbf[...] = w_ref[...].astype(BF16)

    y = y_ref[...] * _silu(z_ref[...])
    y = y * lax.rsqrt(jnp.mean(y * y, axis=-1, keepdims=True) + LN_EPS) * g_ref[...]
    o_ref[...] = jnp.dot(y.astype(BF16), wbf[...], preferred_element_type=F32)


def _ssm_out(y, zxbc, norm_g, w_out, layer):
    t, inner = y.shape
    d = w_out.shape[2]
    tm = ROW_TILE
    return pl.pallas_call(
        _ssm_out_body,
        grid=(t // tm,),
        in_specs=[pl.BlockSpec((tm, inner), lambda i: (i, 0)),
                  pl.BlockSpec((tm, inner), lambda i: (i, 0)),
                  pl.BlockSpec((None, 1, inner), lambda i: (layer, 0, 0)),
                  pl.BlockSpec((None, inner, d), lambda i: (layer, 0, 0))],
        out_specs=pl.BlockSpec((tm, d), lambda i: (i, 0)),
        out_shape=jax.ShapeDtypeStruct((t, d), F32),
        scratch_shapes=[pltpu.VMEM((inner, d), BF16)],
        compiler_params=_params(1),
        name="ssm_gate_norm_out",
    )(y, zxbc, norm_g.reshape(norm_g.shape[0], 1, inner), w_out)


def _gmlp_body(u_ref, v_ref, g_ref, b_ref, ws_ref, bs_ref, w_ref, o_ref, wbf, t_s, *, n_chunks):
    @pl.when(pl.program_id(0) == 0)
    def _():
        wbf[...] = w_ref[...].astype(BF16)

    v = v_ref[...]
    mu = jnp.mean(v, axis=-1, keepdims=True)
    vc = v - mu
    var = jnp.mean(vc * vc, axis=-1, keepdims=True)
    vn = (vc * lax.rsqrt(var + LN_EPS) * g_ref[...] + b_ref[...]).astype(BF16)
    q, gd = GMLP_CHUNK, GMLP_GROUP_DIM
    for g in range(GMLP_GROUPS):
        ws = ws_ref[g].astype(BF16)
        bias = bs_ref[:, g:g + 1]
        for c in range(n_chunks):
            sv = jnp.dot(ws, vn[c * q:(c + 1) * q, g * gd:(g + 1) * gd],
                         preferred_element_type=F32) + bias
            t_s[c * q:(c + 1) * q, g * gd:(g + 1) * gd] = (
                u_ref[c * q:(c + 1) * q, g * gd:(g + 1) * gd] * sv).astype(BF16)
    o_ref[...] = jnp.dot(t_s[...], wbf[...], preferred_element_type=F32)


def _gmlp_mix_out(uv, ln_g, ln_b, w_s, b_s, w_out, layer):
    t = uv.shape[0]
    half, d = GMLP_HALF, w_out.shape[2]
    tm = 512
    return pl.pallas_call(
        functools.partial(_gmlp_body, n_chunks=tm // GMLP_CHUNK),
        grid=(t // tm,),
        in_specs=[pl.BlockSpec((tm, half), lambda i: (i, 0)),
                  pl.BlockSpec((tm, half), lambda i: (i, 1)),
                  pl.BlockSpec((None, 1, half), lambda i: (layer, 0, 0)),
                  pl.BlockSpec((None, 1, half), lambda i: (layer, 0, 0)),
                  pl.BlockSpec((None, GMLP_GROUPS, GMLP_CHUNK, GMLP_CHUNK), lambda i: (layer, 0, 0, 0)),
                  pl.BlockSpec((None, GMLP_CHUNK, GMLP_GROUPS), lambda i: (layer, 0, 0)),
                  pl.BlockSpec((None, half, d), lambda i: (layer, 0, 0))],
        out_specs=pl.BlockSpec((tm, d), lambda i: (i, 0)),
        out_shape=jax.ShapeDtypeStruct((t, d), F32),
        scratch_shapes=[pltpu.VMEM((half, d), BF16), pltpu.VMEM((tm, half), BF16)],
        compiler_params=_params(1),
        name="gmlp_mix_out",
    )(uv, uv, ln_g.reshape(-1, 1, half), ln_b.reshape(-1, 1, half), w_s,
      jnp.swapaxes(b_s, 1, 2), w_out)


def _top_rows(s, n):
    slot = lax.broadcasted_iota(jnp.int32, (n, s.shape[1]), 0)
    out = jnp.zeros((n, s.shape[1]), F32)
    cur = s
    for k in range(n):
        m = jnp.max(cur, axis=0, keepdims=True)
        out = jnp.where(slot == k, m, out)
        cur = jnp.where(cur >= m, NEG_BIG, cur)
    return out


def _peer_score_body(ht_ref, wq_ref, keys_ref, s1_ref, e1_ref, s2_ref, e2_ref, tau_ref, wbf):
    @pl.when(pl.program_id(0) == 0)
    def _():
        wbf[...] = wq_ref[...].astype(BF16)

    ht = ht_ref[...]
    kk, kh = PEER_TOPK, PEER_HALF
    for h in range(PEER_HEADS):
        qh = jnp.dot(wbf[2 * kh * h:2 * kh * (h + 1), :], ht, preferred_element_type=F32)
        s1 = _dot_x3(keys_ref[h, 0], qh[:kh])
        s2 = _dot_x3(keys_ref[h, 1], qh[kh:])
        t1 = _top_rows(s1, kk)
        t2 = _top_rows(s2, kk)
        rank = lax.broadcasted_iota(jnp.int32, t1.shape, 0)
        cand = jnp.concatenate(
            [t1[a:a + 1] + t2 for a in range(4)]
            + [jnp.where(rank >= 4, t1 + t2[b:b + 1], NEG_BIG) for b in range(3)], axis=0)
        best = _top_rows(cand, kk)
        z = jnp.sum(jnp.exp(best - best[0:1]), axis=0, keepdims=True)
        s1_ref[h] = s1
        s2_ref[h] = s2
        e1_ref[h] = jnp.exp(s1 - t1[0:1])
        e2_ref[h] = jnp.exp(s2 - t2[0:1]) / z
        tau_ref[h] = best[kk - 1:kk]


def _peer_scores(ht, wq_t, sub_keys, layer):
    d, t = ht.shape
    tt = PEER_TOKEN_TILE
    nq = wq_t.shape[1]
    big = pl.BlockSpec((PEER_HEADS, PEER_KEYS, tt), lambda i: (0, 0, i))
    shape = jax.ShapeDtypeStruct((PEER_HEADS, PEER_KEYS, t), F32)
    return pl.pallas_call(
        _peer_score_body,
        grid=(t // tt,),
        in_specs=[pl.BlockSpec((d, tt), lambda i: (0, i)),
                  pl.BlockSpec((None, nq, d), lambda i: (layer, 0, 0)),
                  pl.BlockSpec((None, PEER_HEADS, 2, PEER_KEYS, PEER_HALF), lambda i: (layer, 0, 0, 0, 0))],
        out_specs=[big, big, big, big, pl.BlockSpec((PEER_HEADS, 1, tt), lambda i: (0, 0, i))],
        out_shape=[shape, shape, shape, shape, jax.ShapeDtypeStruct((PEER_HEADS, 1, t), F32)],
        scratch_shapes=[pltpu.VMEM((nq, d), BF16)],
        compiler_params=_params(1),
        name="peer_scores",
    )(ht, wq_t, sub_keys)


def _peer_mix_body(ht_ref, s1_ref, e1_ref, tau_ref, s2_ref, e2_ref, u_ref, v_ref, o_ref, ubf, vtbf):
    @pl.when(pl.program_id(1) == 0)
    def _():
        o_ref[...] = jnp.zeros(o_ref.shape, F32)

    ubf[...] = u_ref[...].astype(BF16)
    vtbf[...] = v_ref[...].T.astype(BF16)
    tt = PEER_SUB_TILE

    def sub(j, carry):
        cols = pl.ds(pl.multiple_of(j * tt, tt), tt)
        act = jnp.dot(ubf[...], ht_ref[:, cols], preferred_element_type=F32)
        coefs = []
        for r in range(PEER_ROWS_PER_BLOCK):
            gate = jnp.zeros((PEER_KEYS, tt), F32)
            for h in range(PEER_HEADS):
                chosen = s2_ref[h, :, cols] + s1_ref[h, 0, r:r + 1, cols] >= tau_ref[h, :, cols]
                gate = gate + jnp.where(chosen, e2_ref[h, :, cols], 0.0) * e1_ref[h, 0, r:r + 1, cols]
            coefs.append((_gelu(act[r * PEER_KEYS:(r + 1) * PEER_KEYS]) * gate).astype(BF16))
        coef = jnp.concatenate(coefs, axis=0)
        o_ref[:, cols] = o_ref[:, cols] + jnp.dot(vtbf[...], coef, preferred_element_type=F32)
        return carry

    lax.fori_loop(0, o_ref.shape[1] // tt, sub, 0)


def _peer_mix(ht, s1, e1, s2, e2, tau, u_tab, v_tab, layer):
    d, t = ht.shape
    ts, rb = PEER_SUPER_TILE, PEER_ROWS_PER_BLOCK
    eb = rb * PEER_KEYS
    n_blocks = u_tab.shape[1] // eb
    rows = lambda a: a.reshape(PEER_HEADS, n_blocks, rb, t)
    row_spec = pl.BlockSpec((PEER_HEADS, 1, rb, ts), lambda s, e: (0, e, 0, s))
    tok_spec = pl.BlockSpec((PEER_HEADS, PEER_KEYS, ts), lambda s, e: (0, 0, s))
    tab_spec = pl.BlockSpec((None, eb, d), lambda s, e: (layer, e, 0))
    return pl.pallas_call(
        _peer_mix_body,
        grid=(t // ts, n_blocks),
        in_specs=[pl.BlockSpec((d, ts), lambda s, e: (0, s)), row_spec, row_spec,
                  pl.BlockSpec((PEER_HEADS, 1, ts), lambda s, e: (0, 0, s)),
                  tok_spec, tok_spec, tab_spec, tab_spec],
        out_specs=pl.BlockSpec((d, ts), lambda s, e: (0, s)),
        out_shape=jax.ShapeDtypeStruct((d, t), F32),
        scratch_shapes=[pltpu.VMEM((eb, d), BF16), pltpu.VMEM((d, eb), BF16)],
        compiler_params=_params(2),
        name="peer_mix",
    )(ht, rows(s1), rows(e1), tau, s2, e2, u_tab, v_tab)


def kernel(x_prompt, x_sample, cache_k, cache_v, state_ssm, c, c_ctx, w_mod, b_mod, ln_g, ln_b, attn_w_qkv, attn_w_o, attn_lambda, attn_subln_g, ssm_w_in, ssm_conv_w, ssm_conv_b, ssm_dt_bias, ssm_a_log, ssm_d, ssm_norm_g, ssm_w_out, gmlp_w_in, gmlp_b_in, gmlp_ln_g, gmlp_ln_b, gmlp_w_s, gmlp_b_s, gmlp_w_out, peer_w_q, peer_sub_keys, peer_u, peer_v):
    bc, lc, d = x_prompt.shape
    bl, ll, _ = x_sample.shape
    depth = w_mod.shape[0]
    t_ctx, t_lat = bc * lc, bl * ll
    alpha = (2 * depth) ** 0.25
    assert 1 + bl <= N_MOD_GROUPS and t_ctx % ll == 0

    x = jnp.concatenate([x_prompt.reshape(t_ctx, d), x_sample.reshape(t_lat, d)], axis=0)
    cond = jnp.zeros((N_MOD_GROUPS, d), F32).at[0].set(c_ctx).at[1:1 + bl].set(c)
    mods = _modulation(cond, w_mod, b_mod).reshape(depth, N_MOD_GROUPS, 6, d)
    rope = _rope_tables(ll, attn_w_qkv.shape[2] // 3)
    past = cache_k.shape[2]
    ck = cache_k.reshape(bl, cache_k.shape[1], past, -1)
    cv = cache_v.reshape(bl, cache_v.shape[1], past, -1)
    wq_t = jnp.swapaxes(peer_w_q, 1, 2)

    new_k, new_v, new_s = [], [], []
    h = _modulate(x, mods[0], t_ctx, ll, shift=0, scale=1)
    for i in range(depth):
        kind, j = i % 3, i // 3
        if kind == 0:
            lam_init = 0.8 - 0.6 * math.exp(-0.3 * i)
            qkv = _qkv_rope(h, attn_w_qkv, j, rope, t_ctx, ll)
            width = qkv.shape[1] // 3
            o_ctx = _diff_attention(qkv, attn_lambda, attn_subln_g, j, lam_init, 0, bc, lc)
            o_lat = _diff_attention(qkv, attn_lambda, attn_subln_g, j, lam_init, t_ctx, bl, ll,
                                    cache=(ck, cv))
            o = _linear(jnp.concatenate([o_ctx, o_lat], axis=0).astype(BF16), attn_w_o, j, d)
            new_k.append(qkv[:t_ctx, width:2 * width].reshape(bc, lc, ATTN_HEADS, 2, ATTN_HEAD_DIM))
            new_v.append(qkv[:t_ctx, 2 * width:].reshape(bc, lc, ATTN_HEADS, 2 * ATTN_HEAD_DIM))
        elif kind == 1:
            n_main = SSM_INNER + SSM_INNER + 2 * SSM_GROUPS * SSM_STATE
            zxbc = _linear(h, ssm_w_in, j, n_main)
            def regroup(a):
                a = a.reshape(a.shape[:-1] + (2, SSM_GROUPS, SSM_HPG))
                a = jnp.moveaxis(a, -3, -2).reshape(a.shape[:-3] + (SSM_GROUPS, 2 * SSM_HPG))
                pad = [(0, 0)] * (a.ndim - 1) + [(0, 128 - 2 * SSM_HPG)]
                return jnp.pad(a, pad).reshape(a.shape[:-2] + (SSM_GROUPS * 128,))
            w_dt = regroup(ssm_w_in[j][:, n_main:])[None]
            dtp = _linear(h, w_dt, 0, SSM_GROUPS * 128)
            dtb = regroup(ssm_dt_bias[j].reshape(1, -1))
            a_l = regroup(-jnp.exp(ssm_a_log[j]).reshape(1, -1))
            d_l = jnp.repeat(ssm_d[j], SSM_HEAD_DIM).reshape(1, SSM_INNER)
            cw, cb = ssm_conv_w[j], ssm_conv_b[j].reshape(1, -1)
            y_ctx, st = _ssd(zxbc, dtp, cw, cb, dtb, a_l, d_l, 0, bc, lc, emit_state=True)
            (y_lat,) = _ssd(zxbc, dtp, cw, cb, dtb, a_l, d_l, t_ctx, bl, ll, h0=state_ssm, layer=j)
            o = _ssm_out(jnp.concatenate([y_ctx, y_lat], axis=0), zxbc, ssm_norm_g, ssm_w_out, j)
            new_s.append(st)
        else:
            uv = _linear(h, gmlp_w_in, j, 2 * GMLP_HALF, bias=gmlp_b_in[j], act="gelu")
            o = _gmlp_mix_out(uv, gmlp_ln_g, gmlp_ln_b, gmlp_w_s, gmlp_b_s, gmlp_w_out, j)
        x, ht = _res_ln(x, o, mods[i], mods[i], ln_g[i, 0], ln_b[i, 0], t_ctx, ll, alpha=alpha,
                        gate=2, nshift=3, nscale=4, emit_ht=True)
        s1, e1, s2, e2, tau = _peer_scores(ht, wq_t, peer_sub_keys, i)
        f_t = _peer_mix(ht, s1, e1, s2, e2, tau, peer_u, peer_v, i)
        if i + 1 < depth:
            x, h = _res_ln(x, f_t, mods[i], mods[i + 1], ln_g[i, 1], ln_b[i, 1], t_ctx, ll,
                           alpha=alpha, gate=5, nshift=0, nscale=1, o_transposed=True, emit_h=True)
        else:
            (x,) = _res_ln(x, f_t, mods[i], mods[i], ln_g[i, 1], ln_b[i, 1], t_ctx, ll,
                           alpha=alpha, gate=5, o_transposed=True)
    return (x[:t_ctx].reshape(bc, lc, d), x[t_ctx:].reshape(bl, ll, d),
            jnp.stack(new_k, axis=1), jnp.stack(new_v, axis=1), jnp.stack(new_s, axis=1))
```

```python
import functools
import math

import jax
import jax.numpy as jnp
from jax import lax
from jax.experimental import pallas as pl
from jax.experimental.pallas import tpu as pltpu

F32 = jnp.float32
BF16 = jnp.bfloat16

D_MODEL = 1024
LN_EPS = 1e-5
GRID_W = 64
ATTN_HEADS = 8
ATTN_HEAD_DIM = 64
ROPE_AXIS_DIM = ATTN_HEAD_DIM // 2
ROPE_BASE = 10000.0
SSM_INNER = 2 * D_MODEL
SSM_HEAD_DIM = 64
SSM_GROUPS = 8
SSM_HPG = 4
SSM_STATE = 128
SSM_CONV = 5
SSD_CHUNK = 128
GMLP_HALF = 2 * D_MODEL
GMLP_GROUPS = 8
GMLP_GROUP_DIM = GMLP_HALF // GMLP_GROUPS
GMLP_CHUNK = 128
PEER_HEADS = 8
PEER_KEYS = 128
PEER_HALF = 128
PEER_TOPK = 16
N_MOD_GROUPS = 8
NEG_BIG = -1e30

VMEM_LIMIT = 56 * 1024 * 1024
ROW_TILE = 256
MM_ROW_TILE = 512
MM_COL_TILE = 1024
PEER_TOKEN_TILE = 256
PEER_SUPER_TILE = 1024
PEER_ROWS_PER_BLOCK = 8


def _params(n_axes):
    return pltpu.CompilerParams(dimension_semantics=("arbitrary",) * n_axes,
                                vmem_limit_bytes=VMEM_LIMIT)


def _bdot(a, b):
    return jnp.dot(a.astype(BF16), b.astype(BF16), preferred_element_type=F32)


def _bdot_nt(a, b):
    return lax.dot_general(a.astype(BF16), b.astype(BF16), (((1,), (1,)), ((), ())),
                           preferred_element_type=F32)


def _split3(a):
    hi = a.astype(BF16)
    r = a - hi.astype(F32)
    mid = r.astype(BF16)
    lo = (r - mid.astype(F32)).astype(BF16)
    return hi, mid, lo


def _dot_x3(a, b):
    ah, am, _ = _split3(a)
    bh, bm, _ = _split3(b)
    d = functools.partial(jnp.dot, preferred_element_type=F32)
    return d(ah, bh) + (d(am, bh) + d(ah, bm))


def _silu(x):
    return x * jax.nn.sigmoid(x)


def _gelu(x):
    return 0.5 * x * (1.0 + jnp.tanh(math.sqrt(2.0 / math.pi) * (x + 0.044715 * (x * x * x))))


def _group_index(i, tm, t_ctx, l_lat):
    start = i * tm
    return jnp.where(start < t_ctx, 0, 1 + (start - t_ctx) // l_lat)


def _mod_body(c_ref, w_ref, b_ref, o_ref):
    o_ref[0] = _dot_x3(_silu(c_ref[...]), w_ref[0]) + b_ref[0]


def _modulation(cond, w_mod, b_mod):
    depth, d, n = w_mod.shape
    tn = 1536
    return pl.pallas_call(
        _mod_body,
        grid=(depth, n // tn),
        in_specs=[pl.BlockSpec((N_MOD_GROUPS, d), lambda l, j: (0, 0)),
                  pl.BlockSpec((1, d, tn), lambda l, j: (l, 0, j)),
                  pl.BlockSpec((1, 1, tn), lambda l, j: (l, 0, j))],
        out_specs=pl.BlockSpec((1, N_MOD_GROUPS, tn), lambda l, j: (l, 0, j)),
        out_shape=jax.ShapeDtypeStruct((depth, N_MOD_GROUPS, n), F32),
        compiler_params=_params(2),
        name="modulation",
    )(cond, w_mod, b_mod.reshape(depth, 1, n))


def _modulate_body(x_ref, m_ref, h_ref, *, shift, scale):
    m = m_ref[0]
    h_ref[...] = (x_ref[...] * (1.0 + m[scale:scale + 1]) + m[shift:shift + 1]).astype(BF16)


def _modulate(x, mod, t_ctx, l_lat, shift, scale):
    t, d = x.shape
    tm = ROW_TILE
    return pl.pallas_call(
        functools.partial(_modulate_body, shift=shift, scale=scale),
        grid=(t // tm,),
        in_specs=[pl.BlockSpec((tm, d), lambda i: (i, 0)),
                  pl.BlockSpec((1, 6, d), lambda i: (_group_index(i, tm, t_ctx, l_lat), 0, 0))],
        out_specs=pl.BlockSpec((tm, d), lambda i: (i, 0)),
        out_shape=jax.ShapeDtypeStruct((t, d), BF16),
        compiler_params=_params(1),
        name="modulate",
    )(x, mod)


def _res_ln_body(x_ref, o_ref, m_ref, mn_ref, g_ref, b_ref, *outs,
                 alpha, gate, nshift, nscale, o_transposed, emit_h, emit_ht):
    o = o_ref[...]
    if o_transposed:
        o = o.T
    m = m_ref[0]
    y = alpha * x_ref[...] + m[gate:gate + 1] * o
    mu = jnp.mean(y, axis=-1, keepdims=True)
    yc = y - mu
    var = jnp.mean(yc * yc, axis=-1, keepdims=True)
    xn = yc * lax.rsqrt(var + LN_EPS) * g_ref[...] + b_ref[...]
    outs[0][...] = xn
    if emit_h or emit_ht:
        mn = mn_ref[0]
        h = xn * (1.0 + mn[nscale:nscale + 1]) + mn[nshift:nshift + 1]
        k = 1
        if emit_h:
            outs[k][...] = h.astype(BF16)
            k += 1
        if emit_ht:
            outs[k][...] = h.T.astype(BF16)


def _res_ln(x, o, mod, mod_next, ln_g, ln_b, t_ctx, l_lat, *, alpha, gate, nshift=0, nscale=1,
            o_transposed=False, emit_h=False, emit_ht=False):
    t, d = x.shape
    tm = ROW_TILE
    grp = lambda i: (_group_index(i, tm, t_ctx, l_lat), 0, 0)
    o_spec = (pl.BlockSpec((d, tm), lambda i: (0, i)) if o_transposed
              else pl.BlockSpec((tm, d), lambda i: (i, 0)))
    out_specs = [pl.BlockSpec((tm, d), lambda i: (i, 0))]
    out_shape = [jax.ShapeDtypeStruct((t, d), F32)]
    if emit_h:
        out_specs.append(pl.BlockSpec((tm, d), lambda i: (i, 0)))
        out_shape.append(jax.ShapeDtypeStruct((t, d), BF16))
    if emit_ht:
        out_specs.append(pl.BlockSpec((d, tm), lambda i: (0, i)))
        out_shape.append(jax.ShapeDtypeStruct((d, t), BF16))
    return pl.pallas_call(
        functools.partial(_res_ln_body, alpha=alpha, gate=gate, nshift=nshift, nscale=nscale,
                          o_transposed=o_transposed, emit_h=emit_h, emit_ht=emit_ht),
        grid=(t // tm,),
        in_specs=[pl.BlockSpec((tm, d), lambda i: (i, 0)), o_spec,
                  pl.BlockSpec((1, 6, d), grp), pl.BlockSpec((1, 6, d), grp),
                  pl.BlockSpec((1, d), lambda i: (0, 0)), pl.BlockSpec((1, d), lambda i: (0, 0))],
        out_specs=out_specs,
        out_shape=out_shape,
        compiler_params=_params(1),
        name="residual_layernorm",
    )(x, o, mod, mod_next, ln_g.reshape(1, d), ln_b.reshape(1, d))


def _linear_body(h_ref, w_ref, *refs, has_bias, act):
    if has_bias:
        b_ref, o_ref, wbf = refs
    else:
        o_ref, wbf = refs

    @pl.when(pl.program_id(1) == 0)
    def _():
        wbf[...] = w_ref[...].astype(BF16)

    y = jnp.dot(h_ref[...], wbf[...], preferred_element_type=F32)
    if has_bias:
        y = y + b_ref[...]
    if act == "gelu":
        y = _gelu(y)
    o_ref[...] = y


def _linear(h, w_stack, layer, n_out, bias=None, act=None):
    t, k = h.shape
    tm, tn = MM_ROW_TILE, min(MM_COL_TILE, n_out)
    in_specs = [pl.BlockSpec((tm, k), lambda n, i: (i, 0)),
                pl.BlockSpec((None, k, tn), lambda n, i: (layer, 0, n))]
    args = [h, w_stack]
    if bias is not None:
        in_specs.append(pl.BlockSpec((1, tn), lambda n, i: (0, n)))
        args.append(bias.reshape(1, n_out))
    return pl.pallas_call(
        functools.partial(_linear_body, has_bias=bias is not None, act=act),
        grid=(n_out // tn, t // tm),
        in_specs=in_specs,
        out_specs=pl.BlockSpec((tm, tn), lambda n, i: (i, n)),
        out_shape=jax.ShapeDtypeStruct((t, n_out), F32),
        scratch_shapes=[pltpu.VMEM((k, tn), BF16)],
        compiler_params=_params(2),
        name="linear",
    )(*args)


def _qkv_body(h_ref, w_ref, cos_ref, sa_ref, sb_ref, o_ref, wbf, *, n_ctx_tiles):
    n, i = pl.program_id(0), pl.program_id(1)

    @pl.when(i == 0)
    def _():
        wbf[...] = w_ref[...].astype(BF16)

    y = jnp.dot(h_ref[...], wbf[...], preferred_element_type=F32)
    rope = jnp.logical_and(n < 2, i >= n_ctx_tiles)
    half = ROPE_AXIS_DIM // 2

    @pl.when(rope)
    def _():
        up = pltpu.roll(y, y.shape[1] - half, 1)
        dn = pltpu.roll(y, half, 1)
        o_ref[...] = y * cos_ref[...] + up * sa_ref[...] + dn * sb_ref[...]

    @pl.when(jnp.logical_not(rope))
    def _():
        o_ref[...] = y


def _qkv_rope(h, w_stack, layer, tables, t_ctx, l_lat):
    t, k = h.shape
    n_out = w_stack.shape[2]
    tm, tn = MM_ROW_TILE, MM_COL_TILE
    n_ctx_tiles = t_ctx // tm
    pos = lambda n, i: (jnp.maximum(i - n_ctx_tiles, 0) % (l_lat // tm), 0)
    return pl.pallas_call(
        functools.partial(_qkv_body, n_ctx_tiles=n_ctx_tiles),
        grid=(n_out // tn, t // tm),
        in_specs=[pl.BlockSpec((tm, k), lambda n, i: (i, 0)),
                  pl.BlockSpec((None, k, tn), lambda n, i: (layer, 0, n)),
                  pl.BlockSpec((tm, tn), pos), pl.BlockSpec((tm, tn), pos), pl.BlockSpec((tm, tn), pos)],
        out_specs=pl.BlockSpec((tm, tn), lambda n, i: (i, n)),
        out_shape=jax.ShapeDtypeStruct((t, n_out), F32),
        scratch_shapes=[pltpu.VMEM((k, tn), BF16)],
        compiler_params=_params(2),
        name="qkv_rope",
    )(h, w_stack, *tables)


def _rope_tables(l_lat, width):
    rows = l_lat // GRID_W
    row_pos = jnp.repeat(jnp.arange(rows, dtype=F32), GRID_W)
    col_pos = jnp.tile(jnp.arange(GRID_W, dtype=F32), rows)
    inv_freq = ROPE_BASE ** (-jnp.arange(0, ROPE_AXIS_DIM, 2, dtype=F32) / ROPE_AXIS_DIM)

    def tab(p):
        ang = p[:, None] * inv_freq[None, :]
        ang = jnp.concatenate([ang, ang], -1)
        return jnp.cos(ang), jnp.sin(ang)

    cr, sr = tab(row_pos)
    cc, sc = tab(col_pos)
    cos = jnp.concatenate([cr, cc], -1)
    sin = jnp.concatenate([sr, sc], -1)
    first_half = (jnp.arange(ATTN_HEAD_DIM) % ROPE_AXIS_DIM) < (ROPE_AXIS_DIM // 2)
    sin_a = jnp.where(first_half, -sin, 0.0)
    sin_b = jnp.where(first_half, 0.0, sin)
    rep = width // ATTN_HEAD_DIM
    return tuple(jnp.tile(a, (1, rep)) for a in (cos, sin_a, sin_b))


def _attn_body(lam_ref, g_ref, q_ref, k_ref, v_ref, *refs, lam_init, has_cache):
    if has_cache:
        kc_ref, vc_ref, o_ref = refs
    else:
        (o_ref,) = refs
    lp = lam_ref[...]
    lam = (jnp.exp(jnp.sum(lp[0:1] * lp[1:2], axis=-1, keepdims=True))
           - jnp.exp(jnp.sum(lp[2:3] * lp[3:4], axis=-1, keepdims=True)) + lam_init)
    q = q_ref[...].astype(BF16)
    k = k_ref[...]
    v = v_ref[...]
    if has_cache:
        k = jnp.concatenate([kc_ref[...], k], axis=0)
        v = jnp.concatenate([vc_ref[...], v], axis=0)
    k = k.astype(BF16)
    v = v.astype(BF16)
    dh = ATTN_HEAD_DIM
    scale = dh ** -0.5

    def softmax_map(m):
        s = _bdot_nt(q[:, m * dh:(m + 1) * dh], k[:, m * dh:(m + 1) * dh]) * scale
        e = jnp.exp(s - jnp.max(s, axis=-1, keepdims=True))
        return e / jnp.sum(e, axis=-1, keepdims=True)

    a = softmax_map(0) - lam * softmax_map(1)
    o = _bdot(a, v)
    o = o * lax.rsqrt(jnp.mean(o * o, axis=-1, keepdims=True) + LN_EPS) * g_ref[...]
    o_ref[...] = o * (1.0 - lam_init)


def _diff_attention(qkv, lam_p, subln_g, layer, lam_init, row0, n_batch, seq, cache=None):
    t = qkv.shape[0]
    width = 2 * ATTN_HEAD_DIM
    tq = 128
    nq = seq // tq
    q_base, kv_base = row0 // tq, row0 // seq
    in_specs = [pl.BlockSpec((None, 4, ATTN_HEAD_DIM), lambda b, h, i: (layer, 0, 0)),
                pl.BlockSpec((None, 1, width), lambda b, h, i: (layer, 0, 0)),
                pl.BlockSpec((tq, width), lambda b, h, i: (q_base + b * nq + i, h)),
                pl.BlockSpec((seq, width), lambda b, h, i: (kv_base + b, ATTN_HEADS + h)),
                pl.BlockSpec((seq, width), lambda b, h, i: (kv_base + b, 2 * ATTN_HEADS + h))]
    args = [lam_p, subln_g.reshape(subln_g.shape[0], 1, width), qkv, qkv, qkv]
    if cache is not None:
        ck, cv = cache
        past = ck.shape[2]
        spec = pl.BlockSpec((None, None, past, width), lambda b, h, i: (b, layer, 0, h))
        in_specs += [spec, spec]
        args += [ck, cv]
    return pl.pallas_call(
        functools.partial(_attn_body, lam_init=lam_init, has_cache=cache is not None),
        grid=(n_batch, ATTN_HEADS, nq),
        in_specs=in_specs,
        out_specs=pl.BlockSpec((tq, width), lambda b, h, i: (b * nq + i, h)),
        out_shape=jax.ShapeDtypeStruct((n_batch * seq, ATTN_HEADS * width), F32),
        compiler_params=_params(3),
        name="diff_attention",
    )(*args)


def _conv_silu(x, w, b):
    n = x.shape[0]
    row = lax.broadcasted_iota(jnp.int32, x.shape, 0)
    acc = x * w[SSM_CONV // 2:SSM_CONV // 2 + 1] + b
    for k in range(SSM_CONV):
        sh = k - SSM_CONV // 2
        if sh == 0:
            continue
        shifted = pltpu.roll(x, (-sh) % n, 0)
        valid = jnp.logical_and(row + sh >= 0, row + sh < n)
        acc = acc + jnp.where(valid, shifted, 0.0) * w[k:k + 1]
    return _silu(acc)


def _softplus(x):
    return jnp.maximum(x, 0.0) + jnp.log1p(jnp.exp(-jnp.abs(x)))


def _ssd_body(*refs, has_h0, emit_state, n_chunks):
    (xs_ref, bm_ref, cm_ref, dt_ref, wx_ref, bx_ref, wb_ref, bb_ref, wc_ref, bc_ref,
     dtb_ref, a_ref, d_ref) = refs[:13]
    refs = refs[13:]
    if has_h0:
        h0_ref, refs = refs[0], refs[1:]
    y_ref, refs = refs[0], refs[1:]
    if emit_state:
        st_ref, refs = refs[0], refs[1:]
    xs_s, b_s, c_s, dt_s, da_s, h_s = refs
    q = SSD_CHUNK
    p = SSM_HEAD_DIM

    xs_s[...] = _conv_silu(xs_ref[...], wx_ref[...], bx_ref[...])
    b_s[...] = _conv_silu(bm_ref[...], wb_ref[...], bb_ref[...])
    c_s[...] = _conv_silu(cm_ref[...], wc_ref[...], bc_ref[...])
    dt = _softplus(dt_ref[...] + dtb_ref[...])
    dt_s[...] = dt
    da_s[...] = dt * a_ref[...]

    row = lax.broadcasted_iota(jnp.int32, (q, q), 0)
    col = lax.broadcasted_iota(jnp.int32, (q, q), 1)

    for direction in range(2):
        mask = (row >= col) if direction == 0 else (row <= col)
        tri = jnp.where(mask, 1.0, 0.0).astype(BF16)
        end = q - 1 if direction == 0 else 0
        for r in range(SSM_HPG):
            if has_h0:
                h_s[r] = h0_ref[direction, r]
            else:
                h_s[r] = jnp.zeros((p, SSM_STATE), F32)

        def chunk(ci, carry, direction=direction, mask=mask, tri=tri, end=end):
            c = ci if direction == 0 else n_chunks - 1 - ci
            rows = pl.ds(pl.multiple_of(c * q, q), q)
            xc = xs_s[rows, :]
            bc = b_s[rows, :].astype(BF16)
            cc = c_s[rows, :].astype(BF16)
            dtc = dt_s[rows, :]
            d_hi, d_mid, d_lo = _split3(da_s[rows, :])
            dot = functools.partial(jnp.dot, preferred_element_type=F32)
            acum = dot(tri, d_hi) + (dot(tri, d_mid) + dot(tri, d_lo))
            acum_t = acum.T
            dt_t = dtc.T
            x_t = xc.T
            cb = _bdot_nt(cc, bc)
            a_end = acum[end:end + 1, :]
            ys = []
            for r in range(SSM_HPG):
                hl = direction * SSM_HPG + r
                a_col = acum[:, hl:hl + 1]
                a_row = acum_t[hl:hl + 1, :]
                decay = jnp.exp(jnp.where(mask, a_col - a_row, NEG_BIG))
                xdt = xc[:, r * p:(r + 1) * p] * dtc[:, hl:hl + 1]
                h = h_s[r]
                y = _bdot(cb * decay, xdt) + jnp.exp(a_col) * _bdot_nt(cc, h)
                w = dt_t[hl:hl + 1, :] * jnp.exp(a_end[:, hl:hl + 1] - a_row)
                h_s[r] = jnp.exp(a_end[:, hl:hl + 1]) * h + _bdot(x_t[r * p:(r + 1) * p, :] * w, bc)
                ys.append(y)
            y = jnp.concatenate(ys, axis=1)
            if direction == 0:
                y_ref[rows, :] = y + d_ref[...] * xc
            else:
                y_ref[rows, :] = y_ref[rows, :] + y
            return carry

        lax.fori_loop(0, n_chunks, chunk, 0)
        if emit_state:
            for r in range(SSM_HPG):
                st_ref[direction, r] = h_s[r]


def _ssd(zxbc, dtp, conv_w, conv_b, dt_bias_l, a_l, d_l, row0, n_batch, seq, h0=None, layer=0,
         emit_state=False):
    inner, gs, hp = SSM_INNER, SSM_GROUPS * SSM_STATE, SSM_HPG * SSM_HEAD_DIM
    base = row0 // seq
    g_of = lambda off, w: (lambda b, g: (0, off // w + g))
    row_blk = lambda off, w: (lambda b, g: (base + b, off // w + g))
    in_specs = [pl.BlockSpec((seq, hp), row_blk(inner, hp)),
                pl.BlockSpec((seq, SSM_STATE), row_blk(2 * inner, SSM_STATE)),
                pl.BlockSpec((seq, SSM_STATE), row_blk(2 * inner + gs, SSM_STATE)),
                pl.BlockSpec((seq, 128), row_blk(0, 128)),
                pl.BlockSpec((SSM_CONV, hp), g_of(0, hp)), pl.BlockSpec((1, hp), g_of(0, hp)),
                pl.BlockSpec((SSM_CONV, SSM_STATE), g_of(inner, SSM_STATE)),
                pl.BlockSpec((1, SSM_STATE), g_of(inner, SSM_STATE)),
                pl.BlockSpec((SSM_CONV, SSM_STATE), g_of(inner + gs, SSM_STATE)),
                pl.BlockSpec((1, SSM_STATE), g_of(inner + gs, SSM_STATE)),
                pl.BlockSpec((1, 128), g_of(0, 128)), pl.BlockSpec((1, 128), g_of(0, 128)),
                pl.BlockSpec((1, hp), g_of(0, hp))]
    args = [zxbc, zxbc, zxbc, dtp, conv_w, conv_b, conv_w, conv_b, conv_w, conv_b, dt_bias_l, a_l, d_l]
    if h0 is not None:
        in_specs.append(pl.BlockSpec((None, None, 2, SSM_HPG, SSM_HEAD_DIM, SSM_STATE),
                                     lambda b, g: (b, layer, 0, g, 0, 0)))
        args.append(h0)
    out_specs = [pl.BlockSpec((seq, hp), lambda b, g: (b, g))]
    out_shape = [jax.ShapeDtypeStruct((n_batch * seq, inner), F32)]
    if emit_state:
        out_specs.append(pl.BlockSpec((None, 2, SSM_HPG, SSM_HEAD_DIM, SSM_STATE),
                                      lambda b, g: (b, 0, g, 0, 0)))
        out_shape.append(jax.ShapeDtypeStruct(
            (n_batch, 2, SSM_GROUPS * SSM_HPG, SSM_HEAD_DIM, SSM_STATE), F32))
    return pl.pallas_call(
        functools.partial(_ssd_body, has_h0=h0 is not None, emit_state=emit_state,
                          n_chunks=seq // SSD_CHUNK),
        grid=(n_batch, SSM_GROUPS),
        in_specs=in_specs,
        out_specs=out_specs,
        out_shape=out_shape,
        scratch_shapes=[pltpu.VMEM((seq, hp), F32), pltpu.VMEM((seq, SSM_STATE), F32),
                        pltpu.VMEM((seq, SSM_STATE), F32), pltpu.VMEM((seq, 128), F32),
                        pltpu.VMEM((seq, 128), F32),
                        pltpu.VMEM((SSM_HPG, SSM_HEAD_DIM, SSM_STATE), F32)],
        compiler_params=_params(2),
        name="ssd_scan",
    )(*args)


def _ssm_out_body(y_ref, z_ref, g_ref, w_ref, o_ref, wbf):
    @pl.when(pl.program_id(0) == 0)
    def _():
        wbf[...] = w_ref[...].astype(BF16)

    y = y_ref[...] * _silu(z_ref[...])
    y = y * lax.rsqrt(jnp.mean(y * y, axis=-1, keepdims=True) + LN_EPS) * g_ref[...]
    o_ref[...] = jnp.dot(y.astype(BF16), wbf[...], preferred_element_type=F32)


def _ssm_out(y, zxbc, norm_g, w_out, layer):
    t, inner = y.shape
    d = w_out.shape[2]
    tm = ROW_TILE
    return pl.pallas_call(
        _ssm_out_body,
        grid=(t // tm,),
        in_specs=[pl.BlockSpec((tm, inner), lambda i: (i, 0)),
                  pl.BlockSpec((tm, inner), lambda i: (i, 0)),
                  pl.BlockSpec((None, 1, inner), lambda i: (layer, 0, 0)),
                  pl.BlockSpec((None, inner, d), lambda i: (layer, 0, 0))],
        out_specs=pl.BlockSpec((tm, d), lambda i: (i, 0)),
        out_shape=jax.ShapeDtypeStruct((t, d), F32),
        scratch_shapes=[pltpu.VMEM((inner, d), BF16)],
        compiler_params=_params(1),
        name="ssm_gate_norm_out",
    )(y, zxbc, norm_g.reshape(norm_g.shape[0], 1, inner), w_out)


def _gmlp_body(u_ref, v_ref, g_ref, b_ref, ws_ref, bs_ref, w_ref, o_ref, wbf, t_s, *, n_chunks):
    @pl.when(pl.program_id(0) == 0)
    def _():
        wbf[...] = w_ref[...].astype(BF16)

    v = v_ref[...]
    mu = jnp.mean(v, axis=-1, keepdims=True)
    vc = v - mu
    var = jnp.mean(vc * vc, axis=-1, keepdims=True)
    vn = (vc * lax.rsqrt(var + LN_EPS) * g_ref[...] + b_ref[...]).astype(BF16)
    q, gd = GMLP_CHUNK, GMLP_GROUP_DIM
    for g in range(GMLP_GROUPS):
        ws = ws_ref[g].astype(BF16)
        bias = bs_ref[:, g:g + 1]
        for c in range(n_chunks):
            sv = jnp.dot(ws, vn[c * q:(c + 1) * q, g * gd:(g + 1) * gd],
                         preferred_element_type=F32) + bias
            t_s[c * q:(c + 1) * q, g * gd:(g + 1) * gd] = (
                u_ref[c * q:(c + 1) * q, g * gd:(g + 1) * gd] * sv).astype(BF16)
    o_ref[...] = jnp.dot(t_s[...], wbf[...], preferred_element_type=F32)


def _gmlp_mix_out(uv, ln_g, ln_b, w_s, b_s, w_out, layer):
    t = uv.shape[0]
    half, d = GMLP_HALF, w_out.shape[2]
    tm = 512
    return pl.pallas_call(
        functools.partial(_gmlp_body, n_chunks=tm // GMLP_CHUNK),
        grid=(t // tm,),
        in_specs=[pl.BlockSpec((tm, half), lambda i: (i, 0)),
                  pl.BlockSpec((tm, half), lambda i: (i, 1)),
                  pl.BlockSpec((None, 1, half), lambda i: (layer, 0, 0)),
                  pl.BlockSpec((None, 1, half), lambda i: (layer, 0, 0)),
                  pl.BlockSpec((None, GMLP_GROUPS, GMLP_CHUNK, GMLP_CHUNK), lambda i: (layer, 0, 0, 0)),
                  pl.BlockSpec((None, GMLP_CHUNK, GMLP_GROUPS), lambda i: (layer, 0, 0)),
                  pl.BlockSpec((None, half, d), lambda i: (layer, 0, 0))],
        out_specs=pl.BlockSpec((tm, d), lambda i: (i, 0)),
        out_shape=jax.ShapeDtypeStruct((t, d), F32),
        scratch_shapes=[pltpu.VMEM((half, d), BF16), pltpu.VMEM((tm, half), BF16)],
        compiler_params=_params(1),
        name="gmlp_mix_out",
    )(uv, uv, ln_g.reshape(-1, 1, half), ln_b.reshape(-1, 1, half), w_s,
      jnp.swapaxes(b_s, 1, 2), w_out)


def _top_rows(s, n, n_out):
    slot = lax.broadcasted_iota(jnp.int32, (n_out, s.shape[1]), 0)
    out = jnp.full((n_out, s.shape[1]), NEG_BIG, F32)
    cur = s
    for k in range(n):
        m = jnp.max(cur, axis=0, keepdims=True)
        out = jnp.where(slot == k, m, out)
        cur = jnp.where(cur >= m, NEG_BIG, cur)
    return out


def _peer_score_body(ht_ref, wq_ref, keys_ref, thr_ref, e1_ref, s2_ref, e2_ref, wbf):
    @pl.when(pl.program_id(0) == 0)
    def _():
        wbf[...] = wq_ref[...].astype(BF16)

    ht = ht_ref[...]
    kk, kh, sub = PEER_TOPK, PEER_HALF, PEER_ROWS_PER_BLOCK
    n_top = 3 * sub
    rank = lax.broadcasted_iota(jnp.int32, (kk, ht.shape[1]), 0)
    rank8 = lax.broadcasted_iota(jnp.int32, (sub, ht.shape[1]), 0)
    for h in range(PEER_HEADS):
        qh = jnp.dot(wbf[2 * kh * h:2 * kh * (h + 1), :], ht, preferred_element_type=F32)
        s1 = _dot_x3(keys_ref[h, 0], qh[:kh])
        s2 = _dot_x3(keys_ref[h, 1], qh[kh:])
        t1 = _top_rows(s1, kk + 1, n_top)
        t2 = _top_rows(s2, kk + 1, n_top)
        cand = jnp.concatenate(
            [t1[a:a + 1] + t2[:kk] for a in range(4)]
            + [jnp.where(rank >= 4, t1[:kk] + t2[b:b + 1], NEG_BIG) for b in range(3)]
            + [jnp.where(rank8 == 0, t1[kk:kk + 1] + t2[0:1],
                         jnp.where(rank8 == 1, t1[0:1] + t2[kk:kk + 1], NEG_BIG))], axis=0)
        best = _top_rows(cand, kk + 1, n_top)
        z = jnp.sum(jnp.exp(best[:kk] - best[0:1]), axis=0, keepdims=True)
        thr = 0.5 * (best[kk - 1:kk] + best[kk:kk + 1]) - s1
        e1 = jnp.exp(s1 - t1[0:1])
        for rb in range(PEER_KEYS // sub):
            thr_ref[h, rb] = thr[rb * sub:(rb + 1) * sub]
            e1_ref[h, rb] = e1[rb * sub:(rb + 1) * sub]
        s2_ref[h] = s2
        e2_ref[h] = jnp.exp(s2 - t2[0:1]) / z


def _peer_scores(ht, wq_t, sub_keys, layer):
    d, t = ht.shape
    tt = PEER_TOKEN_TILE
    nq = wq_t.shape[1]
    sub = PEER_ROWS_PER_BLOCK
    row_spec = pl.BlockSpec((PEER_HEADS, PEER_KEYS // sub, sub, tt), lambda i: (0, 0, 0, i))
    row_shape = jax.ShapeDtypeStruct((PEER_HEADS, PEER_KEYS // sub, sub, t), F32)
    tok_spec = pl.BlockSpec((PEER_HEADS, PEER_KEYS, tt), lambda i: (0, 0, i))
    tok_shape = jax.ShapeDtypeStruct((PEER_HEADS, PEER_KEYS, t), F32)
    return pl.pallas_call(
        _peer_score_body,
        grid=(t // tt,),
        in_specs=[pl.BlockSpec((d, tt), lambda i: (0, i)),
                  pl.BlockSpec((None, nq, d), lambda i: (layer, 0, 0)),
                  pl.BlockSpec((None, PEER_HEADS, 2, PEER_KEYS, PEER_HALF), lambda i: (layer, 0, 0, 0, 0))],
        out_specs=[row_spec, row_spec, tok_spec, tok_spec],
        out_shape=[row_shape, row_shape, tok_shape, tok_shape],
        scratch_shapes=[pltpu.VMEM((nq, d), BF16)],
        compiler_params=_params(1),
        name="peer_scores",
    )(ht, wq_t, sub_keys)


def _peer_tables_body(u_ref, v_ref, ub_ref, vt_ref):
    ub_ref[...] = u_ref[...].astype(BF16)
    vt_ref[...] = v_ref[...].T.astype(BF16)


def _peer_tables(u_tab, v_tab, layer):
    n, d = u_tab.shape[1:]
    eb = MM_COL_TILE
    return pl.pallas_call(
        _peer_tables_body,
        grid=(n // eb,),
        in_specs=[pl.BlockSpec((None, eb, d), lambda e: (layer, e, 0)),
                  pl.BlockSpec((None, eb, d), lambda e: (layer, e, 0))],
        out_specs=[pl.BlockSpec((eb, d), lambda e: (e, 0)), pl.BlockSpec((d, eb), lambda e: (0, e))],
        out_shape=[jax.ShapeDtypeStruct((n, d), BF16), jax.ShapeDtypeStruct((d, n), BF16)],
        compiler_params=_params(1),
        name="peer_tables",
    )(u_tab, v_tab)


def _peer_mix_body(ht_ref, thr_ref, e1_ref, s2_ref, e2_ref, u_ref, vt_ref, o_ref, act_s, coef_s):
    @pl.when(pl.program_id(1) == 0)
    def _():
        o_ref[...] = jnp.zeros(o_ref.shape, F32)

    lanes, kq = 128, 32
    n_q = PEER_KEYS // kq
    act_s[...] = jnp.dot(u_ref[...], ht_ref[...], preferred_element_type=F32)

    def gate_chunk(ci, carry):
        part, q = ci // n_q, ci % n_q
        c = pl.ds(pl.multiple_of(part * lanes, lanes), lanes)
        keys = pl.ds(pl.multiple_of(q * kq, kq), kq)
        gates = [jnp.zeros((kq, lanes), F32)] * PEER_ROWS_PER_BLOCK
        for h in range(PEER_HEADS):
            s2 = s2_ref[h, keys, c]
            e2 = e2_ref[h, keys, c]
            for r in range(PEER_ROWS_PER_BLOCK):
                chosen = s2 >= thr_ref[h, 0, r:r + 1, c]
                gates[r] = gates[r] + jnp.where(chosen, e2, 0.0) * e1_ref[h, 0, r:r + 1, c]
        for r in range(PEER_ROWS_PER_BLOCK):
            rows = pl.ds(pl.multiple_of(r * PEER_KEYS + q * kq, kq), kq)
            coef_s[rows, c] = (_gelu(act_s[rows, c]) * gates[r]).astype(BF16)
        return carry

    lax.fori_loop(0, (o_ref.shape[1] // lanes) * n_q, gate_chunk, 0)
    o_ref[...] = o_ref[...] + jnp.dot(vt_ref[...], coef_s[...], preferred_element_type=F32)


def _peer_mix(ht, thr, e1, s2, e2, u_bf, vt_bf):
    d, t = ht.shape
    ts, rb = PEER_SUPER_TILE, PEER_ROWS_PER_BLOCK
    eb = rb * PEER_KEYS
    n_blocks = u_bf.shape[0] // eb
    row_spec = pl.BlockSpec((PEER_HEADS, 1, rb, ts), lambda s, e: (0, e, 0, s))
    tok_spec = pl.BlockSpec((PEER_HEADS, PEER_KEYS, ts), lambda s, e: (0, 0, s))
    return pl.pallas_call(
        _peer_mix_body,
        grid=(t // ts, n_blocks),
        in_specs=[pl.BlockSpec((d, ts), lambda s, e: (0, s)), row_spec, row_spec, tok_spec, tok_spec,
                  pl.BlockSpec((eb, d), lambda s, e: (e, 0)), pl.BlockSpec((d, eb), lambda s, e: (0, e))],
        out_specs=pl.BlockSpec((d, ts), lambda s, e: (0, s)),
        out_shape=jax.ShapeDtypeStruct((d, t), F32),
        scratch_shapes=[pltpu.VMEM((eb, ts), F32), pltpu.VMEM((eb, ts), BF16)],
        compiler_params=_params(2),
        name="peer_mix",
    )(ht, thr, e1, s2, e2, u_bf, vt_bf)


def kernel(x_prompt, x_sample, cache_k, cache_v, state_ssm, c, c_ctx, w_mod, b_mod, ln_g, ln_b, attn_w_qkv, attn_w_o, attn_lambda, attn_subln_g, ssm_w_in, ssm_conv_w, ssm_conv_b, ssm_dt_bias, ssm_a_log, ssm_d, ssm_norm_g, ssm_w_out, gmlp_w_in, gmlp_b_in, gmlp_ln_g, gmlp_ln_b, gmlp_w_s, gmlp_b_s, gmlp_w_out, peer_w_q, peer_sub_keys, peer_u, peer_v):
    bc, lc, d = x_prompt.shape
    bl, ll, _ = x_sample.shape
    depth = w_mod.shape[0]
    t_ctx, t_lat = bc * lc, bl * ll
    alpha = (2 * depth) ** 0.25
    assert 1 + bl <= N_MOD_GROUPS and t_ctx % ll == 0

    x = jnp.concatenate([x_prompt.reshape(t_ctx, d), x_sample.reshape(t_lat, d)], axis=0)
    cond = jnp.zeros((N_MOD_GROUPS, d), F32).at[0].set(c_ctx).at[1:1 + bl].set(c)
    mods = _modulation(cond, w_mod, b_mod).reshape(depth, N_MOD_GROUPS, 6, d)
    rope = _rope_tables(ll, attn_w_qkv.shape[2] // 3)
    past = cache_k.shape[2]
    ck = cache_k.reshape(bl, cache_k.shape[1], past, -1)
    cv = cache_v.reshape(bl, cache_v.shape[1], past, -1)
    wq_t = jnp.swapaxes(peer_w_q, 1, 2)

    new_k, new_v, new_s = [], [], []
    h = _modulate(x, mods[0], t_ctx, ll, shift=0, scale=1)
    for i in range(depth):
        kind, j = i % 3, i // 3
        if kind == 0:
            lam_init = 0.8 - 0.6 * math.exp(-0.3 * i)
            qkv = _qkv_rope(h, attn_w_qkv, j, rope, t_ctx, ll)
            width = qkv.shape[1] // 3
            o_ctx = _diff_attention(qkv, attn_lambda, attn_subln_g, j, lam_init, 0, bc, lc)
            o_lat = _diff_attention(qkv, attn_lambda, attn_subln_g, j, lam_init, t_ctx, bl, ll,
                                    cache=(ck, cv))
            o = _linear(jnp.concatenate([o_ctx, o_lat], axis=0).astype(BF16), attn_w_o, j, d)
            new_k.append(qkv[:t_ctx, width:2 * width].reshape(bc, lc, ATTN_HEADS, 2, ATTN_HEAD_DIM))
            new_v.append(qkv[:t_ctx, 2 * width:].reshape(bc, lc, ATTN_HEADS, 2 * ATTN_HEAD_DIM))
        elif kind == 1:
            n_main = SSM_INNER + SSM_INNER + 2 * SSM_GROUPS * SSM_STATE
            zxbc = _linear(h, ssm_w_in, j, n_main)
            def regroup(a):
                a = a.reshape(a.shape[:-1] + (2, SSM_GROUPS, SSM_HPG))
                a = jnp.moveaxis(a, -3, -2).reshape(a.shape[:-3] + (SSM_GROUPS, 2 * SSM_HPG))
                pad = [(0, 0)] * (a.ndim - 1) + [(0, 128 - 2 * SSM_HPG)]
                return jnp.pad(a, pad).reshape(a.shape[:-2] + (SSM_GROUPS * 128,))
            w_dt = regroup(ssm_w_in[j][:, n_main:])[None]
            dtp = _linear(h, w_dt, 0, SSM_GROUPS * 128)
            dtb = regroup(ssm_dt_bias[j].reshape(1, -1))
            a_l = regroup(-jnp.exp(ssm_a_log[j]).reshape(1, -1))
            d_l = jnp.repeat(ssm_d[j], SSM_HEAD_DIM).reshape(1, SSM_INNER)
            cw, cb = ssm_conv_w[j], ssm_conv_b[j].reshape(1, -1)
            y_ctx, st = _ssd(zxbc, dtp, cw, cb, dtb, a_l, d_l, 0, bc, lc, emit_state=True)
            (y_lat,) = _ssd(zxbc, dtp, cw, cb, dtb, a_l, d_l, t_ctx, bl, ll, h0=state_ssm, layer=j)
            o = _ssm_out(jnp.concatenate([y_ctx, y_lat], axis=0), zxbc, ssm_norm_g, ssm_w_out, j)
            new_s.append(st)
        else:
            uv = _linear(h, gmlp_w_in, j, 2 * GMLP_HALF, bias=gmlp_b_in[j], act="gelu")
            o = _gmlp_mix_out(uv, gmlp_ln_g, gmlp_ln_b, gmlp_w_s, gmlp_b_s, gmlp_w_out, j)
        x, ht = _res_ln(x, o, mods[i], mods[i], ln_g[i, 0], ln_b[i, 0], t_ctx, ll, alpha=alpha,
                        gate=2, nshift=3, nscale=4, emit_ht=True)
        thr, e1, s2, e2 = _peer_scores(ht, wq_t, peer_sub_keys, i)
        u_bf, vt_bf = _peer_tables(peer_u, peer_v, i)
        f_t = _peer_mix(ht, thr, e1, s2, e2, u_bf, vt_bf)
        if i + 1 < depth:
            x, h = _res_ln(x, f_t, mods[i], mods[i + 1], ln_g[i, 1], ln_b[i, 1], t_ctx, ll,
                           alpha=alpha, gate=5, nshift=0, nscale=1, o_transposed=True, emit_h=True)
        else:
            (x,) = _res_ln(x, f_t, mods[i], mods[i], ln_g[i, 1], ln_b[i, 1], t_ctx, ll,
                           alpha=alpha, gate=5, o_transposed=True)
    return (x[:t_ctx].reshape(bc, lc, d), x[t_ctx:].reshape(bl, ll, d),
            jnp.stack(new_k, axis=1), jnp.stack(new_v, axis=1), jnp.stack(new_s, axis=1))
```

```python
import functools
import math

import jax
import jax.numpy as jnp
from jax import lax
from jax.experimental import pallas as pl
from jax.experimental.pallas import tpu as pltpu

F32 = jnp.float32
BF16 = jnp.bfloat16

D_MODEL = 1024
LN_EPS = 1e-5
GRID_W = 64
ATTN_HEADS = 8
ATTN_HEAD_DIM = 64
ROPE_AXIS_DIM = ATTN_HEAD_DIM // 2
ROPE_BASE = 10000.0
SSM_INNER = 2 * D_MODEL
SSM_HEAD_DIM = 64
SSM_GROUPS = 8
SSM_HPG = 4
SSM_STATE = 128
SSM_CONV = 5
SSD_CHUNK = 128
GMLP_HALF = 2 * D_MODEL
GMLP_GROUPS = 8
GMLP_GROUP_DIM = GMLP_HALF // GMLP_GROUPS
GMLP_CHUNK = 128
PEER_HEADS = 8
PEER_KEYS = 128
PEER_HALF = 128
PEER_TOPK = 16
N_MOD_GROUPS = 8
NEG_BIG = -1e30

VMEM_LIMIT = 56 * 1024 * 1024
ROW_TILE = 256
MM_ROW_TILE = 512
MM_COL_TILE = 1024
PEER_TOKEN_TILE = 256
PEER_SUPER_TILE = 1024
PEER_ROWS_PER_BLOCK = 8


def _params(n_axes):
    return pltpu.CompilerParams(dimension_semantics=("arbitrary",) * n_axes,
                                vmem_limit_bytes=VMEM_LIMIT)


def _bdot(a, b):
    return jnp.dot(a.astype(BF16), b.astype(BF16), preferred_element_type=F32)


def _bdot_nt(a, b):
    return lax.dot_general(a.astype(BF16), b.astype(BF16), (((1,), (1,)), ((), ())),
                           preferred_element_type=F32)


def _split3(a):
    hi = a.astype(BF16)
    r = a - hi.astype(F32)
    mid = r.astype(BF16)
    lo = (r - mid.astype(F32)).astype(BF16)
    return hi, mid, lo


def _dot_x3(a, b):
    ah, am, _ = _split3(a)
    bh, bm, _ = _split3(b)
    d = functools.partial(jnp.dot, preferred_element_type=F32)
    return d(ah, bh) + (d(am, bh) + d(ah, bm))


def _silu(x):
    return x * jax.nn.sigmoid(x)


def _gelu(x):
    return 0.5 * x * (1.0 + jnp.tanh(math.sqrt(2.0 / math.pi) * (x + 0.044715 * (x * x * x))))


def _group_index(i, tm, t_ctx, l_lat):
    start = i * tm
    return jnp.where(start < t_ctx, 0, 1 + (start - t_ctx) // l_lat)


def _mod_body(c_ref, w_ref, b_ref, o_ref):
    o_ref[0] = _dot_x3(_silu(c_ref[...]), w_ref[0]) + b_ref[0]


def _modulation(cond, w_mod, b_mod):
    depth, d, n = w_mod.shape
    tn = 1536
    return pl.pallas_call(
        _mod_body,
        grid=(depth, n // tn),
        in_specs=[pl.BlockSpec((N_MOD_GROUPS, d), lambda l, j: (0, 0)),
                  pl.BlockSpec((1, d, tn), lambda l, j: (l, 0, j)),
                  pl.BlockSpec((1, 1, tn), lambda l, j: (l, 0, j))],
        out_specs=pl.BlockSpec((1, N_MOD_GROUPS, tn), lambda l, j: (l, 0, j)),
        out_shape=jax.ShapeDtypeStruct((depth, N_MOD_GROUPS, n), F32),
        compiler_params=_params(2),
        name="modulation",
    )(cond, w_mod, b_mod.reshape(depth, 1, n))


def _modulate_body(x_ref, m_ref, h_ref, *, shift, scale):
    m = m_ref[0]
    h_ref[...] = (x_ref[...] * (1.0 + m[scale:scale + 1]) + m[shift:shift + 1]).astype(BF16)


def _modulate(x, mod, t_ctx, l_lat, shift, scale):
    t, d = x.shape
    tm = ROW_TILE
    return pl.pallas_call(
        functools.partial(_modulate_body, shift=shift, scale=scale),
        grid=(t // tm,),
        in_specs=[pl.BlockSpec((tm, d), lambda i: (i, 0)),
                  pl.BlockSpec((1, 6, d), lambda i: (_group_index(i, tm, t_ctx, l_lat), 0, 0))],
        out_specs=pl.BlockSpec((tm, d), lambda i: (i, 0)),
        out_shape=jax.ShapeDtypeStruct((t, d), BF16),
        compiler_params=_params(1),
        name="modulate",
    )(x, mod)


def _res_ln_body(x_ref, o_ref, m_ref, mn_ref, g_ref, b_ref, *outs,
                 alpha, gate, nshift, nscale, o_transposed, emit_h, emit_ht):
    o = o_ref[...]
    if o_transposed:
        o = o.T
    m = m_ref[0]
    y = alpha * x_ref[...] + m[gate:gate + 1] * o
    mu = jnp.mean(y, axis=-1, keepdims=True)
    yc = y - mu
    var = jnp.mean(yc * yc, axis=-1, keepdims=True)
    xn = yc * lax.rsqrt(var + LN_EPS) * g_ref[...] + b_ref[...]
    outs[0][...] = xn
    if emit_h or emit_ht:
        mn = mn_ref[0]
        h = xn * (1.0 + mn[nscale:nscale + 1]) + mn[nshift:nshift + 1]
        k = 1
        if emit_h:
            outs[k][...] = h.astype(BF16)
            k += 1
        if emit_ht:
            outs[k][...] = h.T.astype(BF16)


def _res_ln(x, o, mod, mod_next, ln_g, ln_b, t_ctx, l_lat, *, alpha, gate, nshift=0, nscale=1,
            o_transposed=False, emit_h=False, emit_ht=False):
    t, d = x.shape
    tm = ROW_TILE
    grp = lambda i: (_group_index(i, tm, t_ctx, l_lat), 0, 0)
    o_spec = (pl.BlockSpec((d, tm), lambda i: (0, i)) if o_transposed
              else pl.BlockSpec((tm, d), lambda i: (i, 0)))
    out_specs = [pl.BlockSpec((tm, d), lambda i: (i, 0))]
    out_shape = [jax.ShapeDtypeStruct((t, d), F32)]
    if emit_h:
        out_specs.append(pl.BlockSpec((tm, d), lambda i: (i, 0)))
        out_shape.append(jax.ShapeDtypeStruct((t, d), BF16))
    if emit_ht:
        out_specs.append(pl.BlockSpec((d, tm), lambda i: (0, i)))
        out_shape.append(jax.ShapeDtypeStruct((d, t), BF16))
    return pl.pallas_call(
        functools.partial(_res_ln_body, alpha=alpha, gate=gate, nshift=nshift, nscale=nscale,
                          o_transposed=o_transposed, emit_h=emit_h, emit_ht=emit_ht),
        grid=(t // tm,),
        in_specs=[pl.BlockSpec((tm, d), lambda i: (i, 0)), o_spec,
                  pl.BlockSpec((1, 6, d), grp), pl.BlockSpec((1, 6, d), grp),
                  pl.BlockSpec((1, d), lambda i: (0, 0)), pl.BlockSpec((1, d), lambda i: (0, 0))],
        out_specs=out_specs,
        out_shape=out_shape,
        compiler_params=_params(1),
        name="residual_layernorm",
    )(x, o, mod, mod_next, ln_g.reshape(1, d), ln_b.reshape(1, d))


def _linear_body(h_ref, w_ref, *refs, has_bias, act):
    if has_bias:
        b_ref, o_ref, wbf = refs
    else:
        o_ref, wbf = refs

    @pl.when(pl.program_id(1) == 0)
    def _():
        wbf[...] = w_ref[...].astype(BF16)

    y = jnp.dot(h_ref[...], wbf[...], preferred_element_type=F32)
    if has_bias:
        y = y + b_ref[...]
    if act == "gelu":
        y = _gelu(y)
    o_ref[...] = y


def _linear(h, w_stack, layer, n_out, bias=None, act=None):
    t, k = h.shape
    tm, tn = MM_ROW_TILE, min(MM_COL_TILE, n_out)
    in_specs = [pl.BlockSpec((tm, k), lambda n, i: (i, 0)),
                pl.BlockSpec((None, k, tn), lambda n, i: (layer, 0, n))]
    args = [h, w_stack]
    if bias is not None:
        in_specs.append(pl.BlockSpec((1, tn), lambda n, i: (0, n)))
        args.append(bias.reshape(1, n_out))
    return pl.pallas_call(
        functools.partial(_linear_body, has_bias=bias is not None, act=act),
        grid=(n_out // tn, t // tm),
        in_specs=in_specs,
        out_specs=pl.BlockSpec((tm, tn), lambda n, i: (i, n)),
        out_shape=jax.ShapeDtypeStruct((t, n_out), F32),
        scratch_shapes=[pltpu.VMEM((k, tn), BF16)],
        compiler_params=_params(2),
        name="linear",
    )(*args)


def _qkv_body(h_ref, w_ref, cos_ref, sa_ref, sb_ref, o_ref, wbf, *, n_ctx_tiles):
    n, i = pl.program_id(0), pl.program_id(1)

    @pl.when(i == 0)
    def _():
        wbf[...] = w_ref[...].astype(BF16)

    y = jnp.dot(h_ref[...], wbf[...], preferred_element_type=F32)
    rope = jnp.logical_and(n < 2, i >= n_ctx_tiles)
    half = ROPE_AXIS_DIM // 2

    @pl.when(rope)
    def _():
        up = pltpu.roll(y, y.shape[1] - half, 1)
        dn = pltpu.roll(y, half, 1)
        o_ref[...] = y * cos_ref[...] + up * sa_ref[...] + dn * sb_ref[...]

    @pl.when(jnp.logical_not(rope))
    def _():
        o_ref[...] = y


def _qkv_rope(h, w_stack, layer, tables, t_ctx, l_lat):
    t, k = h.shape
    n_out = w_stack.shape[2]
    tm, tn = MM_ROW_TILE, MM_COL_TILE
    n_ctx_tiles = t_ctx // tm
    pos = lambda n, i: (jnp.maximum(i - n_ctx_tiles, 0) % (l_lat // tm), 0)
    return pl.pallas_call(
        functools.partial(_qkv_body, n_ctx_tiles=n_ctx_tiles),
        grid=(n_out // tn, t // tm),
        in_specs=[pl.BlockSpec((tm, k), lambda n, i: (i, 0)),
                  pl.BlockSpec((None, k, tn), lambda n, i: (layer, 0, n)),
                  pl.BlockSpec((tm, tn), pos), pl.BlockSpec((tm, tn), pos), pl.BlockSpec((tm, tn), pos)],
        out_specs=pl.BlockSpec((tm, tn), lambda n, i: (i, n)),
        out_shape=jax.ShapeDtypeStruct((t, n_out), F32),
        scratch_shapes=[pltpu.VMEM((k, tn), BF16)],
        compiler_params=_params(2),
        name="qkv_rope",
    )(h, w_stack, *tables)


def _rope_tables(l_lat, width):
    rows = l_lat // GRID_W
    row_pos = jnp.repeat(jnp.arange(rows, dtype=F32), GRID_W)
    col_pos = jnp.tile(jnp.arange(GRID_W, dtype=F32), rows)
    inv_freq = ROPE_BASE ** (-jnp.arange(0, ROPE_AXIS_DIM, 2, dtype=F32) / ROPE_AXIS_DIM)

    def tab(p):
        ang = p[:, None] * inv_freq[None, :]
        ang = jnp.concatenate([ang, ang], -1)
        return jnp.cos(ang), jnp.sin(ang)

    cr, sr = tab(row_pos)
    cc, sc = tab(col_pos)
    cos = jnp.concatenate([cr, cc], -1)
    sin = jnp.concatenate([sr, sc], -1)
    first_half = (jnp.arange(ATTN_HEAD_DIM) % ROPE_AXIS_DIM) < (ROPE_AXIS_DIM // 2)
    sin_a = jnp.where(first_half, -sin, 0.0)
    sin_b = jnp.where(first_half, 0.0, sin)
    rep = width // ATTN_HEAD_DIM
    return tuple(jnp.tile(a, (1, rep)) for a in (cos, sin_a, sin_b))


def _attn_body(lam_ref, g_ref, q_ref, k_ref, v_ref, *refs, lam_init, past, tq):
    if past:
        kc_ref, vc_ref, o_ref, k_s, v_s = refs
        k_s[:past] = kc_ref[...].astype(BF16)
        v_s[:past] = vc_ref[...].astype(BF16)
    else:
        o_ref, k_s, v_s = refs
    k_s[past:] = k_ref[...].astype(BF16)
    v_s[past:] = v_ref[...].astype(BF16)
    lp = lam_ref[...]
    lam = (jnp.exp(jnp.sum(lp[0:1] * lp[1:2], axis=-1, keepdims=True))
           - jnp.exp(jnp.sum(lp[2:3] * lp[3:4], axis=-1, keepdims=True)) + lam_init)
    dh = ATTN_HEAD_DIM
    scale = dh ** -0.5

    def q_block(i, carry):
        rows = pl.ds(pl.multiple_of(i * tq, tq), tq)
        q = (q_ref[rows, :] * scale).astype(BF16)

        def softmax_map(m):
            s = _bdot_nt(q[:, m * dh:(m + 1) * dh], k_s[:, m * dh:(m + 1) * dh])
            e = jnp.exp(s - jnp.max(s, axis=-1, keepdims=True))
            return e / jnp.sum(e, axis=-1, keepdims=True)

        a = softmax_map(0) - lam * softmax_map(1)
        o = _bdot(a, v_s[...])
        o = o * lax.rsqrt(jnp.mean(o * o, axis=-1, keepdims=True) + LN_EPS) * g_ref[...]
        o_ref[rows, :] = (o * (1.0 - lam_init)).astype(o_ref.dtype)
        return carry

    lax.fori_loop(0, q_ref.shape[0] // tq, q_block, 0)


def _diff_attention(qkv, lam_p, subln_g, layer, lam_init, row0, n_batch, seq, cache=None):
    width = 2 * ATTN_HEAD_DIM
    base = row0 // seq
    in_specs = [pl.BlockSpec((None, 4, ATTN_HEAD_DIM), lambda b, h: (layer, 0, 0)),
                pl.BlockSpec((None, 1, width), lambda b, h: (layer, 0, 0)),
                pl.BlockSpec((seq, width), lambda b, h: (base + b, h)),
                pl.BlockSpec((seq, width), lambda b, h: (base + b, ATTN_HEADS + h)),
                pl.BlockSpec((seq, width), lambda b, h: (base + b, 2 * ATTN_HEADS + h))]
    args = [lam_p, subln_g.reshape(subln_g.shape[0], 1, width), qkv, qkv, qkv]
    past = 0
    if cache is not None:
        ck, cv = cache
        past = ck.shape[2]
        spec = pl.BlockSpec((None, None, past, width), lambda b, h: (b, layer, 0, h))
        in_specs += [spec, spec]
        args += [ck, cv]
    return pl.pallas_call(
        functools.partial(_attn_body, lam_init=lam_init, past=past, tq=128 if past else min(seq, 256)),
        grid=(n_batch, ATTN_HEADS),
        in_specs=in_specs,
        out_specs=pl.BlockSpec((seq, width), lambda b, h: (b, h)),
        out_shape=jax.ShapeDtypeStruct((n_batch * seq, ATTN_HEADS * width), BF16),
        scratch_shapes=[pltpu.VMEM((past + seq, width), BF16)] * 2,
        compiler_params=_params(2),
        name="diff_attention",
    )(*args)


def _conv_silu(x, w, b):
    n = x.shape[0]
    row = lax.broadcasted_iota(jnp.int32, x.shape, 0)
    acc = x * w[SSM_CONV // 2:SSM_CONV // 2 + 1] + b
    for k in range(SSM_CONV):
        sh = k - SSM_CONV // 2
        if sh == 0:
            continue
        shifted = pltpu.roll(x, (-sh) % n, 0)
        valid = jnp.logical_and(row + sh >= 0, row + sh < n)
        acc = acc + jnp.where(valid, shifted, 0.0) * w[k:k + 1]
    return _silu(acc)


def _softplus(x):
    return jnp.maximum(x, 0.0) + jnp.log1p(jnp.exp(-jnp.abs(x)))


def _ssd_body(*refs, has_h0, emit_state, n_chunks):
    (xs_ref, bm_ref, cm_ref, dt_ref, wx_ref, bx_ref, wb_ref, bb_ref, wc_ref, bc_ref,
     dtb_ref, a_ref, d_ref) = refs[:13]
    refs = refs[13:]
    if has_h0:
        h0_ref, refs = refs[0], refs[1:]
    y_ref, refs = refs[0], refs[1:]
    if emit_state:
        st_ref, refs = refs[0], refs[1:]
    xs_s, b_s, c_s, dt_s, da_s, xt_s, dtt_s, dat_s, h_s = refs
    q = SSD_CHUNK
    p = SSM_HEAD_DIM
    dot = functools.partial(jnp.dot, preferred_element_type=F32)

    xs = _conv_silu(xs_ref[...], wx_ref[...], bx_ref[...])
    xs_s[...] = xs
    y_ref[...] = d_ref[...] * xs
    b_s[...] = _conv_silu(bm_ref[...], wb_ref[...], bb_ref[...])
    c_s[...] = _conv_silu(cm_ref[...], wc_ref[...], bc_ref[...])
    dt = _softplus(dt_ref[...] + dtb_ref[...])
    dt_s[...] = dt
    da_s[...] = dt * a_ref[...]

    def transposes(c, carry):
        rows = pl.ds(pl.multiple_of(c * q, q), q)
        xt_s[c] = xs_s[rows, :].T
        dtt_s[c] = dt_s[rows, :].T
        dat_s[c] = da_s[rows, :].T
        return carry

    lax.fori_loop(0, n_chunks, transposes, 0)

    row = lax.broadcasted_iota(jnp.int32, (q, q), 0)
    col = lax.broadcasted_iota(jnp.int32, (q, q), 1)
    lower, upper = row >= col, row <= col
    ones = [jnp.where(m, 1.0, 0.0).astype(BF16) for m in (lower, upper)]
    for direction in range(2):
        for r in range(SSM_HPG):
            if has_h0:
                h_s[direction, r] = h0_ref[direction, r]
            else:
                h_s[direction, r] = jnp.zeros((p, SSM_STATE), F32)

    def scan_chunk(c, direction):
        mask = lower if direction == 0 else upper
        end = q - 1 if direction == 0 else 0
        rows = pl.ds(pl.multiple_of(c * q, q), q)
        xc = xs_s[rows, :]
        bc = b_s[rows, :].astype(BF16)
        cc = c_s[rows, :].astype(BF16)
        dtc = dt_s[rows, :]
        dt_t = dtt_s[c]
        x_t = xt_s[c]
        d_hi, d_mid, d_lo = _split3(da_s[rows, :])
        tri = ones[direction]
        acum = dot(tri, d_hi) + (dot(tri, d_mid) + dot(tri, d_lo))
        t_hi, t_mid, t_lo = _split3(dat_s[c])
        tri_t = ones[1 - direction]
        acum_t = dot(t_hi, tri_t) + (dot(t_mid, tri_t) + dot(t_lo, tri_t))
        cb = _bdot_nt(cc, bc)
        a_end = acum[end:end + 1, :]
        ys = []
        for r in range(SSM_HPG):
            hl = direction * SSM_HPG + r
            a_col = acum[:, hl:hl + 1]
            a_row = acum_t[hl:hl + 1, :]
            decay = jnp.exp(jnp.where(mask, a_col - a_row, NEG_BIG))
            xdt = xc[:, r * p:(r + 1) * p] * dtc[:, hl:hl + 1]
            h = h_s[direction, r]
            y = _bdot(cb * decay, xdt) + jnp.exp(a_col) * _bdot_nt(cc, h)
            w = dt_t[hl:hl + 1, :] * jnp.exp(a_end[:, hl:hl + 1] - a_row)
            h_s[direction, r] = (jnp.exp(a_end[:, hl:hl + 1]) * h
                                 + _bdot(x_t[r * p:(r + 1) * p, :] * w, bc))
            ys.append(y)
        return rows, jnp.concatenate(ys, axis=1)

    def step(ci, carry):
        rows_f, y_f = scan_chunk(ci, 0)
        rows_b, y_b = scan_chunk(n_chunks - 1 - ci, 1)
        y_ref[rows_f, :] = y_ref[rows_f, :] + y_f
        y_ref[rows_b, :] = y_ref[rows_b, :] + y_b
        return carry

    lax.fori_loop(0, n_chunks, step, 0)
    if emit_state:
        for direction in range(2):
            for r in range(SSM_HPG):
                st_ref[direction, r] = h_s[direction, r]


def _ssd(zxbc, dtp, conv_w, conv_b, dt_bias_l, a_l, d_l, row0, n_batch, seq, h0=None, layer=0,
         emit_state=False):
    inner, gs, hp = SSM_INNER, SSM_GROUPS * SSM_STATE, SSM_HPG * SSM_HEAD_DIM
    base = row0 // seq
    g_of = lambda off, w: (lambda b, g: (0, off // w + g))
    row_blk = lambda off, w: (lambda b, g: (base + b, off // w + g))
    in_specs = [pl.BlockSpec((seq, hp), row_blk(inner, hp)),
                pl.BlockSpec((seq, SSM_STATE), row_blk(2 * inner, SSM_STATE)),
                pl.BlockSpec((seq, SSM_STATE), row_blk(2 * inner + gs, SSM_STATE)),
                pl.BlockSpec((seq, 128), row_blk(0, 128)),
                pl.BlockSpec((SSM_CONV, hp), g_of(0, hp)), pl.BlockSpec((1, hp), g_of(0, hp)),
                pl.BlockSpec((SSM_CONV, SSM_STATE), g_of(inner, SSM_STATE)),
                pl.BlockSpec((1, SSM_STATE), g_of(inner, SSM_STATE)),
                pl.BlockSpec((SSM_CONV, SSM_STATE), g_of(inner + gs, SSM_STATE)),
                pl.BlockSpec((1, SSM_STATE), g_of(inner + gs, SSM_STATE)),
                pl.BlockSpec((1, 128), g_of(0, 128)), pl.BlockSpec((1, 128), g_of(0, 128)),
                pl.BlockSpec((1, hp), g_of(0, hp))]
    args = [zxbc, zxbc, zxbc, dtp, conv_w, conv_b, conv_w, conv_b, conv_w, conv_b, dt_bias_l, a_l, d_l]
    if h0 is not None:
        in_specs.append(pl.BlockSpec((None, None, 2, SSM_HPG, SSM_HEAD_DIM, SSM_STATE),
                                     lambda b, g: (b, layer, 0, g, 0, 0)))
        args.append(h0)
    out_specs = [pl.BlockSpec((seq, hp), lambda b, g: (b, g))]
    out_shape = [jax.ShapeDtypeStruct((n_batch * seq, inner), F32)]
    if emit_state:
        out_specs.append(pl.BlockSpec((None, 2, SSM_HPG, SSM_HEAD_DIM, SSM_STATE),
                                      lambda b, g: (b, 0, g, 0, 0)))
        out_shape.append(jax.ShapeDtypeStruct(
            (n_batch, 2, SSM_GROUPS * SSM_HPG, SSM_HEAD_DIM, SSM_STATE), F32))
    return pl.pallas_call(
        functools.partial(_ssd_body, has_h0=h0 is not None, emit_state=emit_state,
                          n_chunks=seq // SSD_CHUNK),
        grid=(n_batch, SSM_GROUPS),
        in_specs=in_specs,
        out_specs=out_specs,
        out_shape=out_shape,
        scratch_shapes=[pltpu.VMEM((seq, hp), F32), pltpu.VMEM((seq, SSM_STATE), F32),
                        pltpu.VMEM((seq, SSM_STATE), F32), pltpu.VMEM((seq, 128), F32),
                        pltpu.VMEM((seq, 128), F32),
                        pltpu.VMEM((seq // SSD_CHUNK, hp, SSD_CHUNK), F32),
                        pltpu.VMEM((seq // SSD_CHUNK, 128, SSD_CHUNK), F32),
                        pltpu.VMEM((seq // SSD_CHUNK, 128, SSD_CHUNK), F32),
                        pltpu.VMEM((2, SSM_HPG, SSM_HEAD_DIM, SSM_STATE), F32)],
        compiler_params=_params(2),
        name="ssd_scan",
    )(*args)


def _ssm_out_body(y_ref, z_ref, g_ref, w_ref, o_ref, wbf):
    @pl.when(pl.program_id(0) == 0)
    def _():
        wbf[...] = w_ref[...].astype(BF16)

    y = y_ref[...] * _silu(z_ref[...])
    y = y * lax.rsqrt(jnp.mean(y * y, axis=-1, keepdims=True) + LN_EPS) * g_ref[...]
    o_ref[...] = jnp.dot(y.astype(BF16), wbf[...], preferred_element_type=F32)


def _ssm_out(y, zxbc, norm_g, w_out, layer):
    t, inner = y.shape
    d = w_out.shape[2]
    tm = ROW_TILE
    return pl.pallas_call(
        _ssm_out_body,
        grid=(t // tm,),
        in_specs=[pl.BlockSpec((tm, inner), lambda i: (i, 0)),
                  pl.BlockSpec((tm, inner), lambda i: (i, 0)),
                  pl.BlockSpec((None, 1, inner), lambda i: (layer, 0, 0)),
                  pl.BlockSpec((None, inner, d), lambda i: (layer, 0, 0))],
        out_specs=pl.BlockSpec((tm, d), lambda i: (i, 0)),
        out_shape=jax.ShapeDtypeStruct((t, d), F32),
        scratch_shapes=[pltpu.VMEM((inner, d), BF16)],
        compiler_params=_params(1),
        name="ssm_gate_norm_out",
    )(y, zxbc, norm_g.reshape(norm_g.shape[0], 1, inner), w_out)


def _gmlp_body(u_ref, v_ref, g_ref, b_ref, ws_ref, bs_ref, w_ref, o_ref, wbf, t_s, *, n_chunks):
    @pl.when(pl.program_id(0) == 0)
    def _():
        wbf[...] = w_ref[...].astype(BF16)

    v = v_ref[...]
    mu = jnp.mean(v, axis=-1, keepdims=True)
    vc = v - mu
    var = jnp.mean(vc * vc, axis=-1, keepdims=True)
    vn = (vc * lax.rsqrt(var + LN_EPS) * g_ref[...] + b_ref[...]).astype(BF16)
    q, gd = GMLP_CHUNK, GMLP_GROUP_DIM
    for g in range(GMLP_GROUPS):
        ws = ws_ref[g].astype(BF16)
        bias = bs_ref[:, g:g + 1]
        for c in range(n_chunks):
            sv = jnp.dot(ws, vn[c * q:(c + 1) * q, g * gd:(g + 1) * gd],
                         preferred_element_type=F32) + bias
            t_s[c * q:(c + 1) * q, g * gd:(g + 1) * gd] = (
                u_ref[c * q:(c + 1) * q, g * gd:(g + 1) * gd] * sv).astype(BF16)
    o_ref[...] = jnp.dot(t_s[...], wbf[...], preferred_element_type=F32)


def _gmlp_mix_out(uv, ln_g, ln_b, w_s, b_s, w_out, layer):
    t = uv.shape[0]
    half, d = GMLP_HALF, w_out.shape[2]
    tm = 512
    return pl.pallas_call(
        functools.partial(_gmlp_body, n_chunks=tm // GMLP_CHUNK),
        grid=(t // tm,),
        in_specs=[pl.BlockSpec((tm, half), lambda i: (i, 0)),
                  pl.BlockSpec((tm, half), lambda i: (i, 1)),
                  pl.BlockSpec((None, 1, half), lambda i: (layer, 0, 0)),
                  pl.BlockSpec((None, 1, half), lambda i: (layer, 0, 0)),
                  pl.BlockSpec((None, GMLP_GROUPS, GMLP_CHUNK, GMLP_CHUNK), lambda i: (layer, 0, 0, 0)),
                  pl.BlockSpec((None, GMLP_CHUNK, GMLP_GROUPS), lambda i: (layer, 0, 0)),
                  pl.BlockSpec((None, half, d), lambda i: (layer, 0, 0))],
        out_specs=pl.BlockSpec((tm, d), lambda i: (i, 0)),
        out_shape=jax.ShapeDtypeStruct((t, d), F32),
        scratch_shapes=[pltpu.VMEM((half, d), BF16), pltpu.VMEM((tm, half), BF16)],
        compiler_params=_params(1),
        name="gmlp_mix_out",
    )(uv, uv, ln_g.reshape(-1, 1, half), ln_b.reshape(-1, 1, half), w_s,
      jnp.swapaxes(b_s, 1, 2), w_out)


def _sorting_network(lo, hi):
    def merge(lo, hi, r):
        step = r * 2
        if step < hi - lo:
            yield from merge(lo, hi, step)
            yield from merge(lo + r, hi, step)
            yield from [(i, i + r) for i in range(lo + r, hi - r, step)]
        else:
            yield (lo, lo + r)

    if hi - lo >= 1:
        mid = lo + (hi - lo) // 2
        yield from _sorting_network(lo, mid)
        yield from _sorting_network(mid + 1, hi)
        yield from merge(lo, hi, 1)


def _top_rows(s, n, n_out):
    sub = 8
    n_slabs = s.shape[0] // sub
    v = [s[i * sub:(i + 1) * sub] for i in range(n_slabs)]
    for i, j in _sorting_network(0, n_slabs - 1):
        v[i], v[j] = jnp.maximum(v[i], v[j]), jnp.minimum(v[i], v[j])
    slot = lax.broadcasted_iota(jnp.int32, (n_out, s.shape[1]), 0)
    out = jnp.full((n_out, s.shape[1]), NEG_BIG, F32)
    for k in range(n):
        m = jnp.max(v[0], axis=0, keepdims=True)
        out = jnp.where(slot == k, m, out)
        hit = v[0] >= m
        depth = min(n_slabs, n - 1 - k)
        for i in range(depth):
            v[i] = jnp.where(hit, v[i + 1] if i + 1 < n_slabs else NEG_BIG, v[i])
    return out


def _peer_score_body(ht_ref, wq_ref, keys_ref, thr_ref, e1_ref, s2_ref, e2_ref, wbf):
    @pl.when(pl.program_id(0) == 0)
    def _():
        wbf[...] = wq_ref[...].astype(BF16)

    ht = ht_ref[...]
    kk, kh, sub = PEER_TOPK, PEER_HALF, PEER_ROWS_PER_BLOCK
    assert kk == 16 and sub == 8
    n_top = 3 * sub
    rank8 = lax.broadcasted_iota(jnp.int32, (sub, ht.shape[1]), 0)
    for h in range(PEER_HEADS):
        qh = jnp.dot(wbf[2 * kh * h:2 * kh * (h + 1), :], ht, preferred_element_type=F32)
        s1 = _dot_x3(keys_ref[h, 0], qh[:kh])
        s2 = _dot_x3(keys_ref[h, 1], qh[kh:])
        t1 = _top_rows(s1, kk + 1, n_top)
        t2 = _top_rows(s2, kk + 1, n_top)
        low, one = t1[0:sub], t2[0:sub]
        cand = jnp.concatenate([
            t1[0:1] + one,
            t1[0:1] + t2[sub:kk],
            t1[1:2] + one,
            jnp.where(rank8 < 5, t1[2:3] + one, NEG_BIG),
            jnp.where(rank8 < 4, t1[3:4] + one, low + t2[1:2]),
            jnp.where(rank8 >= 4, low + t2[0:1], NEG_BIG),
            t1[sub:kk] + t2[0:1],
            jnp.where(rank8 == 0, t1[kk:kk + 1] + t2[0:1],
                      jnp.where(rank8 == 1, t1[0:1] + t2[kk:kk + 1],
                                jnp.where(rank8 == 2, t1[4:5] + t2[2:3], NEG_BIG)))], axis=0)
        best = _top_rows(cand, kk + 1, n_top)
        z = jnp.sum(jnp.exp(best[:kk] - best[0:1]), axis=0, keepdims=True)
        thr = 0.5 * (best[kk - 1:kk] + best[kk:kk + 1]) - s1
        e1 = jnp.exp(s1 - t1[0:1])
        for rb in range(PEER_KEYS // sub):
            thr_ref[h, rb] = thr[rb * sub:(rb + 1) * sub]
            e1_ref[h, rb] = e1[rb * sub:(rb + 1) * sub]
        s2_ref[h] = s2
        e2_ref[h] = jnp.exp(s2 - t2[0:1]) * (0.5 / z)


def _peer_scores(ht, wq_t, sub_keys, layer):
    d, t = ht.shape
    tt = PEER_TOKEN_TILE
    nq = wq_t.shape[1]
    sub = PEER_ROWS_PER_BLOCK
    row_spec = pl.BlockSpec((PEER_HEADS, PEER_KEYS // sub, sub, tt), lambda i: (0, 0, 0, i))
    row_shape = jax.ShapeDtypeStruct((PEER_HEADS, PEER_KEYS // sub, sub, t), F32)
    tok_spec = pl.BlockSpec((PEER_HEADS, PEER_KEYS, tt), lambda i: (0, 0, i))
    tok_shape = jax.ShapeDtypeStruct((PEER_HEADS, PEER_KEYS, t), F32)
    return pl.pallas_call(
        _peer_score_body,
        grid=(t // tt,),
        in_specs=[pl.BlockSpec((d, tt), lambda i: (0, i)),
                  pl.BlockSpec((None, nq, d), lambda i: (layer, 0, 0)),
                  pl.BlockSpec((None, PEER_HEADS, 2, PEER_KEYS, PEER_HALF), lambda i: (layer, 0, 0, 0, 0))],
        out_specs=[row_spec, row_spec, tok_spec, tok_spec],
        out_shape=[row_shape, row_shape, tok_shape, tok_shape],
        scratch_shapes=[pltpu.VMEM((nq, d), BF16)],
        compiler_params=_params(1),
        name="peer_scores",
    )(ht, wq_t, sub_keys)


def _peer_tables_body(u_ref, v_ref, ub_ref, vt_ref):
    ub_ref[...] = u_ref[...].astype(BF16)
    vt_ref[...] = v_ref[...].T.astype(BF16)


def _peer_tables(u_tab, v_tab, layer):
    n, d = u_tab.shape[1:]
    eb = MM_COL_TILE
    return pl.pallas_call(
        _peer_tables_body,
        grid=(n // eb,),
        in_specs=[pl.BlockSpec((None, eb, d), lambda e: (layer, e, 0)),
                  pl.BlockSpec((None, eb, d), lambda e: (layer, e, 0))],
        out_specs=[pl.BlockSpec((eb, d), lambda e: (e, 0)), pl.BlockSpec((d, eb), lambda e: (0, e))],
        out_shape=[jax.ShapeDtypeStruct((n, d), BF16), jax.ShapeDtypeStruct((d, n), BF16)],
        compiler_params=_params(1),
        name="peer_tables",
    )(u_tab, v_tab)


def _peer_mix_body(ht_ref, thr_ref, e1_ref, s2_ref, e2_ref, u_ref, vt_ref, o_ref, act_s, coef_s):
    @pl.when(pl.program_id(1) == 0)
    def _():
        o_ref[...] = jnp.zeros(o_ref.shape, F32)

    lanes, kq = 128, 32
    n_q = PEER_KEYS // kq
    act_s[...] = jnp.dot(u_ref[...], ht_ref[...], preferred_element_type=F32)
    c0 = math.sqrt(2.0 / math.pi)

    def gate_chunk(ci, carry):
        part, q = ci // n_q, ci % n_q
        c = pl.ds(pl.multiple_of(part * lanes, lanes), lanes)
        keys = pl.ds(pl.multiple_of(q * kq, kq), kq)
        gates = [jnp.zeros((kq, lanes), F32)] * PEER_ROWS_PER_BLOCK
        for h in range(PEER_HEADS):
            s2 = s2_ref[h, keys, c]
            e2 = e2_ref[h, keys, c]
            for r in range(PEER_ROWS_PER_BLOCK):
                chosen = s2 >= thr_ref[h, 0, r:r + 1, c]
                gates[r] = gates[r] + jnp.where(chosen, e2, 0.0) * e1_ref[h, 0, r:r + 1, c]
        for r in range(PEER_ROWS_PER_BLOCK):
            rows = pl.ds(pl.multiple_of(r * PEER_KEYS + q * kq, kq), kq)
            a = act_s[rows, c]
            twice_gelu = a * (1.0 + jnp.tanh(a * (c0 + (c0 * 0.044715) * (a * a))))
            coef_s[rows, c] = (twice_gelu * gates[r]).astype(BF16)
        return carry

    lax.fori_loop(0, (o_ref.shape[1] // lanes) * n_q, gate_chunk, 0)
    o_ref[...] = o_ref[...] + jnp.dot(vt_ref[...], coef_s[...], preferred_element_type=F32)


def _peer_mix(ht, thr, e1, s2, e2, u_bf, vt_bf):
    d, t = ht.shape
    ts, rb = PEER_SUPER_TILE, PEER_ROWS_PER_BLOCK
    eb = rb * PEER_KEYS
    n_blocks = u_bf.shape[0] // eb
    row_spec = pl.BlockSpec((PEER_HEADS, 1, rb, ts), lambda s, e: (0, e, 0, s))
    tok_spec = pl.BlockSpec((PEER_HEADS, PEER_KEYS, ts), lambda s, e: (0, 0, s))
    return pl.pallas_call(
        _peer_mix_body,
        grid=(t // ts, n_blocks),
        in_specs=[pl.BlockSpec((d, ts), lambda s, e: (0, s)), row_spec, row_spec, tok_spec, tok_spec,
                  pl.BlockSpec((eb, d), lambda s, e: (e, 0)), pl.BlockSpec((d, eb), lambda s, e: (0, e))],
        out_specs=pl.BlockSpec((d, ts), lambda s, e: (0, s)),
        out_shape=jax.ShapeDtypeStruct((d, t), F32),
        scratch_shapes=[pltpu.VMEM((eb, ts), F32), pltpu.VMEM((eb, ts), BF16)],
        compiler_params=_params(2),
        name="peer_mix",
    )(ht, thr, e1, s2, e2, u_bf, vt_bf)


def kernel(x_prompt, x_sample, cache_k, cache_v, state_ssm, c, c_ctx, w_mod, b_mod, ln_g, ln_b, attn_w_qkv, attn_w_o, attn_lambda, attn_subln_g, ssm_w_in, ssm_conv_w, ssm_conv_b, ssm_dt_bias, ssm_a_log, ssm_d, ssm_norm_g, ssm_w_out, gmlp_w_in, gmlp_b_in, gmlp_ln_g, gmlp_ln_b, gmlp_w_s, gmlp_b_s, gmlp_w_out, peer_w_q, peer_sub_keys, peer_u, peer_v):
    bc, lc, d = x_prompt.shape
    bl, ll, _ = x_sample.shape
    depth = w_mod.shape[0]
    t_ctx, t_lat = bc * lc, bl * ll
    alpha = (2 * depth) ** 0.25
    assert 1 + bl <= N_MOD_GROUPS and t_ctx % ll == 0

    x = jnp.concatenate([x_prompt.reshape(t_ctx, d), x_sample.reshape(t_lat, d)], axis=0)
    cond = jnp.zeros((N_MOD_GROUPS, d), F32).at[0].set(c_ctx).at[1:1 + bl].set(c)
    mods = _modulation(cond, w_mod, b_mod).reshape(depth, N_MOD_GROUPS, 6, d)
    rope = _rope_tables(ll, attn_w_qkv.shape[2] // 3)
    past = cache_k.shape[2]
    ck = cache_k.reshape(bl, cache_k.shape[1], past, -1)
    cv = cache_v.reshape(bl, cache_v.shape[1], past, -1)
    wq_t = jnp.swapaxes(peer_w_q, 1, 2)

    new_k, new_v, new_s = [], [], []
    h = _modulate(x, mods[0], t_ctx, ll, shift=0, scale=1)
    for i in range(depth):
        kind, j = i % 3, i // 3
        if kind == 0:
            lam_init = 0.8 - 0.6 * math.exp(-0.3 * i)
            qkv = _qkv_rope(h, attn_w_qkv, j, rope, t_ctx, ll)
            width = qkv.shape[1] // 3
            o_ctx = _diff_attention(qkv, attn_lambda, attn_subln_g, j, lam_init, 0, bc, lc)
            o_lat = _diff_attention(qkv, attn_lambda, attn_subln_g, j, lam_init, t_ctx, bl, ll,
                                    cache=(ck, cv))
            o = _linear(jnp.concatenate([o_ctx, o_lat], axis=0).astype(BF16), attn_w_o, j, d)
            new_k.append(qkv[:t_ctx, width:2 * width].reshape(bc, lc, ATTN_HEADS, 2, ATTN_HEAD_DIM))
            new_v.append(qkv[:t_ctx, 2 * width:].reshape(bc, lc, ATTN_HEADS, 2 * ATTN_HEAD_DIM))
        elif kind == 1:
            n_main = SSM_INNER + SSM_INNER + 2 * SSM_GROUPS * SSM_STATE
            zxbc = _linear(h, ssm_w_in, j, n_main)
            def regroup(a):
                a = a.reshape(a.shape[:-1] + (2, SSM_GROUPS, SSM_HPG))
                a = jnp.moveaxis(a, -3, -2).reshape(a.shape[:-3] + (SSM_GROUPS, 2 * SSM_HPG))
                pad = [(0, 0)] * (a.ndim - 1) + [(0, 128 - 2 * SSM_HPG)]
                return jnp.pad(a, pad).reshape(a.shape[:-2] + (SSM_GROUPS * 128,))
            w_dt = regroup(ssm_w_in[j][:, n_main:])[None]
            dtp = _linear(h, w_dt, 0, SSM_GROUPS * 128)
            dtb = regroup(ssm_dt_bias[j].reshape(1, -1))
            a_l = regroup(-jnp.exp(ssm_a_log[j]).reshape(1, -1))
            d_l = jnp.repeat(ssm_d[j], SSM_HEAD_DIM).reshape(1, SSM_INNER)
            cw, cb = ssm_conv_w[j], ssm_conv_b[j].reshape(1, -1)
            y_ctx, st = _ssd(zxbc, dtp, cw, cb, dtb, a_l, d_l, 0, bc, lc, emit_state=True)
            (y_lat,) = _ssd(zxbc, dtp, cw, cb, dtb, a_l, d_l, t_ctx, bl, ll, h0=state_ssm, layer=j)
            o = _ssm_out(jnp.concatenate([y_ctx, y_lat], axis=0), zxbc, ssm_norm_g, ssm_w_out, j)
            new_s.append(st)
        else:
            uv = _linear(h, gmlp_w_in, j, 2 * GMLP_HALF, bias=gmlp_b_in[j], act="gelu")
            o = _gmlp_mix_out(uv, gmlp_ln_g, gmlp_ln_b, gmlp_w_s, gmlp_b_s, gmlp_w_out, j)
        x, ht = _res_ln(x, o, mods[i], mods[i], ln_g[i, 0], ln_b[i, 0], t_ctx, ll, alpha=alpha,
                        gate=2, nshift=3, nscale=4, emit_ht=True)
        thr, e1, s2, e2 = _peer_scores(ht, wq_t, peer_sub_keys, i)
        u_bf, vt_bf = _peer_tables(peer_u, peer_v, i)
        f_t = _peer_mix(ht, thr, e1, s2, e2, u_bf, vt_bf)
        if i + 1 < depth:
            x, h = _res_ln(x, f_t, mods[i], mods[i + 1], ln_g[i, 1], ln_b[i, 1], t_ctx, ll,
                           alpha=alpha, gate=5, nshift=0, nscale=1, o_transposed=True, emit_h=True)
        else:
            (x,) = _res_ln(x, f_t, mods[i], mods[i], ln_g[i, 1], ln_b[i, 1], t_ctx, ll,
                           alpha=alpha, gate=5, o_transposed=True)
    return (x[:t_ctx].reshape(bc, lc, d), x[t_ctx:].reshape(bl, ll, d),
            jnp.stack(new_k, axis=1), jnp.stack(new_v, axis=1), jnp.stack(new_s, axis=1))
```

```python
import functools
import math

import jax
import jax.numpy as jnp
from jax import lax
from jax.experimental import pallas as pl
from jax.experimental.pallas import tpu as pltpu

F32 = jnp.float32
BF16 = jnp.bfloat16

D_MODEL = 1024
LN_EPS = 1e-5
GRID_W = 64
ATTN_HEADS = 8
ATTN_HEAD_DIM = 64
ROPE_AXIS_DIM = ATTN_HEAD_DIM // 2
ROPE_BASE = 10000.0
SSM_INNER = 2 * D_MODEL
SSM_HEAD_DIM = 64
SSM_GROUPS = 8
SSM_HPG = 4
SSM_STATE = 128
SSM_CONV = 5
SSD_CHUNK = 128
GMLP_HALF = 2 * D_MODEL
GMLP_GROUPS = 8
GMLP_GROUP_DIM = GMLP_HALF // GMLP_GROUPS
GMLP_CHUNK = 128
PEER_HEADS = 8
PEER_KEYS = 128
PEER_HALF = 128
PEER_TOPK = 16
N_MOD_GROUPS = 8
NEG_BIG = -1e30

VMEM_LIMIT = 56 * 1024 * 1024
ROW_TILE = 256
MM_ROW_TILE = 512
MM_COL_TILE = 1024
PEER_TOKEN_TILE = 256
PEER_SUPER_TILE = 1024
PEER_ROWS_PER_BLOCK = 8


def _params(n_axes):
    return pltpu.CompilerParams(dimension_semantics=("arbitrary",) * n_axes,
                                vmem_limit_bytes=VMEM_LIMIT)


def _bdot(a, b):
    return jnp.dot(a.astype(BF16), b.astype(BF16), preferred_element_type=F32)


def _bdot_nt(a, b):
    return lax.dot_general(a.astype(BF16), b.astype(BF16), (((1,), (1,)), ((), ())),
                           preferred_element_type=F32)


def _split3(a):
    hi = a.astype(BF16)
    r = a - hi.astype(F32)
    mid = r.astype(BF16)
    lo = (r - mid.astype(F32)).astype(BF16)
    return hi, mid, lo


def _dot_x3(a, b):
    ah, am, _ = _split3(a)
    bh, bm, _ = _split3(b)
    d = functools.partial(jnp.dot, preferred_element_type=F32)
    return d(ah, bh) + (d(am, bh) + d(ah, bm))


def _silu(x):
    return x * jax.nn.sigmoid(x)


def _gelu(x):
    return 0.5 * x * (1.0 + jnp.tanh(math.sqrt(2.0 / math.pi) * (x + 0.044715 * (x * x * x))))


def _group_index(i, tm, t_ctx, l_lat):
    start = i * tm
    return jnp.where(start < t_ctx, 0, 1 + (start - t_ctx) // l_lat)


def _mod_body(c_ref, w_ref, b_ref, o_ref):
    o_ref[0] = _dot_x3(_silu(c_ref[...]), w_ref[0]) + b_ref[0]


def _modulation(cond, w_mod, b_mod):
    depth, d, n = w_mod.shape
    tn = 1536
    return pl.pallas_call(
        _mod_body,
        grid=(depth, n // tn),
        in_specs=[pl.BlockSpec((N_MOD_GROUPS, d), lambda l, j: (0, 0)),
                  pl.BlockSpec((1, d, tn), lambda l, j: (l, 0, j)),
                  pl.BlockSpec((1, 1, tn), lambda l, j: (l, 0, j))],
        out_specs=pl.BlockSpec((1, N_MOD_GROUPS, tn), lambda l, j: (l, 0, j)),
        out_shape=jax.ShapeDtypeStruct((depth, N_MOD_GROUPS, n), F32),
        compiler_params=_params(2),
        name="modulation",
    )(cond, w_mod, b_mod.reshape(depth, 1, n))


def _modulate_body(x_ref, m_ref, h_ref, *, shift, scale):
    m = m_ref[0]
    h_ref[...] = (x_ref[...] * (1.0 + m[scale:scale + 1]) + m[shift:shift + 1]).astype(BF16)


def _modulate(x, mod, t_ctx, l_lat, shift, scale):
    t, d = x.shape
    tm = ROW_TILE
    return pl.pallas_call(
        functools.partial(_modulate_body, shift=shift, scale=scale),
        grid=(t // tm,),
        in_specs=[pl.BlockSpec((tm, d), lambda i: (i, 0)),
                  pl.BlockSpec((1, 6, d), lambda i: (_group_index(i, tm, t_ctx, l_lat), 0, 0))],
        out_specs=pl.BlockSpec((tm, d), lambda i: (i, 0)),
        out_shape=jax.ShapeDtypeStruct((t, d), BF16),
        compiler_params=_params(1),
        name="modulate",
    )(x, mod)


def _res_ln_body(x_ref, o_ref, m_ref, mn_ref, g_ref, b_ref, *outs,
                 alpha, gate, nshift, nscale, o_transposed, emit_h, emit_ht):
    o = o_ref[...]
    if o_transposed:
        o = o.T
    m = m_ref[0]
    y = alpha * x_ref[...] + m[gate:gate + 1] * o
    mu = jnp.mean(y, axis=-1, keepdims=True)
    yc = y - mu
    var = jnp.mean(yc * yc, axis=-1, keepdims=True)
    xn = yc * lax.rsqrt(var + LN_EPS) * g_ref[...] + b_ref[...]
    outs[0][...] = xn
    if emit_h or emit_ht:
        mn = mn_ref[0]
        h = xn * (1.0 + mn[nscale:nscale + 1]) + mn[nshift:nshift + 1]
        k = 1
        if emit_h:
            outs[k][...] = h.astype(BF16)
            k += 1
        if emit_ht:
            outs[k][...] = h.T.astype(BF16)


def _res_ln(x, o, mod, mod_next, ln_g, ln_b, t_ctx, l_lat, *, alpha, gate, nshift=0, nscale=1,
            o_transposed=False, emit_h=False, emit_ht=False):
    t, d = x.shape
    tm = ROW_TILE
    grp = lambda i: (_group_index(i, tm, t_ctx, l_lat), 0, 0)
    o_spec = (pl.BlockSpec((d, tm), lambda i: (0, i)) if o_transposed
              else pl.BlockSpec((tm, d), lambda i: (i, 0)))
    out_specs = [pl.BlockSpec((tm, d), lambda i: (i, 0))]
    out_shape = [jax.ShapeDtypeStruct((t, d), F32)]
    if emit_h:
        out_specs.append(pl.BlockSpec((tm, d), lambda i: (i, 0)))
        out_shape.append(jax.ShapeDtypeStruct((t, d), BF16))
    if emit_ht:
        out_specs.append(pl.BlockSpec((d, tm), lambda i: (0, i)))
        out_shape.append(jax.ShapeDtypeStruct((d, t), BF16))
    return pl.pallas_call(
        functools.partial(_res_ln_body, alpha=alpha, gate=gate, nshift=nshift, nscale=nscale,
                          o_transposed=o_transposed, emit_h=emit_h, emit_ht=emit_ht),
        grid=(t // tm,),
        in_specs=[pl.BlockSpec((tm, d), lambda i: (i, 0)), o_spec,
                  pl.BlockSpec((1, 6, d), grp), pl.BlockSpec((1, 6, d), grp),
                  pl.BlockSpec((1, d), lambda i: (0, 0)), pl.BlockSpec((1, d), lambda i: (0, 0))],
        out_specs=out_specs,
        out_shape=out_shape,
        compiler_params=_params(1),
        name="residual_layernorm",
    )(x, o, mod, mod_next, ln_g.reshape(1, d), ln_b.reshape(1, d))


def _linear_body(h_ref, w_ref, *refs, has_bias, act):
    if has_bias:
        b_ref, o_ref, wbf = refs
    else:
        o_ref, wbf = refs

    @pl.when(pl.program_id(1) == 0)
    def _():
        wbf[...] = w_ref[...].astype(BF16)

    y = jnp.dot(h_ref[...], wbf[...], preferred_element_type=F32)
    if has_bias:
        y = y + b_ref[...]
    if act == "gelu":
        y = _gelu(y)
    o_ref[...] = y


def _linear(h, w_stack, layer, n_out, bias=None, act=None):
    t, k = h.shape
    tm, tn = MM_ROW_TILE, min(MM_COL_TILE, n_out)
    in_specs = [pl.BlockSpec((tm, k), lambda n, i: (i, 0)),
                pl.BlockSpec((None, k, tn), lambda n, i: (layer, 0, n))]
    args = [h, w_stack]
    if bias is not None:
        in_specs.append(pl.BlockSpec((1, tn), lambda n, i: (0, n)))
        args.append(bias.reshape(1, n_out))
    return pl.pallas_call(
        functools.partial(_linear_body, has_bias=bias is not None, act=act),
        grid=(n_out // tn, t // tm),
        in_specs=in_specs,
        out_specs=pl.BlockSpec((tm, tn), lambda n, i: (i, n)),
        out_shape=jax.ShapeDtypeStruct((t, n_out), F32),
        scratch_shapes=[pltpu.VMEM((k, tn), BF16)],
        compiler_params=_params(2),
        name="linear",
    )(*args)


def _qkv_body(h_ref, w_ref, cos_ref, sa_ref, sb_ref, o_ref, wbf, *, n_ctx_tiles):
    n, i = pl.program_id(0), pl.program_id(1)

    @pl.when(i == 0)
    def _():
        wbf[...] = w_ref[...].astype(BF16)

    y = jnp.dot(h_ref[...], wbf[...], preferred_element_type=F32)
    rope = jnp.logical_and(n < 2, i >= n_ctx_tiles)
    half = ROPE_AXIS_DIM // 2

    @pl.when(rope)
    def _():
        up = pltpu.roll(y, y.shape[1] - half, 1)
        dn = pltpu.roll(y, half, 1)
        o_ref[...] = y * cos_ref[...] + up * sa_ref[...] + dn * sb_ref[...]

    @pl.when(jnp.logical_not(rope))
    def _():
        o_ref[...] = y


def _qkv_rope(h, w_stack, layer, tables, t_ctx, l_lat):
    t, k = h.shape
    n_out = w_stack.shape[2]
    tm, tn = MM_ROW_TILE, MM_COL_TILE
    n_ctx_tiles = t_ctx // tm
    pos = lambda n, i: (jnp.maximum(i - n_ctx_tiles, 0) % (l_lat // tm), 0)
    return pl.pallas_call(
        functools.partial(_qkv_body, n_ctx_tiles=n_ctx_tiles),
        grid=(n_out // tn, t // tm),
        in_specs=[pl.BlockSpec((tm, k), lambda n, i: (i, 0)),
                  pl.BlockSpec((None, k, tn), lambda n, i: (layer, 0, n)),
                  pl.BlockSpec((tm, tn), pos), pl.BlockSpec((tm, tn), pos), pl.BlockSpec((tm, tn), pos)],
        out_specs=pl.BlockSpec((tm, tn), lambda n, i: (i, n)),
        out_shape=jax.ShapeDtypeStruct((t, n_out), F32),
        scratch_shapes=[pltpu.VMEM((k, tn), BF16)],
        compiler_params=_params(2),
        name="qkv_rope",
    )(h, w_stack, *tables)


def _rope_tables(l_lat, width):
    rows = l_lat // GRID_W
    row_pos = jnp.repeat(jnp.arange(rows, dtype=F32), GRID_W)
    col_pos = jnp.tile(jnp.arange(GRID_W, dtype=F32), rows)
    inv_freq = ROPE_BASE ** (-jnp.arange(0, ROPE_AXIS_DIM, 2, dtype=F32) / ROPE_AXIS_DIM)

    def tab(p):
        ang = p[:, None] * inv_freq[None, :]
        ang = jnp.concatenate([ang, ang], -1)
        return jnp.cos(ang), jnp.sin(ang)

    cr, sr = tab(row_pos)
    cc, sc = tab(col_pos)
    cos = jnp.concatenate([cr, cc], -1)
    sin = jnp.concatenate([sr, sc], -1)
    first_half = (jnp.arange(ATTN_HEAD_DIM) % ROPE_AXIS_DIM) < (ROPE_AXIS_DIM // 2)
    sin_a = jnp.where(first_half, -sin, 0.0)
    sin_b = jnp.where(first_half, 0.0, sin)
    rep = width // ATTN_HEAD_DIM
    return tuple(jnp.tile(a, (1, rep)) for a in (cos, sin_a, sin_b))


def _attn_body(lam_ref, g_ref, q_ref, k_ref, v_ref, *refs, lam_init, past, tq):
    if past:
        kc_ref, vc_ref, o_ref, k_s, v_s = refs
        k_s[:past] = kc_ref[...].astype(BF16)
        v_s[:past] = vc_ref[...].astype(BF16)
    else:
        o_ref, k_s, v_s = refs
    k_s[past:] = k_ref[...].astype(BF16)
    v_s[past:] = v_ref[...].astype(BF16)
    lp = lam_ref[...]
    lam = (jnp.exp(jnp.sum(lp[0:1] * lp[1:2], axis=-1, keepdims=True))
           - jnp.exp(jnp.sum(lp[2:3] * lp[3:4], axis=-1, keepdims=True)) + lam_init)
    dh = ATTN_HEAD_DIM
    scale = dh ** -0.5

    def q_block(i, carry):
        rows = pl.ds(pl.multiple_of(i * tq, tq), tq)
        q = (q_ref[rows, :] * scale).astype(BF16)

        def softmax_map(m):
            s = _bdot_nt(q[:, m * dh:(m + 1) * dh], k_s[:, m * dh:(m + 1) * dh])
            e = jnp.exp(s - jnp.max(s, axis=-1, keepdims=True))
            return e / jnp.sum(e, axis=-1, keepdims=True)

        a = softmax_map(0) - lam * softmax_map(1)
        o = _bdot(a, v_s[...])
        o = o * lax.rsqrt(jnp.mean(o * o, axis=-1, keepdims=True) + LN_EPS) * g_ref[...]
        o_ref[rows, :] = (o * (1.0 - lam_init)).astype(o_ref.dtype)
        return carry

    lax.fori_loop(0, q_ref.shape[0] // tq, q_block, 0)


def _diff_attention(qkv, lam_p, subln_g, layer, lam_init, row0, n_batch, seq, cache=None):
    width = 2 * ATTN_HEAD_DIM
    base = row0 // seq
    in_specs = [pl.BlockSpec((None, 4, ATTN_HEAD_DIM), lambda b, h: (layer, 0, 0)),
                pl.BlockSpec((None, 1, width), lambda b, h: (layer, 0, 0)),
                pl.BlockSpec((seq, width), lambda b, h: (base + b, h)),
                pl.BlockSpec((seq, width), lambda b, h: (base + b, ATTN_HEADS + h)),
                pl.BlockSpec((seq, width), lambda b, h: (base + b, 2 * ATTN_HEADS + h))]
    args = [lam_p, subln_g.reshape(subln_g.shape[0], 1, width), qkv, qkv, qkv]
    past = 0
    if cache is not None:
        ck, cv = cache
        past = ck.shape[2]
        spec = pl.BlockSpec((None, None, past, width), lambda b, h: (b, layer, 0, h))
        in_specs += [spec, spec]
        args += [ck, cv]
    return pl.pallas_call(
        functools.partial(_attn_body, lam_init=lam_init, past=past, tq=128 if past else min(seq, 256)),
        grid=(n_batch, ATTN_HEADS),
        in_specs=in_specs,
        out_specs=pl.BlockSpec((seq, width), lambda b, h: (b, h)),
        out_shape=jax.ShapeDtypeStruct((n_batch * seq, ATTN_HEADS * width), BF16),
        scratch_shapes=[pltpu.VMEM((past + seq, width), BF16)] * 2,
        compiler_params=_params(2),
        name="diff_attention",
    )(*args)


def _conv_silu(x, w, b):
    n = x.shape[0]
    row = lax.broadcasted_iota(jnp.int32, x.shape, 0)
    acc = x * w[SSM_CONV // 2:SSM_CONV // 2 + 1] + b
    for k in range(SSM_CONV):
        sh = k - SSM_CONV // 2
        if sh == 0:
            continue
        shifted = pltpu.roll(x, (-sh) % n, 0)
        valid = jnp.logical_and(row + sh >= 0, row + sh < n)
        acc = acc + jnp.where(valid, shifted, 0.0) * w[k:k + 1]
    return _silu(acc)


def _softplus(x):
    return jnp.maximum(x, 0.0) + jnp.log1p(jnp.exp(-jnp.abs(x)))


def _ssd_body(*refs, has_h0, emit_state, n_chunks):
    (xs_ref, bm_ref, cm_ref, dt_ref, wx_ref, bx_ref, wb_ref, bb_ref, wc_ref, bc_ref,
     dtb_ref, a_ref, d_ref) = refs[:13]
    refs = refs[13:]
    if has_h0:
        h0_ref, refs = refs[0], refs[1:]
    y_ref, refs = refs[0], refs[1:]
    if emit_state:
        st_ref, refs = refs[0], refs[1:]
    xs_s, b_s, c_s, dt_s, da_s, xt_s, dtt_s, dat_s, h_s = refs
    q = SSD_CHUNK
    p = SSM_HEAD_DIM
    dot = functools.partial(jnp.dot, preferred_element_type=F32)

    xs = _conv_silu(xs_ref[...], wx_ref[...], bx_ref[...])
    xs_s[...] = xs
    y_ref[...] = d_ref[...] * xs
    b_s[...] = _conv_silu(bm_ref[...], wb_ref[...], bb_ref[...])
    c_s[...] = _conv_silu(cm_ref[...], wc_ref[...], bc_ref[...])
    dt = _softplus(dt_ref[...] + dtb_ref[...])
    dt_s[...] = dt
    da_s[...] = dt * a_ref[...]

    def transposes(c, carry):
        rows = pl.ds(pl.multiple_of(c * q, q), q)
        xt_s[c] = xs_s[rows, :].T
        dtt_s[c] = dt_s[rows, :].T
        dat_s[c] = da_s[rows, :].T
        return carry

    lax.fori_loop(0, n_chunks, transposes, 0)

    row = lax.broadcasted_iota(jnp.int32, (q, q), 0)
    col = lax.broadcasted_iota(jnp.int32, (q, q), 1)
    lower, upper = row >= col, row <= col
    ones = [jnp.where(m, 1.0, 0.0).astype(BF16) for m in (lower, upper)]
    for direction in range(2):
        for r in range(SSM_HPG):
            if has_h0:
                h_s[direction, r] = h0_ref[direction, r]
            else:
                h_s[direction, r] = jnp.zeros((p, SSM_STATE), F32)

    def scan_chunk(c, direction):
        mask = lower if direction == 0 else upper
        end = q - 1 if direction == 0 else 0
        rows = pl.ds(pl.multiple_of(c * q, q), q)
        xc = xs_s[rows, :]
        bc = b_s[rows, :].astype(BF16)
        cc = c_s[rows, :].astype(BF16)
        dtc = dt_s[rows, :]
        dt_t = dtt_s[c]
        x_t = xt_s[c]
        d_hi, d_mid, d_lo = _split3(da_s[rows, :])
        tri = ones[direction]
        acum = dot(tri, d_hi) + (dot(tri, d_mid) + dot(tri, d_lo))
        t_hi, t_mid, t_lo = _split3(dat_s[c])
        tri_t = ones[1 - direction]
        acum_t = dot(t_hi, tri_t) + (dot(t_mid, tri_t) + dot(t_lo, tri_t))
        cb = _bdot_nt(cc, bc)
        a_end = acum[end:end + 1, :]
        ys = []
        for r in range(SSM_HPG):
            hl = direction * SSM_HPG + r
            a_col = acum[:, hl:hl + 1]
            a_row = acum_t[hl:hl + 1, :]
            decay = jnp.exp(jnp.where(mask, a_col - a_row, NEG_BIG))
            xdt = xc[:, r * p:(r + 1) * p] * dtc[:, hl:hl + 1]
            h = h_s[direction, r]
            y = _bdot(cb * decay, xdt) + jnp.exp(a_col) * _bdot_nt(cc, h)
            w = dt_t[hl:hl + 1, :] * jnp.exp(a_end[:, hl:hl + 1] - a_row)
            h_s[direction, r] = (jnp.exp(a_end[:, hl:hl + 1]) * h
                                 + _bdot(x_t[r * p:(r + 1) * p, :] * w, bc))
            ys.append(y)
        return rows, jnp.concatenate(ys, axis=1)

    def step(ci, carry):
        rows_f, y_f = scan_chunk(ci, 0)
        rows_b, y_b = scan_chunk(n_chunks - 1 - ci, 1)
        y_ref[rows_f, :] = y_ref[rows_f, :] + y_f
        y_ref[rows_b, :] = y_ref[rows_b, :] + y_b
        return carry

    lax.fori_loop(0, n_chunks, step, 0)
    if emit_state:
        for direction in range(2):
            for r in range(SSM_HPG):
                st_ref[direction, r] = h_s[direction, r]


def _ssd(zxbc, dtp, conv_w, conv_b, dt_bias_l, a_l, d_l, row0, n_batch, seq, h0=None, layer=0,
         emit_state=False):
    inner, gs, hp = SSM_INNER, SSM_GROUPS * SSM_STATE, SSM_HPG * SSM_HEAD_DIM
    base = row0 // seq
    g_of = lambda off, w: (lambda b, g: (0, off // w + g))
    row_blk = lambda off, w: (lambda b, g: (base + b, off // w + g))
    in_specs = [pl.BlockSpec((seq, hp), row_blk(inner, hp)),
                pl.BlockSpec((seq, SSM_STATE), row_blk(2 * inner, SSM_STATE)),
                pl.BlockSpec((seq, SSM_STATE), row_blk(2 * inner + gs, SSM_STATE)),
                pl.BlockSpec((seq, 128), row_blk(0, 128)),
                pl.BlockSpec((SSM_CONV, hp), g_of(0, hp)), pl.BlockSpec((1, hp), g_of(0, hp)),
                pl.BlockSpec((SSM_CONV, SSM_STATE), g_of(inner, SSM_STATE)),
                pl.BlockSpec((1, SSM_STATE), g_of(inner, SSM_STATE)),
                pl.BlockSpec((SSM_CONV, SSM_STATE), g_of(inner + gs, SSM_STATE)),
                pl.BlockSpec((1, SSM_STATE), g_of(inner + gs, SSM_STATE)),
                pl.BlockSpec((1, 128), g_of(0, 128)), pl.BlockSpec((1, 128), g_of(0, 128)),
                pl.BlockSpec((1, hp), g_of(0, hp))]
    args = [zxbc, zxbc, zxbc, dtp, conv_w, conv_b, conv_w, conv_b, conv_w, conv_b, dt_bias_l, a_l, d_l]
    if h0 is not None:
        in_specs.append(pl.BlockSpec((None, None, 2, SSM_HPG, SSM_HEAD_DIM, SSM_STATE),
                                     lambda b, g: (b, layer, 0, g, 0, 0)))
        args.append(h0)
    out_specs = [pl.BlockSpec((seq, hp), lambda b, g: (b, g))]
    out_shape = [jax.ShapeDtypeStruct((n_batch * seq, inner), F32)]
    if emit_state:
        out_specs.append(pl.BlockSpec((None, 2, SSM_HPG, SSM_HEAD_DIM, SSM_STATE),
                                      lambda b, g: (b, 0, g, 0, 0)))
        out_shape.append(jax.ShapeDtypeStruct(
            (n_batch, 2, SSM_GROUPS * SSM_HPG, SSM_HEAD_DIM, SSM_STATE), F32))
    return pl.pallas_call(
        functools.partial(_ssd_body, has_h0=h0 is not None, emit_state=emit_state,
                          n_chunks=seq // SSD_CHUNK),
        grid=(n_batch, SSM_GROUPS),
        in_specs=in_specs,
        out_specs=out_specs,
        out_shape=out_shape,
        scratch_shapes=[pltpu.VMEM((seq, hp), F32), pltpu.VMEM((seq, SSM_STATE), F32),
                        pltpu.VMEM((seq, SSM_STATE), F32), pltpu.VMEM((seq, 128), F32),
                        pltpu.VMEM((seq, 128), F32),
                        pltpu.VMEM((seq // SSD_CHUNK, hp, SSD_CHUNK), F32),
                        pltpu.VMEM((seq // SSD_CHUNK, 128, SSD_CHUNK), F32),
                        pltpu.VMEM((seq // SSD_CHUNK, 128, SSD_CHUNK), F32),
                        pltpu.VMEM((2, SSM_HPG, SSM_HEAD_DIM, SSM_STATE), F32)],
        compiler_params=_params(2),
        name="ssd_scan",
    )(*args)


def _ssm_out_body(y_ref, z_ref, g_ref, w_ref, o_ref, wbf):
    @pl.when(pl.program_id(0) == 0)
    def _():
        wbf[...] = w_ref[...].astype(BF16)

    y = y_ref[...] * _silu(z_ref[...])
    y = y * lax.rsqrt(jnp.mean(y * y, axis=-1, keepdims=True) + LN_EPS) * g_ref[...]
    o_ref[...] = jnp.dot(y.astype(BF16), wbf[...], preferred_element_type=F32)


def _ssm_out(y, zxbc, norm_g, w_out, layer):
    t, inner = y.shape
    d = w_out.shape[2]
    tm = ROW_TILE
    return pl.pallas_call(
        _ssm_out_body,
        grid=(t // tm,),
        in_specs=[pl.BlockSpec((tm, inner), lambda i: (i, 0)),
                  pl.BlockSpec((tm, inner), lambda i: (i, 0)),
                  pl.BlockSpec((None, 1, inner), lambda i: (layer, 0, 0)),
                  pl.BlockSpec((None, inner, d), lambda i: (layer, 0, 0))],
        out_specs=pl.BlockSpec((tm, d), lambda i: (i, 0)),
        out_shape=jax.ShapeDtypeStruct((t, d), F32),
        scratch_shapes=[pltpu.VMEM((inner, d), BF16)],
        compiler_params=_params(1),
        name="ssm_gate_norm_out",
    )(y, zxbc, norm_g.reshape(norm_g.shape[0], 1, inner), w_out)


def _gmlp_body(u_ref, v_ref, g_ref, b_ref, ws_ref, bs_ref, w_ref, o_ref, wbf, t_s, *, n_chunks):
    @pl.when(pl.program_id(0) == 0)
    def _():
        wbf[...] = w_ref[...].astype(BF16)

    v = v_ref[...]
    mu = jnp.mean(v, axis=-1, keepdims=True)
    vc = v - mu
    var = jnp.mean(vc * vc, axis=-1, keepdims=True)
    vn = (vc * lax.rsqrt(var + LN_EPS) * g_ref[...] + b_ref[...]).astype(BF16)
    q, gd = GMLP_CHUNK, GMLP_GROUP_DIM
    for g in range(GMLP_GROUPS):
        ws = ws_ref[g].astype(BF16)
        bias = bs_ref[:, g:g + 1]
        for c in range(n_chunks):
            sv = jnp.dot(ws, vn[c * q:(c + 1) * q, g * gd:(g + 1) * gd],
                         preferred_element_type=F32) + bias
            t_s[c * q:(c + 1) * q, g * gd:(g + 1) * gd] = (
                u_ref[c * q:(c + 1) * q, g * gd:(g + 1) * gd] * sv).astype(BF16)
    o_ref[...] = jnp.dot(t_s[...], wbf[...], preferred_element_type=F32)


def _gmlp_mix_out(uv, ln_g, ln_b, w_s, b_s, w_out, layer):
    t = uv.shape[0]
    half, d = GMLP_HALF, w_out.shape[2]
    tm = 512
    return pl.pallas_call(
        functools.partial(_gmlp_body, n_chunks=tm // GMLP_CHUNK),
        grid=(t // tm,),
        in_specs=[pl.BlockSpec((tm, half), lambda i: (i, 0)),
                  pl.BlockSpec((tm, half), lambda i: (i, 1)),
                  pl.BlockSpec((None, 1, half), lambda i: (layer, 0, 0)),
                  pl.BlockSpec((None, 1, half), lambda i: (layer, 0, 0)),
                  pl.BlockSpec((None, GMLP_GROUPS, GMLP_CHUNK, GMLP_CHUNK), lambda i: (layer, 0, 0, 0)),
                  pl.BlockSpec((None, GMLP_CHUNK, GMLP_GROUPS), lambda i: (layer, 0, 0)),
                  pl.BlockSpec((None, half, d), lambda i: (layer, 0, 0))],
        out_specs=pl.BlockSpec((tm, d), lambda i: (i, 0)),
        out_shape=jax.ShapeDtypeStruct((t, d), F32),
        scratch_shapes=[pltpu.VMEM((half, d), BF16), pltpu.VMEM((tm, half), BF16)],
        compiler_params=_params(1),
        name="gmlp_mix_out",
    )(uv, uv, ln_g.reshape(-1, 1, half), ln_b.reshape(-1, 1, half), w_s,
      jnp.swapaxes(b_s, 1, 2), w_out)


def _sorting_network(lo, hi):
    def merge(lo, hi, r):
        step = r * 2
        if step < hi - lo:
            yield from merge(lo, hi, step)
            yield from merge(lo + r, hi, step)
            yield from [(i, i + r) for i in range(lo + r, hi - r, step)]
        else:
            yield (lo, lo + r)

    if hi - lo >= 1:
        mid = lo + (hi - lo) // 2
        yield from _sorting_network(lo, mid)
        yield from _sorting_network(mid + 1, hi)
        yield from merge(lo, hi, 1)


def _top_rows(s, n, n_out):
    sub = 8
    n_slabs = s.shape[0] // sub
    v = [s[i * sub:(i + 1) * sub] for i in range(n_slabs)]
    for i, j in _sorting_network(0, n_slabs - 1):
        v[i], v[j] = jnp.maximum(v[i], v[j]), jnp.minimum(v[i], v[j])
    slot = lax.broadcasted_iota(jnp.int32, (n_out, s.shape[1]), 0)
    out = jnp.full((n_out, s.shape[1]), NEG_BIG, F32)
    for k in range(n):
        m = jnp.max(v[0], axis=0, keepdims=True)
        out = jnp.where(slot == k, m, out)
        hit = v[0] >= m
        depth = min(n_slabs, n - 1 - k)
        for i in range(depth):
            v[i] = jnp.where(hit, v[i + 1] if i + 1 < n_slabs else NEG_BIG, v[i])
    return out


def _peer_score_body(ht_ref, wq_ref, keys_ref, thr_ref, e1_ref, s2_ref, e2_ref, wbf):
    @pl.when(pl.program_id(0) == 0)
    def _():
        wbf[...] = wq_ref[...].astype(BF16)

    ht = ht_ref[...]
    kk, kh, sub = PEER_TOPK, PEER_HALF, PEER_ROWS_PER_BLOCK
    assert kk == 16 and sub == 8
    n_top = 3 * sub
    rank8 = lax.broadcasted_iota(jnp.int32, (sub, ht.shape[1]), 0)
    for h in range(PEER_HEADS):
        qh = jnp.dot(wbf[2 * kh * h:2 * kh * (h + 1), :], ht, preferred_element_type=F32)
        s1 = _dot_x3(keys_ref[h, 0], qh[:kh])
        s2 = _dot_x3(keys_ref[h, 1], qh[kh:])
        t1 = _top_rows(s1, kk + 1, n_top)
        t2 = _top_rows(s2, kk + 1, n_top)
        low, one = t1[0:sub], t2[0:sub]
        cand = jnp.concatenate([
            t1[0:1] + one,
            t1[0:1] + t2[sub:kk],
            t1[1:2] + one,
            jnp.where(rank8 < 5, t1[2:3] + one, NEG_BIG),
            jnp.where(rank8 < 4, t1[3:4] + one, low + t2[1:2]),
            jnp.where(rank8 >= 4, low + t2[0:1], NEG_BIG),
            t1[sub:kk] + t2[0:1],
            jnp.where(rank8 == 0, t1[kk:kk + 1] + t2[0:1],
                      jnp.where(rank8 == 1, t1[0:1] + t2[kk:kk + 1],
                                jnp.where(rank8 == 2, t1[4:5] + t2[2:3], NEG_BIG)))], axis=0)
        best = _top_rows(cand, kk + 1, n_top)
        z = jnp.sum(jnp.exp(best[:kk] - best[0:1]), axis=0, keepdims=True)
        thr = 0.5 * (best[kk - 1:kk] + best[kk:kk + 1]) - s1
        e1 = jnp.exp(s1 - t1[0:1])
        for rb in range(PEER_KEYS // sub):
            thr_ref[h, rb] = thr[rb * sub:(rb + 1) * sub]
            e1_ref[h, rb] = e1[rb * sub:(rb + 1) * sub]
        s2_ref[h] = s2
        e2_ref[h] = jnp.exp(s2 - t2[0:1]) * (0.5 / z)


def _peer_scores(ht, wq_t, sub_keys, layer):
    d, t = ht.shape
    tt = PEER_TOKEN_TILE
    nq = wq_t.shape[1]
    sub = PEER_ROWS_PER_BLOCK
    row_spec = pl.BlockSpec((PEER_HEADS, PEER_KEYS // sub, sub, tt), lambda i: (0, 0, 0, i))
    row_shape = jax.ShapeDtypeStruct((PEER_HEADS, PEER_KEYS // sub, sub, t), F32)
    tok_spec = pl.BlockSpec((PEER_HEADS, PEER_KEYS, tt), lambda i: (0, 0, i))
    tok_shape = jax.ShapeDtypeStruct((PEER_HEADS, PEER_KEYS, t), F32)
    return pl.pallas_call(
        _peer_score_body,
        grid=(t // tt,),
        in_specs=[pl.BlockSpec((d, tt), lambda i: (0, i)),
                  pl.BlockSpec((None, nq, d), lambda i: (layer, 0, 0)),
                  pl.BlockSpec((None, PEER_HEADS, 2, PEER_KEYS, PEER_HALF), lambda i: (layer, 0, 0, 0, 0))],
        out_specs=[row_spec, row_spec, tok_spec, tok_spec],
        out_shape=[row_shape, row_shape, tok_shape, tok_shape],
        scratch_shapes=[pltpu.VMEM((nq, d), BF16)],
        compiler_params=_params(1),
        name="peer_scores",
    )(ht, wq_t, sub_keys)


def _peer_tables_body(u_ref, v_ref, ub_ref, vt_ref):
    ub_ref[...] = u_ref[...].astype(BF16)
    vt_ref[...] = v_ref[...].T.astype(BF16)


def _peer_tables(u_tab, v_tab, layer):
    n, d = u_tab.shape[1:]
    eb = MM_COL_TILE
    return pl.pallas_call(
        _peer_tables_body,
        grid=(n // eb,),
        in_specs=[pl.BlockSpec((None, eb, d), lambda e: (layer, e, 0)),
                  pl.BlockSpec((None, eb, d), lambda e: (layer, e, 0))],
        out_specs=[pl.BlockSpec((eb, d), lambda e: (e, 0)), pl.BlockSpec((d, eb), lambda e: (0, e))],
        out_shape=[jax.ShapeDtypeStruct((n, d), BF16), jax.ShapeDtypeStruct((d, n), BF16)],
        compiler_params=_params(1),
        name="peer_tables",
    )(u_tab, v_tab)


MXU_TILE = 256
MXU_ROWS = 1024


def _mxu_slice(mxu, k, lhs_ref, rhs_ref, rhs_cols):
    ks = slice(k * MXU_TILE, (k + 1) * MXU_TILE)
    pltpu.matmul_push_rhs(rhs_ref[ks, rhs_cols], k % 2, mxu)
    pltpu.matmul_acc_lhs(0, lhs_ref[:, ks], mxu, k % 2)


def _mxu_result(mxu):
    return pltpu.matmul_pop(0, (MXU_ROWS, MXU_TILE), F32, mxu)


def _peer_mix_body(ht_ref, thr_ref, e1_ref, s2_ref, e2_ref, u_ref, vt_ref, o_ref, act_s, coef_s):
    e = pl.program_id(1)
    lanes, kq = 128, 32
    n_q = PEER_KEYS // kq
    c0 = math.sqrt(2.0 / math.pi)
    par = e % 2
    assert u_ref.shape[0] == MXU_ROWS and vt_ref.shape[0] == MXU_ROWS

    @pl.when(e == 0)
    def _():
        o_ref[...] = jnp.zeros(o_ref.shape, F32)
        act_s[1] = jnp.zeros(act_s.shape[1:], F32)
        coef_s[...] = jnp.zeros(coef_s.shape, BF16)

    def gate_chunk(part, q):
        c = pl.ds(pl.multiple_of(part * lanes, lanes), lanes)
        keys = slice(q * kq, (q + 1) * kq)
        gates = [jnp.zeros((kq, lanes), F32)] * PEER_ROWS_PER_BLOCK
        for h in range(PEER_HEADS):
            s2 = s2_ref[h, keys, c]
            e2 = e2_ref[h, keys, c]
            for r in range(PEER_ROWS_PER_BLOCK):
                chosen = s2 >= thr_ref[h, 0, r:r + 1, c]
                gates[r] = gates[r] + jnp.where(chosen, e2, 0.0) * e1_ref[h, 0, r:r + 1, c]
        for r in range(PEER_ROWS_PER_BLOCK):
            rows = slice(r * PEER_KEYS + q * kq, r * PEER_KEYS + (q + 1) * kq)
            a = act_s[1 - par, rows, c]
            twice_gelu = a * (1.0 + jnp.tanh(a * (c0 + (c0 * 0.044715) * (a * a))))
            coef_s[1 - par, rows, c] = (twice_gelu * gates[r]).astype(BF16)

    def piece(n, carry):
        cols = pl.ds(pl.multiple_of(n * MXU_TILE, MXU_TILE), MXU_TILE)
        chunks = [(part, q) for part in range(MXU_TILE // lanes) for q in range(n_q)]
        for k in range(u_ref.shape[1] // MXU_TILE):
            _mxu_slice(0, k, u_ref, ht_ref, cols)
            _mxu_slice(1, k, vt_ref, coef_s.at[par], cols)
        n_tail = 1
        for part, q in chunks[:-n_tail]:
            gate_chunk(n * (MXU_TILE // lanes) + part, q)
        act_s[par, :, cols] = _mxu_result(0)
        o_ref[:, cols] = o_ref[:, cols] + _mxu_result(1)
        for part, q in chunks[-n_tail:]:
            gate_chunk(n * (MXU_TILE // lanes) + part, q)
        return carry

    lax.fori_loop(0, o_ref.shape[1] // MXU_TILE, piece, 0)


def _peer_mix(ht, thr, e1, s2, e2, u_bf, vt_bf):
    d, t = ht.shape
    ts, rb = PEER_SUPER_TILE, PEER_ROWS_PER_BLOCK
    eb = rb * PEER_KEYS
    n_blocks = u_bf.shape[0] // eb
    assert eb == d == MXU_ROWS
    blk = lambda e, lag: jnp.clip(e - lag, 0, n_blocks - 1)
    row_spec = pl.BlockSpec((PEER_HEADS, 1, rb, ts), lambda s, e: (0, blk(e, 1), 0, s))
    tok_spec = pl.BlockSpec((PEER_HEADS, PEER_KEYS, ts), lambda s, e: (0, 0, s))
    return pl.pallas_call(
        _peer_mix_body,
        grid=(t // ts, n_blocks + 2),
        in_specs=[pl.BlockSpec((d, ts), lambda s, e: (0, s)), row_spec, row_spec, tok_spec, tok_spec,
                  pl.BlockSpec((eb, d), lambda s, e: (blk(e, 0), 0)),
                  pl.BlockSpec((d, eb), lambda s, e: (0, blk(e, 2)))],
        out_specs=pl.BlockSpec((d, ts), lambda s, e: (0, s)),
        out_shape=jax.ShapeDtypeStruct((d, t), F32),
        scratch_shapes=[pltpu.VMEM((2, eb, ts), F32), pltpu.VMEM((2, eb, ts), BF16)],
        compiler_params=_params(2),
        name="peer_mix",
    )(ht, thr, e1, s2, e2, u_bf, vt_bf)


def kernel(x_prompt, x_sample, cache_k, cache_v, state_ssm, c, c_ctx, w_mod, b_mod, ln_g, ln_b, attn_w_qkv, attn_w_o, attn_lambda, attn_subln_g, ssm_w_in, ssm_conv_w, ssm_conv_b, ssm_dt_bias, ssm_a_log, ssm_d, ssm_norm_g, ssm_w_out, gmlp_w_in, gmlp_b_in, gmlp_ln_g, gmlp_ln_b, gmlp_w_s, gmlp_b_s, gmlp_w_out, peer_w_q, peer_sub_keys, peer_u, peer_v):
    bc, lc, d = x_prompt.shape
    bl, ll, _ = x_sample.shape
    depth = w_mod.shape[0]
    t_ctx, t_lat = bc * lc, bl * ll
    alpha = (2 * depth) ** 0.25
    assert 1 + bl <= N_MOD_GROUPS and t_ctx % ll == 0

    x = jnp.concatenate([x_prompt.reshape(t_ctx, d), x_sample.reshape(t_lat, d)], axis=0)
    cond = jnp.zeros((N_MOD_GROUPS, d), F32).at[0].set(c_ctx).at[1:1 + bl].set(c)
    mods = _modulation(cond, w_mod, b_mod).reshape(depth, N_MOD_GROUPS, 6, d)
    rope = _rope_tables(ll, attn_w_qkv.shape[2] // 3)
    past = cache_k.shape[2]
    ck = cache_k.reshape(bl, cache_k.shape[1], past, -1)
    cv = cache_v.reshape(bl, cache_v.shape[1], past, -1)
    wq_t = jnp.swapaxes(peer_w_q, 1, 2)

    new_k, new_v, new_s = [], [], []
    h = _modulate(x, mods[0], t_ctx, ll, shift=0, scale=1)
    for i in range(depth):
        kind, j = i % 3, i // 3
        if kind == 0:
            lam_init = 0.8 - 0.6 * math.exp(-0.3 * i)
            qkv = _qkv_rope(h, attn_w_qkv, j, rope, t_ctx, ll)
            width = qkv.shape[1] // 3
            o_ctx = _diff_attention(qkv, attn_lambda, attn_subln_g, j, lam_init, 0, bc, lc)
            o_lat = _diff_attention(qkv, attn_lambda, attn_subln_g, j, lam_init, t_ctx, bl, ll,
                                    cache=(ck, cv))
            o = _linear(jnp.concatenate([o_ctx, o_lat], axis=0).astype(BF16), attn_w_o, j, d)
            new_k.append(qkv[:t_ctx, width:2 * width].reshape(bc, lc, ATTN_HEADS, 2, ATTN_HEAD_DIM))
            new_v.append(qkv[:t_ctx, 2 * width:].reshape(bc, lc, ATTN_HEADS, 2 * ATTN_HEAD_DIM))
        elif kind == 1:
            n_main = SSM_INNER + SSM_INNER + 2 * SSM_GROUPS * SSM_STATE
            zxbc = _linear(h, ssm_w_in, j, n_main)
            def regroup(a):
                a = a.reshape(a.shape[:-1] + (2, SSM_GROUPS, SSM_HPG))
                a = jnp.moveaxis(a, -3, -2).reshape(a.shape[:-3] + (SSM_GROUPS, 2 * SSM_HPG))
                pad = [(0, 0)] * (a.ndim - 1) + [(0, 128 - 2 * SSM_HPG)]
                return jnp.pad(a, pad).reshape(a.shape[:-2] + (SSM_GROUPS * 128,))
            w_dt = regroup(ssm_w_in[j][:, n_main:])[None]
            dtp = _linear(h, w_dt, 0, SSM_GROUPS * 128)
            dtb = regroup(ssm_dt_bias[j].reshape(1, -1))
            a_l = regroup(-jnp.exp(ssm_a_log[j]).reshape(1, -1))
            d_l = jnp.repeat(ssm_d[j], SSM_HEAD_DIM).reshape(1, SSM_INNER)
            cw, cb = ssm_conv_w[j], ssm_conv_b[j].reshape(1, -1)
            y_ctx, st = _ssd(zxbc, dtp, cw, cb, dtb, a_l, d_l, 0, bc, lc, emit_state=True)
            (y_lat,) = _ssd(zxbc, dtp, cw, cb, dtb, a_l, d_l, t_ctx, bl, ll, h0=state_ssm, layer=j)
            o = _ssm_out(jnp.concatenate([y_ctx, y_lat], axis=0), zxbc, ssm_norm_g, ssm_w_out, j)
            new_s.append(st)
        else:
            uv = _linear(h, gmlp_w_in, j, 2 * GMLP_HALF, bias=gmlp_b_in[j], act="gelu")
            o = _gmlp_mix_out(uv, gmlp_ln_g, gmlp_ln_b, gmlp_w_s, gmlp_b_s, gmlp_w_out, j)
        x, ht = _res_ln(x, o, mods[i], mods[i], ln_g[i, 0], ln_b[i, 0], t_ctx, ll, alpha=alpha,
                        gate=2, nshift=3, nscale=4, emit_ht=True)
        thr, e1, s2, e2 = _peer_scores(ht, wq_t, peer_sub_keys, i)
        u_bf, vt_bf = _peer_tables(peer_u, peer_v, i)
        f_t = _peer_mix(ht, thr, e1, s2, e2, u_bf, vt_bf)
        if i + 1 < depth:
            x, h = _res_ln(x, f_t, mods[i], mods[i + 1], ln_g[i, 1], ln_b[i, 1], t_ctx, ll,
                           alpha=alpha, gate=5, nshift=0, nscale=1, o_transposed=True, emit_h=True)
        else:
            (x,) = _res_ln(x, f_t, mods[i], mods[i], ln_g[i, 1], ln_b[i, 1], t_ctx, ll,
                           alpha=alpha, gate=5, o_transposed=True)
    return (x[:t_ctx].reshape(bc, lc, d), x[t_ctx:].reshape(bl, ll, d),
            jnp.stack(new_k, axis=1), jnp.stack(new_v, axis=1), jnp.stack(new_s, axis=1))
```

```python
import functools
import math

import jax
import jax.numpy as jnp
from jax import lax
from jax.experimental import pallas as pl
from jax.experimental.pallas import tpu as pltpu

F32 = jnp.float32
BF16 = jnp.bfloat16

D_MODEL = 1024
LN_EPS = 1e-5
GRID_W = 64
ATTN_HEADS = 8
ATTN_HEAD_DIM = 64
ROPE_AXIS_DIM = ATTN_HEAD_DIM // 2
ROPE_BASE = 10000.0
SSM_INNER = 2 * D_MODEL
SSM_HEAD_DIM = 64
SSM_GROUPS = 8
SSM_HPG = 4
SSM_STATE = 128
SSM_CONV = 5
SSD_CHUNK = 128
GMLP_HALF = 2 * D_MODEL
GMLP_GROUPS = 8
GMLP_GROUP_DIM = GMLP_HALF // GMLP_GROUPS
GMLP_CHUNK = 128
PEER_HEADS = 8
PEER_KEYS = 128
PEER_HALF = 128
PEER_TOPK = 16
N_MOD_GROUPS = 8
NEG_BIG = -1e30

VMEM_LIMIT = 56 * 1024 * 1024
ROW_TILE = 512
SSM_OUT_ROW_TILE = 256
MM_ROW_TILE = 1024
QKV_ROW_TILE = 512
MM_COL_TILE = 1024
PEER_TOKEN_TILE = 256
PEER_SUPER_TILE = 1024
PEER_ROWS_PER_BLOCK = 8


def _params(n_axes):
    return pltpu.CompilerParams(dimension_semantics=("arbitrary",) * n_axes,
                                vmem_limit_bytes=VMEM_LIMIT)


def _bdot(a, b):
    return jnp.dot(a.astype(BF16), b.astype(BF16), preferred_element_type=F32)


def _bdot_nt(a, b):
    return lax.dot_general(a.astype(BF16), b.astype(BF16), (((1,), (1,)), ((), ())),
                           preferred_element_type=F32)


def _split3(a):
    hi = a.astype(BF16)
    r = a - hi.astype(F32)
    mid = r.astype(BF16)
    lo = (r - mid.astype(F32)).astype(BF16)
    return hi, mid, lo


def _dot_x3(a, b):
    ah, am, _ = _split3(a)
    bh, bm, _ = _split3(b)
    d = functools.partial(jnp.dot, preferred_element_type=F32)
    return d(ah, bh) + (d(am, bh) + d(ah, bm))


def _silu(x):
    return x * jax.nn.sigmoid(x)


def _gelu(x):
    return 0.5 * x * (1.0 + jnp.tanh(math.sqrt(2.0 / math.pi) * (x + 0.044715 * (x * x * x))))


def _group_index(i, tm, t_ctx, l_lat):
    start = i * tm
    return jnp.where(start < t_ctx, 0, 1 + (start - t_ctx) // l_lat)


def _mod_body(c_ref, w_ref, b_ref, o_ref):
    o_ref[0] = _dot_x3(_silu(c_ref[...]), w_ref[0]) + b_ref[0]


def _modulation(cond, w_mod, b_mod):
    depth, d, n = w_mod.shape
    tn = 1536
    return pl.pallas_call(
        _mod_body,
        grid=(depth, n // tn),
        in_specs=[pl.BlockSpec((N_MOD_GROUPS, d), lambda l, j: (0, 0)),
                  pl.BlockSpec((1, d, tn), lambda l, j: (l, 0, j)),
                  pl.BlockSpec((1, 1, tn), lambda l, j: (l, 0, j))],
        out_specs=pl.BlockSpec((1, N_MOD_GROUPS, tn), lambda l, j: (l, 0, j)),
        out_shape=jax.ShapeDtypeStruct((depth, N_MOD_GROUPS, n), F32),
        compiler_params=_params(2),
        name="modulation",
    )(cond, w_mod, b_mod.reshape(depth, 1, n))


def _modulate_body(x_ref, m_ref, h_ref, *, shift, scale):
    m = m_ref[0]
    h_ref[...] = (x_ref[...] * (1.0 + m[scale:scale + 1]) + m[shift:shift + 1]).astype(BF16)


def _modulate(x, mod, t_ctx, l_lat, shift, scale):
    t, d = x.shape
    tm = ROW_TILE
    return pl.pallas_call(
        functools.partial(_modulate_body, shift=shift, scale=scale),
        grid=(t // tm,),
        in_specs=[pl.BlockSpec((tm, d), lambda i: (i, 0)),
                  pl.BlockSpec((1, 6, d), lambda i: (_group_index(i, tm, t_ctx, l_lat), 0, 0))],
        out_specs=pl.BlockSpec((tm, d), lambda i: (i, 0)),
        out_shape=jax.ShapeDtypeStruct((t, d), BF16),
        compiler_params=_params(1),
        name="modulate",
    )(x, mod)


def _res_ln_body(x_ref, o_ref, m_ref, mn_ref, g_ref, b_ref, *outs,
                 alpha, gate, nshift, nscale, o_transposed, emit_h, emit_ht):
    o = o_ref[...]
    if o_transposed:
        o = o.T
    m = m_ref[0]
    y = alpha * x_ref[...] + m[gate:gate + 1] * o
    mu = jnp.mean(y, axis=-1, keepdims=True)
    yc = y - mu
    var = jnp.mean(yc * yc, axis=-1, keepdims=True)
    xn = yc * lax.rsqrt(var + LN_EPS) * g_ref[...] + b_ref[...]
    outs[0][...] = xn
    if emit_h or emit_ht:
        mn = mn_ref[0]
        h = xn * (1.0 + mn[nscale:nscale + 1]) + mn[nshift:nshift + 1]
        k = 1
        if emit_h:
            outs[k][...] = h.astype(BF16)
            k += 1
        if emit_ht:
            outs[k][...] = h.T.astype(BF16)


def _res_ln(x, o, mod, mod_next, ln_g, ln_b, t_ctx, l_lat, *, alpha, gate, nshift=0, nscale=1,
            o_transposed=False, emit_h=False, emit_ht=False):
    t, d = x.shape
    tm = ROW_TILE
    grp = lambda i: (_group_index(i, tm, t_ctx, l_lat), 0, 0)
    o_spec = (pl.BlockSpec((d, tm), lambda i: (0, i)) if o_transposed
              else pl.BlockSpec((tm, d), lambda i: (i, 0)))
    out_specs = [pl.BlockSpec((tm, d), lambda i: (i, 0))]
    out_shape = [jax.ShapeDtypeStruct((t, d), F32)]
    if emit_h:
        out_specs.append(pl.BlockSpec((tm, d), lambda i: (i, 0)))
        out_shape.append(jax.ShapeDtypeStruct((t, d), BF16))
    if emit_ht:
        out_specs.append(pl.BlockSpec((d, tm), lambda i: (0, i)))
        out_shape.append(jax.ShapeDtypeStruct((d, t), BF16))
    return pl.pallas_call(
        functools.partial(_res_ln_body, alpha=alpha, gate=gate, nshift=nshift, nscale=nscale,
                          o_transposed=o_transposed, emit_h=emit_h, emit_ht=emit_ht),
        grid=(t // tm,),
        in_specs=[pl.BlockSpec((tm, d), lambda i: (i, 0)), o_spec,
                  pl.BlockSpec((1, 6, d), grp), pl.BlockSpec((1, 6, d), grp),
                  pl.BlockSpec((1, d), lambda i: (0, 0)), pl.BlockSpec((1, d), lambda i: (0, 0))],
        out_specs=out_specs,
        out_shape=out_shape,
        compiler_params=_params(1),
        name="residual_layernorm",
    )(x, o, mod, mod_next, ln_g.reshape(1, d), ln_b.reshape(1, d))


def _linear_body(h_ref, w_ref, *refs, has_bias, act):
    if has_bias:
        b_ref, o_ref, wbf = refs
    else:
        o_ref, wbf = refs

    @pl.when(pl.program_id(1) == 0)
    def _():
        wbf[...] = w_ref[...].astype(BF16)

    y = jnp.dot(h_ref[...], wbf[...], preferred_element_type=F32)
    if has_bias:
        y = y + b_ref[...]
    if act == "gelu":
        y = _gelu(y)
    o_ref[...] = y


def _linear(h, w_stack, layer, n_out, bias=None, act=None):
    t, k = h.shape
    tm, tn = MM_ROW_TILE, min(MM_COL_TILE, n_out)
    in_specs = [pl.BlockSpec((tm, k), lambda n, i: (i, 0)),
                pl.BlockSpec((None, k, tn), lambda n, i: (layer, 0, n))]
    args = [h, w_stack]
    if bias is not None:
        in_specs.append(pl.BlockSpec((1, tn), lambda n, i: (0, n)))
        args.append(bias.reshape(1, n_out))
    return pl.pallas_call(
        functools.partial(_linear_body, has_bias=bias is not None, act=act),
        grid=(n_out // tn, t // tm),
        in_specs=in_specs,
        out_specs=pl.BlockSpec((tm, tn), lambda n, i: (i, n)),
        out_shape=jax.ShapeDtypeStruct((t, n_out), F32),
        scratch_shapes=[pltpu.VMEM((k, tn), BF16)],
        compiler_params=_params(2),
        name="linear",
    )(*args)


def _qkv_body(h_ref, w_ref, cos_ref, sa_ref, sb_ref, o_ref, kc_ref, vc_ref, wbf, *, n_ctx_tiles):
    n, i = pl.program_id(0), pl.program_id(1)

    @pl.when(i == 0)
    def _():
        wbf[...] = w_ref[...].astype(BF16)

    y = jnp.dot(h_ref[...], wbf[...], preferred_element_type=F32)
    rope = jnp.logical_and(n < 2, i >= n_ctx_tiles)
    half = ROPE_AXIS_DIM // 2

    @pl.when(jnp.logical_and(n == 1, i < n_ctx_tiles))
    def _():
        kc_ref[...] = y

    @pl.when(jnp.logical_and(n == 2, i < n_ctx_tiles))
    def _():
        vc_ref[...] = y

    @pl.when(rope)
    def _():
        up = pltpu.roll(y, y.shape[1] - half, 1)
        dn = pltpu.roll(y, half, 1)
        o_ref[...] = y * cos_ref[...] + up * sa_ref[...] + dn * sb_ref[...]

    @pl.when(jnp.logical_not(rope))
    def _():
        o_ref[...] = y


def _qkv_rope(h, w_stack, layer, tables, t_ctx, l_lat):
    t, k = h.shape
    n_out = w_stack.shape[2]
    tm, tn = QKV_ROW_TILE, MM_COL_TILE
    assert n_out == 3 * tn
    n_ctx_tiles = t_ctx // tm
    pos = lambda n, i: (jnp.maximum(i - n_ctx_tiles, 0) % (l_lat // tm), 0)
    last = n_ctx_tiles - 1
    k_rows = lambda n, i: (jnp.where(n == 1, jnp.minimum(i, last), jnp.where(n < 1, 0, last)), 0)
    v_rows = lambda n, i: (jnp.where(n == 2, jnp.minimum(i, last), 0), 0)
    ctx_shape = jax.ShapeDtypeStruct((t_ctx, tn), F32)
    return pl.pallas_call(
        functools.partial(_qkv_body, n_ctx_tiles=n_ctx_tiles),
        grid=(n_out // tn, t // tm),
        in_specs=[pl.BlockSpec((tm, k), lambda n, i: (i, 0)),
                  pl.BlockSpec((None, k, tn), lambda n, i: (layer, 0, n)),
                  pl.BlockSpec((tm, tn), pos), pl.BlockSpec((tm, tn), pos), pl.BlockSpec((tm, tn), pos)],
        out_specs=[pl.BlockSpec((tm, tn), lambda n, i: (i, n)),
                   pl.BlockSpec((tm, tn), k_rows), pl.BlockSpec((tm, tn), v_rows)],
        out_shape=[jax.ShapeDtypeStruct((t, n_out), F32), ctx_shape, ctx_shape],
        scratch_shapes=[pltpu.VMEM((k, tn), BF16)],
        compiler_params=_params(2),
        name="qkv_rope",
    )(h, w_stack, *tables)


def _rope_tables(l_lat, width):
    rows = l_lat // GRID_W
    row_pos = jnp.repeat(jnp.arange(rows, dtype=F32), GRID_W)
    col_pos = jnp.tile(jnp.arange(GRID_W, dtype=F32), rows)
    inv_freq = ROPE_BASE ** (-jnp.arange(0, ROPE_AXIS_DIM, 2, dtype=F32) / ROPE_AXIS_DIM)

    def tab(p):
        ang = p[:, None] * inv_freq[None, :]
        ang = jnp.concatenate([ang, ang], -1)
        return jnp.cos(ang), jnp.sin(ang)

    cr, sr = tab(row_pos)
    cc, sc = tab(col_pos)
    cos = jnp.concatenate([cr, cc], -1)
    sin = jnp.concatenate([sr, sc], -1)
    first_half = (jnp.arange(ATTN_HEAD_DIM) % ROPE_AXIS_DIM) < (ROPE_AXIS_DIM // 2)
    sin_a = jnp.where(first_half, -sin, 0.0)
    sin_b = jnp.where(first_half, 0.0, sin)
    rep = width // ATTN_HEAD_DIM
    return tuple(jnp.tile(a, (1, rep)) for a in (cos, sin_a, sin_b))


def _attn_body(lam_ref, g_ref, q_ref, k_ref, v_ref, *refs, lam_init, past, tq):
    if past:
        kc_ref, vc_ref, o_ref, k_s, v_s = refs
        k_s[:past] = kc_ref[...].astype(BF16)
        v_s[:past] = vc_ref[...].astype(BF16)
    else:
        o_ref, k_s, v_s = refs
    k_s[past:] = k_ref[...].astype(BF16)
    v_s[past:] = v_ref[...].astype(BF16)
    lp = lam_ref[...]
    lam = (jnp.exp(jnp.sum(lp[0:1] * lp[1:2], axis=-1, keepdims=True))
           - jnp.exp(jnp.sum(lp[2:3] * lp[3:4], axis=-1, keepdims=True)) + lam_init)
    dh = ATTN_HEAD_DIM
    scale = dh ** -0.5

    def q_block(i, carry):
        rows = pl.ds(pl.multiple_of(i * tq, tq), tq)
        for c0 in range(0, q_ref.shape[1], 2 * dh):
            q = (q_ref[rows, c0:c0 + 2 * dh] * scale).astype(BF16)

            def softmax_map(m, q=q, c0=c0):
                s = _bdot_nt(q[:, m * dh:(m + 1) * dh], k_s[:, c0 + m * dh:c0 + (m + 1) * dh])
                e = jnp.exp(s - jnp.max(s, axis=-1, keepdims=True))
                return e / jnp.sum(e, axis=-1, keepdims=True)

            a = softmax_map(0) - lam * softmax_map(1)
            o = _bdot(a, v_s[:, c0:c0 + 2 * dh])
            o = o * lax.rsqrt(jnp.mean(o * o, axis=-1, keepdims=True) + LN_EPS) * g_ref[...]
            o_ref[rows, c0:c0 + 2 * dh] = (o * (1.0 - lam_init)).astype(o_ref.dtype)
        return carry

    lax.fori_loop(0, q_ref.shape[0] // tq, q_block, 0)


def _diff_attention(qkv, lam_p, subln_g, layer, lam_init, row0, n_batch, seq, cache=None):
    head_w = 2 * ATTN_HEAD_DIM
    hps = 1 if cache is not None else 4
    width, groups = hps * head_w, ATTN_HEADS // hps
    base = row0 // seq
    in_specs = [pl.BlockSpec((None, 4, ATTN_HEAD_DIM), lambda b, h: (layer, 0, 0)),
                pl.BlockSpec((None, 1, head_w), lambda b, h: (layer, 0, 0)),
                pl.BlockSpec((seq, width), lambda b, h: (base + b, h)),
                pl.BlockSpec((seq, width), lambda b, h: (base + b, groups + h)),
                pl.BlockSpec((seq, width), lambda b, h: (base + b, 2 * groups + h))]
    args = [lam_p, subln_g.reshape(subln_g.shape[0], 1, head_w), qkv, qkv, qkv]
    past = 0
    if cache is not None:
        ck, cv = cache
        past = ck.shape[2]
        spec = pl.BlockSpec((None, None, past, width), lambda b, h: (b, layer, 0, h))
        in_specs += [spec, spec]
        args += [ck, cv]
    return pl.pallas_call(
        functools.partial(_attn_body, lam_init=lam_init, past=past, tq=min(seq, 256)),
        grid=(n_batch, groups),
        in_specs=in_specs,
        out_specs=pl.BlockSpec((seq, width), lambda b, h: (b, h)),
        out_shape=jax.ShapeDtypeStruct((n_batch * seq, ATTN_HEADS * head_w), BF16),
        scratch_shapes=[pltpu.VMEM((past + seq, width), BF16)] * 2,
        compiler_params=_params(2),
        name="diff_attention",
    )(*args)


def _conv_silu(x, w, b):
    n = x.shape[0]
    row = lax.broadcasted_iota(jnp.int32, x.shape, 0)
    acc = x * w[SSM_CONV // 2:SSM_CONV // 2 + 1] + b
    for k in range(SSM_CONV):
        sh = k - SSM_CONV // 2
        if sh == 0:
            continue
        shifted = pltpu.roll(x, (-sh) % n, 0)
        valid = jnp.logical_and(row + sh >= 0, row + sh < n)
        acc = acc + jnp.where(valid, shifted, 0.0) * w[k:k + 1]
    return _silu(acc)


def _softplus(x):
    return jnp.maximum(x, 0.0) + jnp.log1p(jnp.exp(-jnp.abs(x)))


def _ssd_body(*refs, has_h0, emit_state, n_chunks):
    (xs_ref, bm_ref, cm_ref, dt_ref, wx_ref, bx_ref, wb_ref, bb_ref, wc_ref, bc_ref,
     dtb_ref, a_ref, d_ref) = refs[:13]
    refs = refs[13:]
    if has_h0:
        h0_ref, refs = refs[0], refs[1:]
    y_ref, refs = refs[0], refs[1:]
    if emit_state:
        st_ref, refs = refs[0], refs[1:]
    xs_s, b_s, c_s, dt_s, da_s, xt_s, dtt_s, dat_s, h_s = refs
    q = SSD_CHUNK
    p = SSM_HEAD_DIM
    dot = functools.partial(jnp.dot, preferred_element_type=F32)

    xs = _conv_silu(xs_ref[...], wx_ref[...], bx_ref[...])
    xs_s[...] = xs
    y_ref[...] = d_ref[...] * xs
    b_s[...] = _conv_silu(bm_ref[...], wb_ref[...], bb_ref[...])
    c_s[...] = _conv_silu(cm_ref[...], wc_ref[...], bc_ref[...])
    dt = _softplus(dt_ref[...] + dtb_ref[...])
    dt_s[...] = dt
    da_s[...] = dt * a_ref[...]

    def transposes(c, carry):
        rows = pl.ds(pl.multiple_of(c * q, q), q)
        xt_s[c] = xs_s[rows, :].T
        dtt_s[c] = dt_s[rows, :].T
        dat_s[c] = da_s[rows, :].T
        return carry

    lax.fori_loop(0, n_chunks, transposes, 0)

    row = lax.broadcasted_iota(jnp.int32, (q, q), 0)
    col = lax.broadcasted_iota(jnp.int32, (q, q), 1)
    lower, upper = row >= col, row <= col
    ones = [jnp.where(m, 1.0, 0.0).astype(BF16) for m in (lower, upper)]
    for direction in range(2):
        for r in range(SSM_HPG):
            if has_h0:
                h_s[direction, r] = h0_ref[direction, r]
            else:
                h_s[direction, r] = jnp.zeros((p, SSM_STATE), F32)

    def scan_chunk(c, direction):
        mask = lower if direction == 0 else upper
        end = q - 1 if direction == 0 else 0
        rows = pl.ds(pl.multiple_of(c * q, q), q)
        xc = xs_s[rows, :]
        bc = b_s[rows, :].astype(BF16)
        cc = c_s[rows, :].astype(BF16)
        dtc = dt_s[rows, :]
        dt_t = dtt_s[c]
        x_t = xt_s[c]
        d_hi, d_mid, d_lo = _split3(da_s[rows, :])
        tri = ones[direction]
        acum = dot(tri, d_hi) + (dot(tri, d_mid) + dot(tri, d_lo))
        t_hi, t_mid, t_lo = _split3(dat_s[c])
        tri_t = ones[1 - direction]
        acum_t = dot(t_hi, tri_t) + (dot(t_mid, tri_t) + dot(t_lo, tri_t))
        cb = _bdot_nt(cc, bc)
        a_end = acum[end:end + 1, :]
        ys = []
        for r in range(SSM_HPG):
            hl = direction * SSM_HPG + r
            a_col = acum[:, hl:hl + 1]
            a_row = acum_t[hl:hl + 1, :]
            decay = jnp.exp(jnp.where(mask, a_col - a_row, NEG_BIG))
            xdt = xc[:, r * p:(r + 1) * p] * dtc[:, hl:hl + 1]
            h = h_s[direction, r]
            y = _bdot(cb * decay, xdt) + jnp.exp(a_col) * _bdot_nt(cc, h)
            w = dt_t[hl:hl + 1, :] * jnp.exp(a_end[:, hl:hl + 1] - a_row)
            h_s[direction, r] = (jnp.exp(a_end[:, hl:hl + 1]) * h
                                 + _bdot(x_t[r * p:(r + 1) * p, :] * w, bc))
            ys.append(y)
        return rows, jnp.concatenate(ys, axis=1)

    def step(ci, carry):
        rows_f, y_f = scan_chunk(ci, 0)
        rows_b, y_b = scan_chunk(n_chunks - 1 - ci, 1)
        y_ref[rows_f, :] = y_ref[rows_f, :] + y_f
        y_ref[rows_b, :] = y_ref[rows_b, :] + y_b
        return carry

    lax.fori_loop(0, n_chunks, step, 0)
    if emit_state:
        for direction in range(2):
            for r in range(SSM_HPG):
                st_ref[direction, r] = h_s[direction, r]


def _ssd(zxbc, dtp, conv_w, conv_b, dt_bias_l, a_l, d_l, row0, n_batch, seq, h0=None, layer=0,
         emit_state=False):
    inner, gs, hp = SSM_INNER, SSM_GROUPS * SSM_STATE, SSM_HPG * SSM_HEAD_DIM
    base = row0 // seq
    g_of = lambda off, w: (lambda b, g: (0, off // w + g))
    row_blk = lambda off, w: (lambda b, g: (base + b, off // w + g))
    in_specs = [pl.BlockSpec((seq, hp), row_blk(inner, hp)),
                pl.BlockSpec((seq, SSM_STATE), row_blk(2 * inner, SSM_STATE)),
                pl.BlockSpec((seq, SSM_STATE), row_blk(2 * inner + gs, SSM_STATE)),
                pl.BlockSpec((seq, 128), row_blk(0, 128)),
                pl.BlockSpec((SSM_CONV, hp), g_of(0, hp)), pl.BlockSpec((1, hp), g_of(0, hp)),
                pl.BlockSpec((SSM_CONV, SSM_STATE), g_of(inner, SSM_STATE)),
                pl.BlockSpec((1, SSM_STATE), g_of(inner, SSM_STATE)),
                pl.BlockSpec((SSM_CONV, SSM_STATE), g_of(inner + gs, SSM_STATE)),
                pl.BlockSpec((1, SSM_STATE), g_of(inner + gs, SSM_STATE)),
                pl.BlockSpec((1, 128), g_of(0, 128)), pl.BlockSpec((1, 128), g_of(0, 128)),
                pl.BlockSpec((1, hp), g_of(0, hp))]
    args = [zxbc, zxbc, zxbc, dtp, conv_w, conv_b, conv_w, conv_b, conv_w, conv_b, dt_bias_l, a_l, d_l]
    if h0 is not None:
        in_specs.append(pl.BlockSpec((None, None, 2, SSM_HPG, SSM_HEAD_DIM, SSM_STATE),
                                     lambda b, g: (b, layer, 0, g, 0, 0)))
        args.append(h0)
    out_specs = [pl.BlockSpec((seq, hp), lambda b, g: (b, g))]
    out_shape = [jax.ShapeDtypeStruct((n_batch * seq, inner), F32)]
    if emit_state:
        out_specs.append(pl.BlockSpec((None, 2, SSM_HPG, SSM_HEAD_DIM, SSM_STATE),
                                      lambda b, g: (b, 0, g, 0, 0)))
        out_shape.append(jax.ShapeDtypeStruct(
            (n_batch, 2, SSM_GROUPS * SSM_HPG, SSM_HEAD_DIM, SSM_STATE), F32))
    return pl.pallas_call(
        functools.partial(_ssd_body, has_h0=h0 is not None, emit_state=emit_state,
                          n_chunks=seq // SSD_CHUNK),
        grid=(n_batch, SSM_GROUPS),
        in_specs=in_specs,
        out_specs=out_specs,
        out_shape=out_shape,
        scratch_shapes=[pltpu.VMEM((seq, hp), F32), pltpu.VMEM((seq, SSM_STATE), F32),
                        pltpu.VMEM((seq, SSM_STATE), F32), pltpu.VMEM((seq, 128), F32),
                        pltpu.VMEM((seq, 128), F32),
                        pltpu.VMEM((seq // SSD_CHUNK, hp, SSD_CHUNK), F32),
                        pltpu.VMEM((seq // SSD_CHUNK, 128, SSD_CHUNK), F32),
                        pltpu.VMEM((seq // SSD_CHUNK, 128, SSD_CHUNK), F32),
                        pltpu.VMEM((2, SSM_HPG, SSM_HEAD_DIM, SSM_STATE), F32)],
        compiler_params=_params(2),
        name="ssd_scan",
    )(*args)


def _ssm_out_body(y_ref, z_ref, g_ref, w_ref, o_ref, wbf):
    @pl.when(pl.program_id(0) == 0)
    def _():
        wbf[...] = w_ref[...].astype(BF16)

    y = y_ref[...] * _silu(z_ref[...])
    y = y * lax.rsqrt(jnp.mean(y * y, axis=-1, keepdims=True) + LN_EPS) * g_ref[...]
    o_ref[...] = jnp.dot(y.astype(BF16), wbf[...], preferred_element_type=F32)


def _ssm_out(y, zxbc, norm_g, w_out, layer):
    t, inner = y.shape
    d = w_out.shape[2]
    tm = SSM_OUT_ROW_TILE
    return pl.pallas_call(
        _ssm_out_body,
        grid=(t // tm,),
        in_specs=[pl.BlockSpec((tm, inner), lambda i: (i, 0)),
                  pl.BlockSpec((tm, inner), lambda i: (i, 0)),
                  pl.BlockSpec((None, 1, inner), lambda i: (layer, 0, 0)),
                  pl.BlockSpec((None, inner, d), lambda i: (layer, 0, 0))],
        out_specs=pl.BlockSpec((tm, d), lambda i: (i, 0)),
        out_shape=jax.ShapeDtypeStruct((t, d), F32),
        scratch_shapes=[pltpu.VMEM((inner, d), BF16)],
        compiler_params=_params(1),
        name="ssm_gate_norm_out",
    )(y, zxbc, norm_g.reshape(norm_g.shape[0], 1, inner), w_out)


def _gmlp_body(u_ref, v_ref, g_ref, b_ref, ws_ref, bs_ref, w_ref, o_ref, wbf, t_s, *, n_chunks):
    @pl.when(pl.program_id(0) == 0)
    def _():
        wbf[...] = w_ref[...].astype(BF16)

    v = v_ref[...]
    mu = jnp.mean(v, axis=-1, keepdims=True)
    vc = v - mu
    var = jnp.mean(vc * vc, axis=-1, keepdims=True)
    vn = (vc * lax.rsqrt(var + LN_EPS) * g_ref[...] + b_ref[...]).astype(BF16)
    q, gd = GMLP_CHUNK, GMLP_GROUP_DIM
    for g in range(GMLP_GROUPS):
        ws = ws_ref[g].astype(BF16)
        bias = bs_ref[:, g:g + 1]
        for c in range(n_chunks):
            sv = jnp.dot(ws, vn[c * q:(c + 1) * q, g * gd:(g + 1) * gd],
                         preferred_element_type=F32) + bias
            t_s[c * q:(c + 1) * q, g * gd:(g + 1) * gd] = (
                u_ref[c * q:(c + 1) * q, g * gd:(g + 1) * gd] * sv).astype(BF16)
    o_ref[...] = jnp.dot(t_s[...], wbf[...], preferred_element_type=F32)


def _gmlp_mix_out(uv, ln_g, ln_b, w_s, b_s, w_out, layer):
    t = uv.shape[0]
    half, d = GMLP_HALF, w_out.shape[2]
    tm = 512
    return pl.pallas_call(
        functools.partial(_gmlp_body, n_chunks=tm // GMLP_CHUNK),
        grid=(t // tm,),
        in_specs=[pl.BlockSpec((tm, half), lambda i: (i, 0)),
                  pl.BlockSpec((tm, half), lambda i: (i, 1)),
                  pl.BlockSpec((None, 1, half), lambda i: (layer, 0, 0)),
                  pl.BlockSpec((None, 1, half), lambda i: (layer, 0, 0)),
                  pl.BlockSpec((None, GMLP_GROUPS, GMLP_CHUNK, GMLP_CHUNK), lambda i: (layer, 0, 0, 0)),
                  pl.BlockSpec((None, GMLP_CHUNK, GMLP_GROUPS), lambda i: (layer, 0, 0)),
                  pl.BlockSpec((None, half, d), lambda i: (layer, 0, 0))],
        out_specs=pl.BlockSpec((tm, d), lambda i: (i, 0)),
        out_shape=jax.ShapeDtypeStruct((t, d), F32),
        scratch_shapes=[pltpu.VMEM((half, d), BF16), pltpu.VMEM((tm, half), BF16)],
        compiler_params=_params(1),
        name="gmlp_mix_out",
    )(uv, uv, ln_g.reshape(-1, 1, half), ln_b.reshape(-1, 1, half), w_s,
      jnp.swapaxes(b_s, 1, 2), w_out)


def _sorting_network(lo, hi):
    def merge(lo, hi, r):
        step = r * 2
        if step < hi - lo:
            yield from merge(lo, hi, step)
            yield from merge(lo + r, hi, step)
            yield from [(i, i + r) for i in range(lo + r, hi - r, step)]
        else:
            yield (lo, lo + r)

    if hi - lo >= 1:
        mid = lo + (hi - lo) // 2
        yield from _sorting_network(lo, mid)
        yield from _sorting_network(mid + 1, hi)
        yield from merge(lo, hi, 1)


def _top_rows(s, n, n_out):
    sub = 8
    n_slabs = s.shape[0] // sub
    v = [s[i * sub:(i + 1) * sub] for i in range(n_slabs)]
    for i, j in _sorting_network(0, n_slabs - 1):
        v[i], v[j] = jnp.maximum(v[i], v[j]), jnp.minimum(v[i], v[j])
    slot = lax.broadcasted_iota(jnp.int32, (n_out, s.shape[1]), 0)
    out = jnp.full((n_out, s.shape[1]), NEG_BIG, F32)
    for k in range(n):
        m = jnp.max(v[0], axis=0, keepdims=True)
        out = jnp.where(slot == k, m, out)
        hit = v[0] >= m
        depth = min(n_slabs, n - 1 - k)
        for i in range(depth):
            v[i] = jnp.where(hit, v[i + 1] if i + 1 < n_slabs else NEG_BIG, v[i])
    return out


def _peer_score_body(ht_ref, wq_ref, keys_ref, thr_ref, e1_ref, s2_ref, e2_ref, wbf):
    @pl.when(pl.program_id(0) == 0)
    def _():
        wbf[...] = wq_ref[...].astype(BF16)

    ht = ht_ref[...]
    kk, kh, sub = PEER_TOPK, PEER_HALF, PEER_ROWS_PER_BLOCK
    assert kk == 16 and sub == 8
    n_top = 3 * sub
    rank8 = lax.broadcasted_iota(jnp.int32, (sub, ht.shape[1]), 0)
    for h in range(PEER_HEADS):
        qh = jnp.dot(wbf[2 * kh * h:2 * kh * (h + 1), :], ht, preferred_element_type=F32)
        s1 = _dot_x3(keys_ref[h, 0], qh[:kh])
        s2 = _dot_x3(keys_ref[h, 1], qh[kh:])
        t1 = _top_rows(s1, kk + 1, n_top)
        t2 = _top_rows(s2, kk + 1, n_top)
        low, one = t1[0:sub], t2[0:sub]
        cand = jnp.concatenate([
            t1[0:1] + one,
            t1[0:1] + t2[sub:kk],
            t1[1:2] + one,
            jnp.where(rank8 < 5, t1[2:3] + one, NEG_BIG),
            jnp.where(rank8 < 4, t1[3:4] + one, low + t2[1:2]),
            jnp.where(rank8 >= 4, low + t2[0:1], NEG_BIG),
            t1[sub:kk] + t2[0:1],
            jnp.where(rank8 == 0, t1[kk:kk + 1] + t2[0:1],
                      jnp.where(rank8 == 1, t1[0:1] + t2[kk:kk + 1],
                                jnp.where(rank8 == 2, t1[4:5] + t2[2:3], NEG_BIG)))], axis=0)
        best = _top_rows(cand, kk + 1, n_top)
        z = jnp.sum(jnp.exp(best[:kk] - best[0:1]), axis=0, keepdims=True)
        thr = 0.5 * (best[kk - 1:kk] + best[kk:kk + 1]) - s1
        e1 = jnp.exp(s1 - t1[0:1])
        for rb in range(PEER_KEYS // sub):
            thr_ref[h, rb] = thr[rb * sub:(rb + 1) * sub]
            e1_ref[h, rb] = e1[rb * sub:(rb + 1) * sub]
        s2_ref[h] = s2
        e2_ref[h] = jnp.exp(s2 - t2[0:1]) * (0.5 / z)


def _peer_scores(ht, wq_t, sub_keys, layer):
    d, t = ht.shape
    tt = PEER_TOKEN_TILE
    nq = wq_t.shape[1]
    sub = PEER_ROWS_PER_BLOCK
    row_spec = pl.BlockSpec((PEER_HEADS, PEER_KEYS // sub, sub, tt), lambda i: (0, 0, 0, i))
    row_shape = jax.ShapeDtypeStruct((PEER_HEADS, PEER_KEYS // sub, sub, t), F32)
    tok_spec = pl.BlockSpec((PEER_HEADS, PEER_KEYS, tt), lambda i: (0, 0, i))
    tok_shape = jax.ShapeDtypeStruct((PEER_HEADS, PEER_KEYS, t), F32)
    return pl.pallas_call(
        _peer_score_body,
        grid=(t // tt,),
        in_specs=[pl.BlockSpec((d, tt), lambda i: (0, i)),
                  pl.BlockSpec((None, nq, d), lambda i: (layer, 0, 0)),
                  pl.BlockSpec((None, PEER_HEADS, 2, PEER_KEYS, PEER_HALF), lambda i: (layer, 0, 0, 0, 0))],
        out_specs=[row_spec, row_spec, tok_spec, tok_spec],
        out_shape=[row_shape, row_shape, tok_shape, tok_shape],
        scratch_shapes=[pltpu.VMEM((nq, d), BF16)],
        compiler_params=_params(1),
        name="peer_scores",
    )(ht, wq_t, sub_keys)


def _peer_tables_body(u_ref, v_ref, ub_ref, vt_ref):
    ub_ref[...] = u_ref[...].astype(BF16)
    vt_ref[...] = v_ref[...].T.astype(BF16)


def _peer_tables(u_tab, v_tab, layer):
    n, d = u_tab.shape[1:]
    eb = MM_COL_TILE
    return pl.pallas_call(
        _peer_tables_body,
        grid=(n // eb,),
        in_specs=[pl.BlockSpec((None, eb, d), lambda e: (layer, e, 0)),
                  pl.BlockSpec((None, eb, d), lambda e: (layer, e, 0))],
        out_specs=[pl.BlockSpec((eb, d), lambda e: (e, 0)), pl.BlockSpec((d, eb), lambda e: (0, e))],
        out_shape=[jax.ShapeDtypeStruct((n, d), BF16), jax.ShapeDtypeStruct((d, n), BF16)],
        compiler_params=_params(1),
        name="peer_tables",
    )(u_tab, v_tab)


def _peer_mix_body(ht_ref, thr_ref, e1_ref, s2_ref, e2_ref, u_ref, vt_ref, o_ref, act_s, coef_s):
    @pl.when(pl.program_id(1) == 0)
    def _():
        o_ref[...] = jnp.zeros(o_ref.shape, F32)

    lanes, kq = 128, 32
    n_q = PEER_KEYS // kq
    act_s[...] = jnp.dot(u_ref[...], ht_ref[...], preferred_element_type=F32)
    c0 = math.sqrt(2.0 / math.pi)

    def gate_chunk(ci, carry):
        part, q = ci // n_q, ci % n_q
        c = pl.ds(pl.multiple_of(part * lanes, lanes), lanes)
        keys = pl.ds(pl.multiple_of(q * kq, kq), kq)
        gates = [jnp.zeros((kq, lanes), F32)] * PEER_ROWS_PER_BLOCK
        for h in range(PEER_HEADS):
            s2 = s2_ref[h, keys, c]
            e2 = e2_ref[h, keys, c]
            for r in range(PEER_ROWS_PER_BLOCK):
                chosen = s2 >= thr_ref[h, 0, r:r + 1, c]
                gates[r] = gates[r] + jnp.where(chosen, e2, 0.0) * e1_ref[h, 0, r:r + 1, c]
        for r in range(PEER_ROWS_PER_BLOCK):
            rows = pl.ds(pl.multiple_of(r * PEER_KEYS + q * kq, kq), kq)
            a = act_s[rows, c]
            twice_gelu = a * (1.0 + jnp.tanh(a * (c0 + (c0 * 0.044715) * (a * a))))
            coef_s[rows, c] = (twice_gelu * gates[r]).astype(BF16)
        return carry

    lax.fori_loop(0, (o_ref.shape[1] // lanes) * n_q, gate_chunk, 0)
    o_ref[...] = o_ref[...] + jnp.dot(vt_ref[...], coef_s[...], preferred_element_type=F32)


def _peer_mix(ht, thr, e1, s2, e2, u_bf, vt_bf):
    d, t = ht.shape
    ts, rb = PEER_SUPER_TILE, PEER_ROWS_PER_BLOCK
    eb = rb * PEER_KEYS
    n_blocks = u_bf.shape[0] // eb
    row_spec = pl.BlockSpec((PEER_HEADS, 1, rb, ts), lambda s, e: (0, e, 0, s))
    tok_spec = pl.BlockSpec((PEER_HEADS, PEER_KEYS, ts), lambda s, e: (0, 0, s))
    return pl.pallas_call(
        _peer_mix_body,
        grid=(t // ts, n_blocks),
        in_specs=[pl.BlockSpec((d, ts), lambda s, e: (0, s)), row_spec, row_spec, tok_spec, tok_spec,
                  pl.BlockSpec((eb, d), lambda s, e: (e, 0)), pl.BlockSpec((d, eb), lambda s, e: (0, e))],
        out_specs=pl.BlockSpec((d, ts), lambda s, e: (0, s)),
        out_shape=jax.ShapeDtypeStruct((d, t), F32),
        scratch_shapes=[pltpu.VMEM((eb, ts), F32), pltpu.VMEM((eb, ts), BF16)],
        compiler_params=_params(2),
        name="peer_mix",
    )(ht, thr, e1, s2, e2, u_bf, vt_bf)


def kernel(x_prompt, x_sample, cache_k, cache_v, state_ssm, c, c_ctx, w_mod, b_mod, ln_g, ln_b, attn_w_qkv, attn_w_o, attn_lambda, attn_subln_g, ssm_w_in, ssm_conv_w, ssm_conv_b, ssm_dt_bias, ssm_a_log, ssm_d, ssm_norm_g, ssm_w_out, gmlp_w_in, gmlp_b_in, gmlp_ln_g, gmlp_ln_b, gmlp_w_s, gmlp_b_s, gmlp_w_out, peer_w_q, peer_sub_keys, peer_u, peer_v):
    bc, lc, d = x_prompt.shape
    bl, ll, _ = x_sample.shape
    depth = w_mod.shape[0]
    t_ctx, t_lat = bc * lc, bl * ll
    alpha = (2 * depth) ** 0.25
    assert 1 + bl <= N_MOD_GROUPS and t_ctx % ll == 0

    x = jnp.concatenate([x_prompt.reshape(t_ctx, d), x_sample.reshape(t_lat, d)], axis=0)
    cond = jnp.zeros((N_MOD_GROUPS, d), F32).at[0].set(c_ctx).at[1:1 + bl].set(c)
    mods = _modulation(cond, w_mod, b_mod).reshape(depth, N_MOD_GROUPS, 6, d)
    rope = _rope_tables(ll, attn_w_qkv.shape[2] // 3)
    past = cache_k.shape[2]
    ck = cache_k.reshape(bl, cache_k.shape[1], past, -1)
    cv = cache_v.reshape(bl, cache_v.shape[1], past, -1)
    wq_t = jnp.swapaxes(peer_w_q, 1, 2)

    new_k, new_v, new_s = [], [], []
    h = _modulate(x, mods[0], t_ctx, ll, shift=0, scale=1)
    for i in range(depth):
        kind, j = i % 3, i // 3
        if kind == 0:
            lam_init = 0.8 - 0.6 * math.exp(-0.3 * i)
            qkv, k_ctx, v_ctx = _qkv_rope(h, attn_w_qkv, j, rope, t_ctx, ll)
            o_ctx = _diff_attention(qkv, attn_lambda, attn_subln_g, j, lam_init, 0, bc, lc)
            o_lat = _diff_attention(qkv, attn_lambda, attn_subln_g, j, lam_init, t_ctx, bl, ll,
                                    cache=(ck, cv))
            o = _linear(jnp.concatenate([o_ctx, o_lat], axis=0).astype(BF16), attn_w_o, j, d)
            new_k.append(k_ctx.reshape(bc, lc, ATTN_HEADS, 2, ATTN_HEAD_DIM))
            new_v.append(v_ctx.reshape(bc, lc, ATTN_HEADS, 2 * ATTN_HEAD_DIM))
        elif kind == 1:
            n_main = SSM_INNER + SSM_INNER + 2 * SSM_GROUPS * SSM_STATE
            zxbc = _linear(h, ssm_w_in, j, n_main)
            def regroup(a):
                a = a.reshape(a.shape[:-1] + (2, SSM_GROUPS, SSM_HPG))
                a = jnp.moveaxis(a, -3, -2).reshape(a.shape[:-3] + (SSM_GROUPS, 2 * SSM_HPG))
                pad = [(0, 0)] * (a.ndim - 1) + [(0, 128 - 2 * SSM_HPG)]
                return jnp.pad(a, pad).reshape(a.shape[:-2] + (SSM_GROUPS * 128,))
            w_dt = regroup(ssm_w_in[j][:, n_main:])[None]
            dtp = _linear(h, w_dt, 0, SSM_GROUPS * 128)
            dtb = regroup(ssm_dt_bias[j].reshape(1, -1))
            a_l = regroup(-jnp.exp(ssm_a_log[j]).reshape(1, -1))
            d_l = jnp.repeat(ssm_d[j], SSM_HEAD_DIM).reshape(1, SSM_INNER)
            cw, cb = ssm_conv_w[j], ssm_conv_b[j].reshape(1, -1)
            y_ctx, st = _ssd(zxbc, dtp, cw, cb, dtb, a_l, d_l, 0, bc, lc, emit_state=True)
            (y_lat,) = _ssd(zxbc, dtp, cw, cb, dtb, a_l, d_l, t_ctx, bl, ll, h0=state_ssm, layer=j)
            o = _ssm_out(jnp.concatenate([y_ctx, y_lat], axis=0), zxbc, ssm_norm_g, ssm_w_out, j)
            new_s.append(st)
        else:
            uv = _linear(h, gmlp_w_in, j, 2 * GMLP_HALF, bias=gmlp_b_in[j], act="gelu")
            o = _gmlp_mix_out(uv, gmlp_ln_g, gmlp_ln_b, gmlp_w_s, gmlp_b_s, gmlp_w_out, j)
        x, ht = _res_ln(x, o, mods[i], mods[i], ln_g[i, 0], ln_b[i, 0], t_ctx, ll, alpha=alpha,
                        gate=2, nshift=3, nscale=4, emit_ht=True)
        thr, e1, s2, e2 = _peer_scores(ht, wq_t, peer_sub_keys, i)
        u_bf, vt_bf = _peer_tables(peer_u, peer_v, i)
        f_t = _peer_mix(ht, thr, e1, s2, e2, u_bf, vt_bf)
        if i + 1 < depth:
            x, h = _res_ln(x, f_t, mods[i], mods[i + 1], ln_g[i, 1], ln_b[i, 1], t_ctx, ll,
                           alpha=alpha, gate=5, nshift=0, nscale=1, o_transposed=True, emit_h=True)
        else:
            (x,) = _res_ln(x, f_t, mods[i], mods[i], ln_g[i, 1], ln_b[i, 1], t_ctx, ll,
                           alpha=alpha, gate=5, o_transposed=True)
    return (x[:t_ctx].reshape(bc, lc, d), x[t_ctx:].reshape(bl, ll, d),
            jnp.stack(new_k, axis=1), jnp.stack(new_v, axis=1), jnp.stack(new_s, axis=1))
```

```python
import functools
import math

import jax
import jax.numpy as jnp
from jax import lax
from jax.experimental import pallas as pl
from jax.experimental.pallas import tpu as pltpu

F32 = jnp.float32
BF16 = jnp.bfloat16

D_MODEL = 1024
LN_EPS = 1e-5
GRID_W = 64
ATTN_HEADS = 8
ATTN_HEAD_DIM = 64
ROPE_AXIS_DIM = ATTN_HEAD_DIM // 2
ROPE_BASE = 10000.0
SSM_INNER = 2 * D_MODEL
SSM_HEAD_DIM = 64
SSM_GROUPS = 8
SSM_HPG = 4
SSM_STATE = 128
SSM_CONV = 5
SSD_CHUNK = 128
GMLP_HALF = 2 * D_MODEL
GMLP_GROUPS = 8
GMLP_GROUP_DIM = GMLP_HALF // GMLP_GROUPS
GMLP_CHUNK = 128
PEER_HEADS = 8
PEER_KEYS = 128
PEER_HALF = 128
PEER_TOPK = 16
N_MOD_GROUPS = 8
NEG_BIG = -1e30

VMEM_LIMIT = 56 * 1024 * 1024
ROW_TILE = 512
SSM_OUT_ROW_TILE = 256
MM_ROW_TILE = 1024
MM_COL_TILE = 1024
PEER_TOKEN_TILE = 256
PEER_SUPER_TILE = 1024
PEER_ROWS_PER_BLOCK = 8


def _params(n_axes):
    return pltpu.CompilerParams(dimension_semantics=("arbitrary",) * n_axes,
                                vmem_limit_bytes=VMEM_LIMIT)


def _bdot(a, b):
    return jnp.dot(a.astype(BF16), b.astype(BF16), preferred_element_type=F32)


def _bdot_nt(a, b):
    return lax.dot_general(a.astype(BF16), b.astype(BF16), (((1,), (1,)), ((), ())),
                           preferred_element_type=F32)


def _split3(a):
    hi = a.astype(BF16)
    r = a - hi.astype(F32)
    mid = r.astype(BF16)
    lo = (r - mid.astype(F32)).astype(BF16)
    return hi, mid, lo


def _dot_x3(a, b):
    ah, am, _ = _split3(a)
    bh, bm, _ = _split3(b)
    d = functools.partial(jnp.dot, preferred_element_type=F32)
    return d(ah, bh) + (d(am, bh) + d(ah, bm))


def _silu(x):
    return x * jax.nn.sigmoid(x)


def _gelu(x):
    return 0.5 * x * (1.0 + jnp.tanh(math.sqrt(2.0 / math.pi) * (x + 0.044715 * (x * x * x))))


def _group_index(i, tm, t_ctx, l_lat):
    start = i * tm
    return jnp.where(start < t_ctx, 0, 1 + (start - t_ctx) // l_lat)


def _mod_body(c_ref, w_ref, b_ref, o_ref):
    o_ref[0] = _dot_x3(_silu(c_ref[...]), w_ref[0]) + b_ref[0]


def _modulation(cond, w_mod, b_mod):
    depth, d, n = w_mod.shape
    tn = 1536
    return pl.pallas_call(
        _mod_body,
        grid=(depth, n // tn),
        in_specs=[pl.BlockSpec((N_MOD_GROUPS, d), lambda l, j: (0, 0)),
                  pl.BlockSpec((1, d, tn), lambda l, j: (l, 0, j)),
                  pl.BlockSpec((1, 1, tn), lambda l, j: (l, 0, j))],
        out_specs=pl.BlockSpec((1, N_MOD_GROUPS, tn), lambda l, j: (l, 0, j)),
        out_shape=jax.ShapeDtypeStruct((depth, N_MOD_GROUPS, n), F32),
        compiler_params=_params(2),
        name="modulation",
    )(cond, w_mod, b_mod.reshape(depth, 1, n))


def _modulate_body(x_ref, m_ref, h_ref, *, shift, scale):
    m = m_ref[0]
    h_ref[...] = (x_ref[...] * (1.0 + m[scale:scale + 1]) + m[shift:shift + 1]).astype(BF16)


def _modulate(x, mod, t_ctx, l_lat, shift, scale):
    t, d = x.shape
    tm = ROW_TILE
    return pl.pallas_call(
        functools.partial(_modulate_body, shift=shift, scale=scale),
        grid=(t // tm,),
        in_specs=[pl.BlockSpec((tm, d), lambda i: (i, 0)),
                  pl.BlockSpec((1, 6, d), lambda i: (_group_index(i, tm, t_ctx, l_lat), 0, 0))],
        out_specs=pl.BlockSpec((tm, d), lambda i: (i, 0)),
        out_shape=jax.ShapeDtypeStruct((t, d), BF16),
        compiler_params=_params(1),
        name="modulate",
    )(x, mod)


def _res_ln_body(x_ref, o_ref, m_ref, mn_ref, g_ref, b_ref, *outs,
                 alpha, gate, nshift, nscale, o_transposed, emit_h, emit_ht):
    o = o_ref[...]
    if o_transposed:
        o = o.T
    m = m_ref[0]
    y = alpha * x_ref[...] + m[gate:gate + 1] * o
    mu = jnp.mean(y, axis=-1, keepdims=True)
    yc = y - mu
    var = jnp.mean(yc * yc, axis=-1, keepdims=True)
    xn = yc * lax.rsqrt(var + LN_EPS) * g_ref[...] + b_ref[...]
    outs[0][...] = xn
    if emit_h or emit_ht:
        mn = mn_ref[0]
        h = xn * (1.0 + mn[nscale:nscale + 1]) + mn[nshift:nshift + 1]
        k = 1
        if emit_h:
            outs[k][...] = h.astype(BF16)
            k += 1
        if emit_ht:
            outs[k][...] = h.T.astype(BF16)


def _res_ln(x, o, mod, mod_next, ln_g, ln_b, t_ctx, l_lat, *, alpha, gate, nshift=0, nscale=1,
            o_transposed=False, emit_h=False, emit_ht=False):
    t, d = x.shape
    tm = ROW_TILE
    grp = lambda i: (_group_index(i, tm, t_ctx, l_lat), 0, 0)
    o_spec = (pl.BlockSpec((d, tm), lambda i: (0, i)) if o_transposed
              else pl.BlockSpec((tm, d), lambda i: (i, 0)))
    out_specs = [pl.BlockSpec((tm, d), lambda i: (i, 0))]
    out_shape = [jax.ShapeDtypeStruct((t, d), F32)]
    if emit_h:
        out_specs.append(pl.BlockSpec((tm, d), lambda i: (i, 0)))
        out_shape.append(jax.ShapeDtypeStruct((t, d), BF16))
    if emit_ht:
        out_specs.append(pl.BlockSpec((d, tm), lambda i: (0, i)))
        out_shape.append(jax.ShapeDtypeStruct((d, t), BF16))
    return pl.pallas_call(
        functools.partial(_res_ln_body, alpha=alpha, gate=gate, nshift=nshift, nscale=nscale,
                          o_transposed=o_transposed, emit_h=emit_h, emit_ht=emit_ht),
        grid=(t // tm,),
        in_specs=[pl.BlockSpec((tm, d), lambda i: (i, 0)), o_spec,
                  pl.BlockSpec((1, 6, d), grp), pl.BlockSpec((1, 6, d), grp),
                  pl.BlockSpec((1, d), lambda i: (0, 0)), pl.BlockSpec((1, d), lambda i: (0, 0))],
        out_specs=out_specs,
        out_shape=out_shape,
        compiler_params=_params(1),
        name="residual_layernorm",
    )(x, o, mod, mod_next, ln_g.reshape(1, d), ln_b.reshape(1, d))


def _two_part_specs(parts, tm, grid_rank):
    first, second = parts
    n_first = first.shape[0] // tm
    row = lambda idx: idx[-1] if grid_rank == 2 else idx[0]
    return ([pl.BlockSpec((tm, first.shape[1]), lambda *idx: (jnp.minimum(row(idx), n_first - 1), 0)),
             pl.BlockSpec((tm, second.shape[1]), lambda *idx: (jnp.maximum(row(idx) - n_first, 0), 0))],
            n_first)


def _linear_body(*refs, has_bias, act, first_tiles):
    n_src = 1 if first_tiles is None else 2
    srcs, w_ref, refs = refs[:n_src], refs[n_src], refs[n_src + 1:]
    if has_bias:
        b_ref, o_ref, wbf = refs
    else:
        o_ref, wbf = refs
    i = pl.program_id(1)

    @pl.when(i == 0)
    def _():
        wbf[...] = w_ref[...].astype(BF16)

    def project(h_ref):
        y = jnp.dot(h_ref[...], wbf[...], preferred_element_type=F32)
        if has_bias:
            y = y + b_ref[...]
        if act == "gelu":
            y = _gelu(y)
        o_ref[...] = y

    if first_tiles is None:
        project(srcs[0])
    else:
        pl.when(i < first_tiles)(lambda: project(srcs[0]))
        pl.when(i >= first_tiles)(lambda: project(srcs[1]))


def _linear(h, w_stack, layer, n_out, bias=None, act=None):
    tm, tn = MM_ROW_TILE, min(MM_COL_TILE, n_out)
    if isinstance(h, tuple):
        t, k = h[0].shape[0] + h[1].shape[0], h[0].shape[1]
        in_specs, first_tiles = _two_part_specs(h, tm, 2)
        args = list(h)
    else:
        t, k = h.shape
        in_specs, first_tiles = [pl.BlockSpec((tm, k), lambda n, i: (i, 0))], None
        args = [h]
    in_specs.append(pl.BlockSpec((None, k, tn), lambda n, i: (layer, 0, n)))
    args.append(w_stack)
    if bias is not None:
        in_specs.append(pl.BlockSpec((1, tn), lambda n, i: (0, n)))
        args.append(bias.reshape(1, n_out))
    return pl.pallas_call(
        functools.partial(_linear_body, has_bias=bias is not None, act=act, first_tiles=first_tiles),
        grid=(n_out // tn, t // tm),
        in_specs=in_specs,
        out_specs=pl.BlockSpec((tm, tn), lambda n, i: (i, n)),
        out_shape=jax.ShapeDtypeStruct((t, n_out), F32),
        scratch_shapes=[pltpu.VMEM((k, tn), BF16)],
        compiler_params=_params(2),
        name="linear",
    )(*args)


def _qkv_body(h_ref, w_ref, cos_ref, sa_ref, sb_ref, o_ref, kc_ref, vc_ref, wbf, *, n_ctx_tiles):
    n, i = pl.program_id(0), pl.program_id(1)

    @pl.when(i == 0)
    def _():
        wbf[...] = w_ref[...].astype(BF16)

    y = jnp.dot(h_ref[...], wbf[...], preferred_element_type=F32)
    rope = jnp.logical_and(n < 2, i >= n_ctx_tiles)
    half = ROPE_AXIS_DIM // 2

    @pl.when(jnp.logical_and(n == 1, i < n_ctx_tiles))
    def _():
        kc_ref[...] = y

    @pl.when(jnp.logical_and(n == 2, i < n_ctx_tiles))
    def _():
        vc_ref[...] = y

    @pl.when(rope)
    def _():
        w = cos_ref.shape[1]
        for c0 in range(0, y.shape[1], w):
            yg = y[:, c0:c0 + w]
            up = pltpu.roll(yg, w - half, 1)
            dn = pltpu.roll(yg, half, 1)
            o_ref[:, c0:c0 + w] = yg * cos_ref[...] + up * sa_ref[...] + dn * sb_ref[...]

    @pl.when(jnp.logical_not(rope))
    def _():
        o_ref[...] = y


def _qkv_rope(h, w_stack, layer, tables, t_ctx, l_lat):
    t, k = h.shape
    n_out = w_stack.shape[2]
    tm, tn = MM_ROW_TILE, MM_COL_TILE
    assert n_out == 3 * tn
    n_ctx_tiles = t_ctx // tm
    pos = lambda n, i: (jnp.maximum(i - n_ctx_tiles, 0) % (l_lat // tm), 0)
    last = n_ctx_tiles - 1
    k_rows = lambda n, i: (jnp.where(n == 1, jnp.minimum(i, last), jnp.where(n < 1, 0, last)), 0)
    v_rows = lambda n, i: (jnp.where(n == 2, jnp.minimum(i, last), 0), 0)
    ctx_shape = jax.ShapeDtypeStruct((t_ctx, tn), F32)
    return pl.pallas_call(
        functools.partial(_qkv_body, n_ctx_tiles=n_ctx_tiles),
        grid=(n_out // tn, t // tm),
        in_specs=[pl.BlockSpec((tm, k), lambda n, i: (i, 0)),
                  pl.BlockSpec((None, k, tn), lambda n, i: (layer, 0, n)),
                  *[pl.BlockSpec((tm, tab.shape[1]), pos) for tab in tables]],
        out_specs=[pl.BlockSpec((tm, tn), lambda n, i: (i, n)),
                   pl.BlockSpec((tm, tn), k_rows), pl.BlockSpec((tm, tn), v_rows)],
        out_shape=[jax.ShapeDtypeStruct((t, n_out), F32), ctx_shape, ctx_shape],
        scratch_shapes=[pltpu.VMEM((k, tn), BF16)],
        compiler_params=_params(2),
        name="qkv_rope",
    )(h, w_stack, *tables)


def _rope_tables(l_lat, width):
    rows = l_lat // GRID_W
    row_pos = jnp.repeat(jnp.arange(rows, dtype=F32), GRID_W)
    col_pos = jnp.tile(jnp.arange(GRID_W, dtype=F32), rows)
    inv_freq = ROPE_BASE ** (-jnp.arange(0, ROPE_AXIS_DIM, 2, dtype=F32) / ROPE_AXIS_DIM)

    def tab(p):
        ang = p[:, None] * inv_freq[None, :]
        ang = jnp.concatenate([ang, ang], -1)
        return jnp.cos(ang), jnp.sin(ang)

    cr, sr = tab(row_pos)
    cc, sc = tab(col_pos)
    cos = jnp.concatenate([cr, cc], -1)
    sin = jnp.concatenate([sr, sc], -1)
    first_half = (jnp.arange(ATTN_HEAD_DIM) % ROPE_AXIS_DIM) < (ROPE_AXIS_DIM // 2)
    sin_a = jnp.where(first_half, -sin, 0.0)
    sin_b = jnp.where(first_half, 0.0, sin)
    rep = width // ATTN_HEAD_DIM
    return tuple(jnp.tile(a, (1, rep)) for a in (cos, sin_a, sin_b))


def _attn_body(lam_ref, g_ref, q_ref, k_ref, v_ref, *refs, lam_init, past, tq):
    if past:
        kc_ref, vc_ref, o_ref, k_s, v_s = refs
        k_s[:past] = kc_ref[...].astype(BF16)
        v_s[:past] = vc_ref[...].astype(BF16)
    else:
        o_ref, k_s, v_s = refs
    k_s[past:] = k_ref[...].astype(BF16)
    v_s[past:] = v_ref[...].astype(BF16)
    lp = lam_ref[...]
    lam = (jnp.exp(jnp.sum(lp[0:1] * lp[1:2], axis=-1, keepdims=True))
           - jnp.exp(jnp.sum(lp[2:3] * lp[3:4], axis=-1, keepdims=True)) + lam_init)
    dh = ATTN_HEAD_DIM
    scale = dh ** -0.5

    def q_block(i, carry):
        rows = pl.ds(pl.multiple_of(i * tq, tq), tq)
        for c0 in range(0, q_ref.shape[1], 2 * dh):
            q = (q_ref[rows, c0:c0 + 2 * dh] * scale).astype(BF16)

            def softmax_map(m, q=q, c0=c0):
                s = _bdot_nt(q[:, m * dh:(m + 1) * dh], k_s[:, c0 + m * dh:c0 + (m + 1) * dh])
                e = jnp.exp(s - jnp.max(s, axis=-1, keepdims=True))
                return e / jnp.sum(e, axis=-1, keepdims=True)

            a = softmax_map(0) - lam * softmax_map(1)
            o = _bdot(a, v_s[:, c0:c0 + 2 * dh])
            o = o * lax.rsqrt(jnp.mean(o * o, axis=-1, keepdims=True) + LN_EPS) * g_ref[...]
            o_ref[rows, c0:c0 + 2 * dh] = (o * (1.0 - lam_init)).astype(o_ref.dtype)
        return carry

    lax.fori_loop(0, q_ref.shape[0] // tq, q_block, 0)


def _diff_attention(qkv, lam_p, subln_g, layer, lam_init, row0, n_batch, seq, cache=None):
    head_w = 2 * ATTN_HEAD_DIM
    hps = 1 if cache is not None else 4
    width, groups = hps * head_w, ATTN_HEADS // hps
    base = row0 // seq
    in_specs = [pl.BlockSpec((None, 4, ATTN_HEAD_DIM), lambda b, h: (layer, 0, 0)),
                pl.BlockSpec((None, 1, head_w), lambda b, h: (layer, 0, 0)),
                pl.BlockSpec((seq, width), lambda b, h: (base + b, h)),
                pl.BlockSpec((seq, width), lambda b, h: (base + b, groups + h)),
                pl.BlockSpec((seq, width), lambda b, h: (base + b, 2 * groups + h))]
    args = [lam_p, subln_g.reshape(subln_g.shape[0], 1, head_w), qkv, qkv, qkv]
    past = 0
    if cache is not None:
        ck, cv = cache
        past = ck.shape[2]
        spec = pl.BlockSpec((None, None, past, width), lambda b, h: (b, layer, 0, h))
        in_specs += [spec, spec]
        args += [ck, cv]
    return pl.pallas_call(
        functools.partial(_attn_body, lam_init=lam_init, past=past, tq=min(seq, 256)),
        grid=(n_batch, groups),
        in_specs=in_specs,
        out_specs=pl.BlockSpec((seq, width), lambda b, h: (b, h)),
        out_shape=jax.ShapeDtypeStruct((n_batch * seq, ATTN_HEADS * head_w), BF16),
        scratch_shapes=[pltpu.VMEM((past + seq, width), BF16)] * 2,
        compiler_params=_params(2),
        name="diff_attention",
    )(*args)


def _conv_silu(x, w, b):
    n = x.shape[0]
    row = lax.broadcasted_iota(jnp.int32, x.shape, 0)
    acc = x * w[SSM_CONV // 2:SSM_CONV // 2 + 1] + b
    for k in range(SSM_CONV):
        sh = k - SSM_CONV // 2
        if sh == 0:
            continue
        shifted = pltpu.roll(x, (-sh) % n, 0)
        valid = jnp.logical_and(row + sh >= 0, row + sh < n)
        acc = acc + jnp.where(valid, shifted, 0.0) * w[k:k + 1]
    return _silu(acc)


def _softplus(x):
    return jnp.maximum(x, 0.0) + jnp.log1p(jnp.exp(-jnp.abs(x)))


def _ssd_body(*refs, has_h0, emit_state, n_chunks):
    (xs_ref, bm_ref, cm_ref, dt_ref, wx_ref, bx_ref, wb_ref, bb_ref, wc_ref, bc_ref,
     dtb_ref, a_ref, d_ref) = refs[:13]
    refs = refs[13:]
    if has_h0:
        h0_ref, refs = refs[0], refs[1:]
    y_ref, refs = refs[0], refs[1:]
    if emit_state:
        st_ref, refs = refs[0], refs[1:]
    xs_s, b_s, c_s, dt_s, da_s, xt_s, dtt_s, dat_s, h_s = refs
    q = SSD_CHUNK
    p = SSM_HEAD_DIM
    dot = functools.partial(jnp.dot, preferred_element_type=F32)

    xs = _conv_silu(xs_ref[...], wx_ref[...], bx_ref[...])
    xs_s[...] = xs
    y_ref[...] = d_ref[...] * xs
    b_s[...] = _conv_silu(bm_ref[...], wb_ref[...], bb_ref[...])
    c_s[...] = _conv_silu(cm_ref[...], wc_ref[...], bc_ref[...])
    dt = _softplus(dt_ref[...] + dtb_ref[...])
    dt_s[...] = dt
    da_s[...] = dt * a_ref[...]

    def transposes(c, carry):
        rows = pl.ds(pl.multiple_of(c * q, q), q)
        xt_s[c] = xs_s[rows, :].T
        dtt_s[c] = dt_s[rows, :].T
        dat_s[c] = da_s[rows, :].T
        return carry

    lax.fori_loop(0, n_chunks, transposes, 0)

    row = lax.broadcasted_iota(jnp.int32, (q, q), 0)
    col = lax.broadcasted_iota(jnp.int32, (q, q), 1)
    lower, upper = row >= col, row <= col
    ones = [jnp.where(m, 1.0, 0.0).astype(BF16) for m in (lower, upper)]
    for direction in range(2):
        for r in range(SSM_HPG):
            if has_h0:
                h_s[direction, r] = h0_ref[direction, r]
            else:
                h_s[direction, r] = jnp.zeros((p, SSM_STATE), F32)

    def scan_chunk(c, direction):
        mask = lower if direction == 0 else upper
        end = q - 1 if direction == 0 else 0
        rows = pl.ds(pl.multiple_of(c * q, q), q)
        xc = xs_s[rows, :]
        bc = b_s[rows, :].astype(BF16)
        cc = c_s[rows, :].astype(BF16)
        dtc = dt_s[rows, :]
        dt_t = dtt_s[c]
        x_t = xt_s[c]
        d_hi, d_mid, d_lo = _split3(da_s[rows, :])
        tri = ones[direction]
        acum = dot(tri, d_hi) + (dot(tri, d_mid) + dot(tri, d_lo))
        t_hi, t_mid, t_lo = _split3(dat_s[c])
        tri_t = ones[1 - direction]
        acum_t = dot(t_hi, tri_t) + (dot(t_mid, tri_t) + dot(t_lo, tri_t))
        cb = _bdot_nt(cc, bc)
        a_end = acum[end:end + 1, :]
        ys = []
        for r in range(SSM_HPG):
            hl = direction * SSM_HPG + r
            a_col = acum[:, hl:hl + 1]
            a_row = acum_t[hl:hl + 1, :]
            decay = jnp.exp(jnp.where(mask, a_col - a_row, NEG_BIG))
            xdt = xc[:, r * p:(r + 1) * p] * dtc[:, hl:hl + 1]
            h = h_s[direction, r]
            y = _bdot(cb * decay, xdt) + jnp.exp(a_col) * _bdot_nt(cc, h)
            w = dt_t[hl:hl + 1, :] * jnp.exp(a_end[:, hl:hl + 1] - a_row)
            h_s[direction, r] = (jnp.exp(a_end[:, hl:hl + 1]) * h
                                 + _bdot(x_t[r * p:(r + 1) * p, :] * w, bc))
            ys.append(y)
        return rows, jnp.concatenate(ys, axis=1)

    def step(ci, carry):
        rows_f, y_f = scan_chunk(ci, 0)
        rows_b, y_b = scan_chunk(n_chunks - 1 - ci, 1)
        y_ref[rows_f, :] = y_ref[rows_f, :] + y_f
        y_ref[rows_b, :] = y_ref[rows_b, :] + y_b
        return carry

    lax.fori_loop(0, n_chunks, step, 0)
    if emit_state:
        for direction in range(2):
            for r in range(SSM_HPG):
                st_ref[direction, r] = h_s[direction, r]


def _ssd(zxbc, dtp, conv_w, conv_b, dt_bias_l, a_l, d_l, row0, n_batch, seq, h0=None, layer=0,
         emit_state=False):
    inner, gs, hp = SSM_INNER, SSM_GROUPS * SSM_STATE, SSM_HPG * SSM_HEAD_DIM
    base = row0 // seq
    g_of = lambda off, w: (lambda b, g: (0, off // w + g))
    row_blk = lambda off, w: (lambda b, g: (base + b, off // w + g))
    in_specs = [pl.BlockSpec((seq, hp), row_blk(inner, hp)),
                pl.BlockSpec((seq, SSM_STATE), row_blk(2 * inner, SSM_STATE)),
                pl.BlockSpec((seq, SSM_STATE), row_blk(2 * inner + gs, SSM_STATE)),
                pl.BlockSpec((seq, 128), row_blk(0, 128)),
                pl.BlockSpec((SSM_CONV, hp), g_of(0, hp)), pl.BlockSpec((1, hp), g_of(0, hp)),
                pl.BlockSpec((SSM_CONV, SSM_STATE), g_of(inner, SSM_STATE)),
                pl.BlockSpec((1, SSM_STATE), g_of(inner, SSM_STATE)),
                pl.BlockSpec((SSM_CONV, SSM_STATE), g_of(inner + gs, SSM_STATE)),
                pl.BlockSpec((1, SSM_STATE), g_of(inner + gs, SSM_STATE)),
                pl.BlockSpec((1, 128), g_of(0, 128)), pl.BlockSpec((1, 128), g_of(0, 128)),
                pl.BlockSpec((1, hp), g_of(0, hp))]
    args = [zxbc, zxbc, zxbc, dtp, conv_w, conv_b, conv_w, conv_b, conv_w, conv_b, dt_bias_l, a_l, d_l]
    if h0 is not None:
        in_specs.append(pl.BlockSpec((None, None, 2, SSM_HPG, SSM_HEAD_DIM, SSM_STATE),
                                     lambda b, g: (b, layer, 0, g, 0, 0)))
        args.append(h0)
    out_specs = [pl.BlockSpec((seq, hp), lambda b, g: (b, g))]
    out_shape = [jax.ShapeDtypeStruct((n_batch * seq, inner), F32)]
    if emit_state:
        out_specs.append(pl.BlockSpec((None, 2, SSM_HPG, SSM_HEAD_DIM, SSM_STATE),
                                      lambda b, g: (b, 0, g, 0, 0)))
        out_shape.append(jax.ShapeDtypeStruct(
            (n_batch, 2, SSM_GROUPS * SSM_HPG, SSM_HEAD_DIM, SSM_STATE), F32))
    return pl.pallas_call(
        functools.partial(_ssd_body, has_h0=h0 is not None, emit_state=emit_state,
                          n_chunks=seq // SSD_CHUNK),
        grid=(n_batch, SSM_GROUPS),
        in_specs=in_specs,
        out_specs=out_specs,
        out_shape=out_shape,
        scratch_shapes=[pltpu.VMEM((seq, hp), F32), pltpu.VMEM((seq, SSM_STATE), F32),
                        pltpu.VMEM((seq, SSM_STATE), F32), pltpu.VMEM((seq, 128), F32),
                        pltpu.VMEM((seq, 128), F32),
                        pltpu.VMEM((seq // SSD_CHUNK, hp, SSD_CHUNK), F32),
                        pltpu.VMEM((seq // SSD_CHUNK, 128, SSD_CHUNK), F32),
                        pltpu.VMEM((seq // SSD_CHUNK, 128, SSD_CHUNK), F32),
                        pltpu.VMEM((2, SSM_HPG, SSM_HEAD_DIM, SSM_STATE), F32)],
        compiler_params=_params(2),
        name="ssd_scan",
    )(*args)


def _ssm_out_body(ya_ref, yb_ref, z_ref, g_ref, w_ref, o_ref, wbf, *, first_tiles):
    i = pl.program_id(0)

    @pl.when(i == 0)
    def _():
        wbf[...] = w_ref[...].astype(BF16)

    def gate_norm_project(y_ref):
        y = y_ref[...] * _silu(z_ref[...])
        y = y * lax.rsqrt(jnp.mean(y * y, axis=-1, keepdims=True) + LN_EPS) * g_ref[...]
        o_ref[...] = jnp.dot(y.astype(BF16), wbf[...], preferred_element_type=F32)

    pl.when(i < first_tiles)(lambda: gate_norm_project(ya_ref))
    pl.when(i >= first_tiles)(lambda: gate_norm_project(yb_ref))


def _ssm_out(y_parts, zxbc, norm_g, w_out, layer):
    t, inner = y_parts[0].shape[0] + y_parts[1].shape[0], y_parts[0].shape[1]
    d = w_out.shape[2]
    tm = SSM_OUT_ROW_TILE
    y_specs, first_tiles = _two_part_specs(y_parts, tm, 1)
    return pl.pallas_call(
        functools.partial(_ssm_out_body, first_tiles=first_tiles),
        grid=(t // tm,),
        in_specs=y_specs + [pl.BlockSpec((tm, inner), lambda i: (i, 0)),
                            pl.BlockSpec((None, 1, inner), lambda i: (layer, 0, 0)),
                            pl.BlockSpec((None, inner, d), lambda i: (layer, 0, 0))],
        out_specs=pl.BlockSpec((tm, d), lambda i: (i, 0)),
        out_shape=jax.ShapeDtypeStruct((t, d), F32),
        scratch_shapes=[pltpu.VMEM((inner, d), BF16)],
        compiler_params=_params(1),
        name="ssm_gate_norm_out",
    )(*y_parts, zxbc, norm_g.reshape(norm_g.shape[0], 1, inner), w_out)


def _gmlp_body(u_ref, v_ref, g_ref, b_ref, ws_ref, bs_ref, w_ref, o_ref, wbf, t_s, *, n_chunks):
    @pl.when(pl.program_id(0) == 0)
    def _():
        wbf[...] = w_ref[...].astype(BF16)

    v = v_ref[...]
    mu = jnp.mean(v, axis=-1, keepdims=True)
    vc = v - mu
    var = jnp.mean(vc * vc, axis=-1, keepdims=True)
    vn = (vc * lax.rsqrt(var + LN_EPS) * g_ref[...] + b_ref[...]).astype(BF16)
    q, gd = GMLP_CHUNK, GMLP_GROUP_DIM
    for g in range(GMLP_GROUPS):
        ws = ws_ref[g].astype(BF16)
        bias = bs_ref[:, g:g + 1]
        for c in range(n_chunks):
            sv = jnp.dot(ws, vn[c * q:(c + 1) * q, g * gd:(g + 1) * gd],
                         preferred_element_type=F32) + bias
            t_s[c * q:(c + 1) * q, g * gd:(g + 1) * gd] = (
                u_ref[c * q:(c + 1) * q, g * gd:(g + 1) * gd] * sv).astype(BF16)
    o_ref[...] = jnp.dot(t_s[...], wbf[...], preferred_element_type=F32)


def _gmlp_mix_out(uv, ln_g, ln_b, w_s, b_s, w_out, layer):
    t = uv.shape[0]
    half, d = GMLP_HALF, w_out.shape[2]
    tm = 512
    return pl.pallas_call(
        functools.partial(_gmlp_body, n_chunks=tm // GMLP_CHUNK),
        grid=(t // tm,),
        in_specs=[pl.BlockSpec((tm, half), lambda i: (i, 0)),
                  pl.BlockSpec((tm, half), lambda i: (i, 1)),
                  pl.BlockSpec((None, 1, half), lambda i: (layer, 0, 0)),
                  pl.BlockSpec((None, 1, half), lambda i: (layer, 0, 0)),
                  pl.BlockSpec((None, GMLP_GROUPS, GMLP_CHUNK, GMLP_CHUNK), lambda i: (layer, 0, 0, 0)),
                  pl.BlockSpec((None, GMLP_CHUNK, GMLP_GROUPS), lambda i: (layer, 0, 0)),
                  pl.BlockSpec((None, half, d), lambda i: (layer, 0, 0))],
        out_specs=pl.BlockSpec((tm, d), lambda i: (i, 0)),
        out_shape=jax.ShapeDtypeStruct((t, d), F32),
        scratch_shapes=[pltpu.VMEM((half, d), BF16), pltpu.VMEM((tm, half), BF16)],
        compiler_params=_params(1),
        name="gmlp_mix_out",
    )(uv, uv, ln_g.reshape(-1, 1, half), ln_b.reshape(-1, 1, half), w_s,
      jnp.swapaxes(b_s, 1, 2), w_out)


def _sorting_network(lo, hi):
    def merge(lo, hi, r):
        step = r * 2
        if step < hi - lo:
            yield from merge(lo, hi, step)
            yield from merge(lo + r, hi, step)
            yield from [(i, i + r) for i in range(lo + r, hi - r, step)]
        else:
            yield (lo, lo + r)

    if hi - lo >= 1:
        mid = lo + (hi - lo) // 2
        yield from _sorting_network(lo, mid)
        yield from _sorting_network(mid + 1, hi)
        yield from merge(lo, hi, 1)


def _top_rows(s, n, n_out):
    sub = 8
    n_slabs = s.shape[0] // sub
    v = [s[i * sub:(i + 1) * sub] for i in range(n_slabs)]
    for i, j in _sorting_network(0, n_slabs - 1):
        v[i], v[j] = jnp.maximum(v[i], v[j]), jnp.minimum(v[i], v[j])
    slot = lax.broadcasted_iota(jnp.int32, (n_out, s.shape[1]), 0)
    out = jnp.full((n_out, s.shape[1]), NEG_BIG, F32)
    for k in range(n):
        m = jnp.max(v[0], axis=0, keepdims=True)
        out = jnp.where(slot == k, m, out)
        hit = v[0] >= m
        depth = min(n_slabs, n - 1 - k)
        for i in range(depth):
            v[i] = jnp.where(hit, v[i + 1] if i + 1 < n_slabs else NEG_BIG, v[i])
    return out


def _peer_score_body(ht_ref, wq_ref, keys_ref, thr_ref, e1_ref, s2_ref, e2_ref, wbf):
    @pl.when(pl.program_id(0) == 0)
    def _():
        wbf[...] = wq_ref[...].astype(BF16)

    ht = ht_ref[...]
    kk, kh, sub = PEER_TOPK, PEER_HALF, PEER_ROWS_PER_BLOCK
    assert kk == 16 and sub == 8
    n_top = 3 * sub
    rank8 = lax.broadcasted_iota(jnp.int32, (sub, ht.shape[1]), 0)
    for h in range(PEER_HEADS):
        qh = jnp.dot(wbf[2 * kh * h:2 * kh * (h + 1), :], ht, preferred_element_type=F32)
        s1 = _dot_x3(keys_ref[h, 0], qh[:kh])
        s2 = _dot_x3(keys_ref[h, 1], qh[kh:])
        t1 = _top_rows(s1, kk + 1, n_top)
        t2 = _top_rows(s2, kk + 1, n_top)
        low, one = t1[0:sub], t2[0:sub]
        cand = jnp.concatenate([
            t1[0:1] + one,
            t1[0:1] + t2[sub:kk],
            t1[1:2] + one,
            jnp.where(rank8 < 5, t1[2:3] + one, NEG_BIG),
            jnp.where(rank8 < 4, t1[3:4] + one, low + t2[1:2]),
            jnp.where(rank8 >= 4, low + t2[0:1], NEG_BIG),
            t1[sub:kk] + t2[0:1],
            jnp.where(rank8 == 0, t1[kk:kk + 1] + t2[0:1],
                      jnp.where(rank8 == 1, t1[0:1] + t2[kk:kk + 1],
                                jnp.where(rank8 == 2, t1[4:5] + t2[2:3], NEG_BIG)))], axis=0)
        best = _top_rows(cand, kk + 1, n_top)
        z = jnp.sum(jnp.exp(best[:kk] - best[0:1]), axis=0, keepdims=True)
        thr = 0.5 * (best[kk - 1:kk] + best[kk:kk + 1]) - s1
        e1 = jnp.exp(s1 - t1[0:1])
        for rb in range(PEER_KEYS // sub):
            thr_ref[h, rb] = thr[rb * sub:(rb + 1) * sub]
            e1_ref[h, rb] = e1[rb * sub:(rb + 1) * sub]
        s2_ref[h] = s2
        e2_ref[h] = jnp.exp(s2 - t2[0:1]) * (0.5 / z)


def _peer_scores(ht, wq_t, sub_keys, layer):
    d, t = ht.shape
    tt = PEER_TOKEN_TILE
    nq = wq_t.shape[1]
    sub = PEER_ROWS_PER_BLOCK
    row_spec = pl.BlockSpec((PEER_HEADS, PEER_KEYS // sub, sub, tt), lambda i: (0, 0, 0, i))
    row_shape = jax.ShapeDtypeStruct((PEER_HEADS, PEER_KEYS // sub, sub, t), F32)
    tok_spec = pl.BlockSpec((PEER_HEADS, PEER_KEYS, tt), lambda i: (0, 0, i))
    tok_shape = jax.ShapeDtypeStruct((PEER_HEADS, PEER_KEYS, t), F32)
    return pl.pallas_call(
        _peer_score_body,
        grid=(t // tt,),
        in_specs=[pl.BlockSpec((d, tt), lambda i: (0, i)),
                  pl.BlockSpec((None, nq, d), lambda i: (layer, 0, 0)),
                  pl.BlockSpec((None, PEER_HEADS, 2, PEER_KEYS, PEER_HALF), lambda i: (layer, 0, 0, 0, 0))],
        out_specs=[row_spec, row_spec, tok_spec, tok_spec],
        out_shape=[row_shape, row_shape, tok_shape, tok_shape],
        scratch_shapes=[pltpu.VMEM((nq, d), BF16)],
        compiler_params=_params(1),
        name="peer_scores",
    )(ht, wq_t, sub_keys)


def _peer_tables_body(u_ref, v_ref, ub_ref, vt_ref):
    ub_ref[...] = u_ref[...].astype(BF16)
    vt_ref[...] = v_ref[...].T.astype(BF16)


def _peer_tables(u_tab, v_tab, layer):
    n, d = u_tab.shape[1:]
    eb = MM_COL_TILE
    return pl.pallas_call(
        _peer_tables_body,
        grid=(n // eb,),
        in_specs=[pl.BlockSpec((None, eb, d), lambda e: (layer, e, 0)),
                  pl.BlockSpec((None, eb, d), lambda e: (layer, e, 0))],
        out_specs=[pl.BlockSpec((eb, d), lambda e: (e, 0)), pl.BlockSpec((d, eb), lambda e: (0, e))],
        out_shape=[jax.ShapeDtypeStruct((n, d), BF16), jax.ShapeDtypeStruct((d, n), BF16)],
        compiler_params=_params(1),
        name="peer_tables",
    )(u_tab, v_tab)


def _peer_mix_body(ht_ref, thr_ref, e1_ref, s2_ref, e2_ref, u_ref, vt_ref, o_ref, act_s, coef_s):
    @pl.when(pl.program_id(1) == 0)
    def _():
        o_ref[...] = jnp.zeros(o_ref.shape, F32)

    lanes, kq = 128, 32
    n_q = PEER_KEYS // kq
    act_s[...] = jnp.dot(u_ref[...], ht_ref[...], preferred_element_type=F32)
    c0 = math.sqrt(2.0 / math.pi)

    def gate_chunk(ci, carry):
        part, q = ci // n_q, ci % n_q
        c = pl.ds(pl.multiple_of(part * lanes, lanes), lanes)
        keys = pl.ds(pl.multiple_of(q * kq, kq), kq)
        gates = [jnp.zeros((kq, lanes), F32)] * PEER_ROWS_PER_BLOCK
        for h in range(PEER_HEADS):
            s2 = s2_ref[h, keys, c]
            e2 = e2_ref[h, keys, c]
            for r in range(PEER_ROWS_PER_BLOCK):
                chosen = s2 >= thr_ref[h, 0, r:r + 1, c]
                gates[r] = gates[r] + jnp.where(chosen, e2, 0.0) * e1_ref[h, 0, r:r + 1, c]
        for r in range(PEER_ROWS_PER_BLOCK):
            rows = pl.ds(pl.multiple_of(r * PEER_KEYS + q * kq, kq), kq)
            a = act_s[rows, c]
            twice_gelu = a * (1.0 + jnp.tanh(a * (c0 + (c0 * 0.044715) * (a * a))))
            coef_s[rows, c] = (twice_gelu * gates[r]).astype(BF16)
        return carry

    lax.fori_loop(0, (o_ref.shape[1] // lanes) * n_q, gate_chunk, 0)
    o_ref[...] = o_ref[...] + jnp.dot(vt_ref[...], coef_s[...], preferred_element_type=F32)


def _peer_mix(ht, thr, e1, s2, e2, u_bf, vt_bf):
    d, t = ht.shape
    ts, rb = PEER_SUPER_TILE, PEER_ROWS_PER_BLOCK
    eb = rb * PEER_KEYS
    n_blocks = u_bf.shape[0] // eb
    row_spec = pl.BlockSpec((PEER_HEADS, 1, rb, ts), lambda s, e: (0, e, 0, s))
    tok_spec = pl.BlockSpec((PEER_HEADS, PEER_KEYS, ts), lambda s, e: (0, 0, s))
    return pl.pallas_call(
        _peer_mix_body,
        grid=(t // ts, n_blocks),
        in_specs=[pl.BlockSpec((d, ts), lambda s, e: (0, s)), row_spec, row_spec, tok_spec, tok_spec,
                  pl.BlockSpec((eb, d), lambda s, e: (e, 0)), pl.BlockSpec((d, eb), lambda s, e: (0, e))],
        out_specs=pl.BlockSpec((d, ts), lambda s, e: (0, s)),
        out_shape=jax.ShapeDtypeStruct((d, t), F32),
        scratch_shapes=[pltpu.VMEM((eb, ts), F32), pltpu.VMEM((eb, ts), BF16)],
        compiler_params=_params(2),
        name="peer_mix",
    )(ht, thr, e1, s2, e2, u_bf, vt_bf)


def kernel(x_prompt, x_sample, cache_k, cache_v, state_ssm, c, c_ctx, w_mod, b_mod, ln_g, ln_b, attn_w_qkv, attn_w_o, attn_lambda, attn_subln_g, ssm_w_in, ssm_conv_w, ssm_conv_b, ssm_dt_bias, ssm_a_log, ssm_d, ssm_norm_g, ssm_w_out, gmlp_w_in, gmlp_b_in, gmlp_ln_g, gmlp_ln_b, gmlp_w_s, gmlp_b_s, gmlp_w_out, peer_w_q, peer_sub_keys, peer_u, peer_v):
    bc, lc, d = x_prompt.shape
    bl, ll, _ = x_sample.shape
    depth = w_mod.shape[0]
    t_ctx, t_lat = bc * lc, bl * ll
    alpha = (2 * depth) ** 0.25
    assert 1 + bl <= N_MOD_GROUPS and t_ctx % ll == 0

    x = jnp.concatenate([x_prompt.reshape(t_ctx, d), x_sample.reshape(t_lat, d)], axis=0)
    cond = jnp.zeros((N_MOD_GROUPS, d), F32).at[0].set(c_ctx).at[1:1 + bl].set(c)
    mods = _modulation(cond, w_mod, b_mod).reshape(depth, N_MOD_GROUPS, 6, d)
    rope = _rope_tables(ll, 2 * ATTN_HEAD_DIM)
    past = cache_k.shape[2]
    ck = cache_k.reshape(bl, cache_k.shape[1], past, -1)
    cv = cache_v.reshape(bl, cache_v.shape[1], past, -1)
    wq_t = jnp.swapaxes(peer_w_q, 1, 2)

    new_k, new_v, new_s = [], [], []
    h = _modulate(x, mods[0], t_ctx, ll, shift=0, scale=1)
    for i in range(depth):
        kind, j = i % 3, i // 3
        if kind == 0:
            lam_init = 0.8 - 0.6 * math.exp(-0.3 * i)
            qkv, k_ctx, v_ctx = _qkv_rope(h, attn_w_qkv, j, rope, t_ctx, ll)
            o_ctx = _diff_attention(qkv, attn_lambda, attn_subln_g, j, lam_init, 0, bc, lc)
            o_lat = _diff_attention(qkv, attn_lambda, attn_subln_g, j, lam_init, t_ctx, bl, ll,
                                    cache=(ck, cv))
            o = _linear((o_ctx, o_lat), attn_w_o, j, d)
            new_k.append(k_ctx.reshape(bc, lc, ATTN_HEADS, 2, ATTN_HEAD_DIM))
            new_v.append(v_ctx.reshape(bc, lc, ATTN_HEADS, 2 * ATTN_HEAD_DIM))
        elif kind == 1:
            n_main = SSM_INNER + SSM_INNER + 2 * SSM_GROUPS * SSM_STATE
            zxbc = _linear(h, ssm_w_in, j, n_main)
            def regroup(a):
                a = a.reshape(a.shape[:-1] + (2, SSM_GROUPS, SSM_HPG))
                a = jnp.moveaxis(a, -3, -2).reshape(a.shape[:-3] + (SSM_GROUPS, 2 * SSM_HPG))
                pad = [(0, 0)] * (a.ndim - 1) + [(0, 128 - 2 * SSM_HPG)]
                return jnp.pad(a, pad).reshape(a.shape[:-2] + (SSM_GROUPS * 128,))
            w_dt = regroup(ssm_w_in[j][:, n_main:])[None]
            dtp = _linear(h, w_dt, 0, SSM_GROUPS * 128)
            dtb = regroup(ssm_dt_bias[j].reshape(1, -1))
            a_l = regroup(-jnp.exp(ssm_a_log[j]).reshape(1, -1))
            d_l = jnp.repeat(ssm_d[j], SSM_HEAD_DIM).reshape(1, SSM_INNER)
            cw, cb = ssm_conv_w[j], ssm_conv_b[j].reshape(1, -1)
            y_ctx, st = _ssd(zxbc, dtp, cw, cb, dtb, a_l, d_l, 0, bc, lc, emit_state=True)
            (y_lat,) = _ssd(zxbc, dtp, cw, cb, dtb, a_l, d_l, t_ctx, bl, ll, h0=state_ssm, layer=j)
            o = _ssm_out((y_ctx, y_lat), zxbc, ssm_norm_g, ssm_w_out, j)
            new_s.append(st)
        else:
            uv = _linear(h, gmlp_w_in, j, 2 * GMLP_HALF, bias=gmlp_b_in[j], act="gelu")
            o = _gmlp_mix_out(uv, gmlp_ln_g, gmlp_ln_b, gmlp_w_s, gmlp_b_s, gmlp_w_out, j)
        x, ht = _res_ln(x, o, mods[i], mods[i], ln_g[i, 0], ln_b[i, 0], t_ctx, ll, alpha=alpha,
                        gate=2, nshift=3, nscale=4, emit_ht=True)
        thr, e1, s2, e2 = _peer_scores(ht, wq_t, peer_sub_keys, i)
        u_bf, vt_bf = _peer_tables(peer_u, peer_v, i)
        f_t = _peer_mix(ht, thr, e1, s2, e2, u_bf, vt_bf)
        if i + 1 < depth:
            x, h = _res_ln(x, f_t, mods[i], mods[i + 1], ln_g[i, 1], ln_b[i, 1], t_ctx, ll,
                           alpha=alpha, gate=5, nshift=0, nscale=1, o_transposed=True, emit_h=True)
        else:
            (x,) = _res_ln(x, f_t, mods[i], mods[i], ln_g[i, 1], ln_b[i, 1], t_ctx, ll,
                           alpha=alpha, gate=5, o_transposed=True)
    return (x[:t_ctx].reshape(bc, lc, d), x[t_ctx:].reshape(bl, ll, d),
            jnp.stack(new_k, axis=1), jnp.stack(new_v, axis=1), jnp.stack(new_s, axis=1))
```

```python
import functools
import math

import jax
import jax.numpy as jnp
from jax import lax
from jax.experimental import pallas as pl
from jax.experimental.pallas import tpu as pltpu

F32 = jnp.float32
BF16 = jnp.bfloat16

D_MODEL = 1024
LN_EPS = 1e-5
GRID_W = 64
ATTN_HEADS = 8
ATTN_HEAD_DIM = 64
ROPE_AXIS_DIM = ATTN_HEAD_DIM // 2
ROPE_BASE = 10000.0
SSM_INNER = 2 * D_MODEL
SSM_HEAD_DIM = 64
SSM_GROUPS = 8
SSM_HPG = 4
SSM_STATE = 128
SSM_CONV = 5
SSD_CHUNK = 128
GMLP_HALF = 2 * D_MODEL
GMLP_GROUPS = 8
GMLP_GROUP_DIM = GMLP_HALF // GMLP_GROUPS
GMLP_CHUNK = 128
PEER_HEADS = 8
PEER_KEYS = 128
PEER_HALF = 128
PEER_TOPK = 16
N_MOD_GROUPS = 8
NEG_BIG = -1e30

VMEM_LIMIT = 56 * 1024 * 1024
ROW_TILE = 512
SSM_OUT_ROW_TILE = 256
MM_ROW_TILE = 1024
MM_COL_TILE = 1024
PEER_TOKEN_TILE = 256
PEER_SUPER_TILE = 1024
PEER_ROWS_PER_BLOCK = 8


def _params(n_axes):
    return pltpu.CompilerParams(dimension_semantics=("arbitrary",) * n_axes,
                                vmem_limit_bytes=VMEM_LIMIT)


def _bdot(a, b):
    return jnp.dot(a.astype(BF16), b.astype(BF16), preferred_element_type=F32)


def _bdot_nt(a, b):
    return lax.dot_general(a.astype(BF16), b.astype(BF16), (((1,), (1,)), ((), ())),
                           preferred_element_type=F32)


def _split3(a):
    hi = a.astype(BF16)
    r = a - hi.astype(F32)
    mid = r.astype(BF16)
    lo = (r - mid.astype(F32)).astype(BF16)
    return hi, mid, lo


def _dot_x3(a, b):
    ah, am, _ = _split3(a)
    bh, bm, _ = _split3(b)
    d = functools.partial(jnp.dot, preferred_element_type=F32)
    return d(ah, bh) + (d(am, bh) + d(ah, bm))


def _silu(x):
    return x * jax.nn.sigmoid(x)


def _gelu(x):
    return 0.5 * x * (1.0 + jnp.tanh(math.sqrt(2.0 / math.pi) * (x + 0.044715 * (x * x * x))))


def _group_index(i, tm, t_ctx, l_lat):
    start = i * tm
    return jnp.where(start < t_ctx, 0, 1 + (start - t_ctx) // l_lat)


def _mod_body(c_ref, w_ref, b_ref, o_ref):
    o_ref[0] = _dot_x3(_silu(c_ref[...]), w_ref[0]) + b_ref[0]


def _modulation(cond, w_mod, b_mod):
    depth, d, n = w_mod.shape
    tn = 1536
    return pl.pallas_call(
        _mod_body,
        grid=(depth, n // tn),
        in_specs=[pl.BlockSpec((N_MOD_GROUPS, d), lambda l, j: (0, 0)),
                  pl.BlockSpec((1, d, tn), lambda l, j: (l, 0, j)),
                  pl.BlockSpec((1, 1, tn), lambda l, j: (l, 0, j))],
        out_specs=pl.BlockSpec((1, N_MOD_GROUPS, tn), lambda l, j: (l, 0, j)),
        out_shape=jax.ShapeDtypeStruct((depth, N_MOD_GROUPS, n), F32),
        compiler_params=_params(2),
        name="modulation",
    )(cond, w_mod, b_mod.reshape(depth, 1, n))


def _modulate_body(xa_ref, xb_ref, m_ref, h_ref, *, shift, scale, first_tiles):
    def modulate(x_ref):
        m = m_ref[0]
        h_ref[...] = (x_ref[...] * (1.0 + m[scale:scale + 1]) + m[shift:shift + 1]).astype(BF16)

    i = pl.program_id(0)
    pl.when(i < first_tiles)(lambda: modulate(xa_ref))
    pl.when(i >= first_tiles)(lambda: modulate(xb_ref))


def _modulate(x_parts, mod, t_ctx, l_lat, shift, scale):
    t, d = x_parts[0].shape[0] + x_parts[1].shape[0], x_parts[0].shape[1]
    tm = ROW_TILE
    x_specs, first_tiles = _two_part_specs(x_parts, tm, 1)
    return pl.pallas_call(
        functools.partial(_modulate_body, shift=shift, scale=scale, first_tiles=first_tiles),
        grid=(t // tm,),
        in_specs=x_specs + [pl.BlockSpec((1, 6, d), lambda i: (_group_index(i, tm, t_ctx, l_lat), 0, 0))],
        out_specs=pl.BlockSpec((tm, d), lambda i: (i, 0)),
        out_shape=jax.ShapeDtypeStruct((t, d), BF16),
        compiler_params=_params(1),
        name="modulate",
    )(*x_parts, mod)


def _res_ln_body(*refs, alpha, gate, nshift, nscale, o_transposed, emit_h, emit_ht,
                 x_first_tiles, out_first_tiles):
    n_x = 1 if x_first_tiles is None else 2
    x_refs, (o_ref, m_ref, mn_ref, g_ref, b_ref), outs = refs[:n_x], refs[n_x:n_x + 5], refs[n_x + 5:]
    n_xo = 1 if out_first_tiles is None else 2
    i = pl.program_id(0)

    def residual_norm(x_ref):
        o = o_ref[...]
        if o_transposed:
            o = o.T
        m = m_ref[0]
        y = alpha * x_ref[...] + m[gate:gate + 1] * o
        mu = jnp.mean(y, axis=-1, keepdims=True)
        yc = y - mu
        var = jnp.mean(yc * yc, axis=-1, keepdims=True)
        xn = yc * lax.rsqrt(var + LN_EPS) * g_ref[...] + b_ref[...]
        if out_first_tiles is None:
            outs[0][...] = xn
        else:
            @pl.when(i < out_first_tiles)
            def _():
                outs[0][...] = xn

            @pl.when(i >= out_first_tiles)
            def _():
                outs[1][...] = xn
        if emit_h or emit_ht:
            mn = mn_ref[0]
            h = xn * (1.0 + mn[nscale:nscale + 1]) + mn[nshift:nshift + 1]
            k = n_xo
            if emit_h:
                outs[k][...] = h.astype(BF16)
                k += 1
            if emit_ht:
                outs[k][...] = h.T.astype(BF16)

    if x_first_tiles is None:
        residual_norm(x_refs[0])
    else:
        pl.when(i < x_first_tiles)(lambda: residual_norm(x_refs[0]))
        pl.when(i >= x_first_tiles)(lambda: residual_norm(x_refs[1]))


def _res_ln(x, o, mod, mod_next, ln_g, ln_b, t_ctx, l_lat, *, alpha, gate, nshift=0, nscale=1,
            o_transposed=False, emit_h=False, emit_ht=False, split_out=False):
    tm = ROW_TILE
    if isinstance(x, tuple):
        t, d = x[0].shape[0] + x[1].shape[0], x[0].shape[1]
        x_specs, x_first_tiles = _two_part_specs(x, tm, 1)
        x_args = list(x)
    else:
        t, d = x.shape
        x_specs, x_first_tiles, x_args = [pl.BlockSpec((tm, d), lambda i: (i, 0))], None, [x]
    grp = lambda i: (_group_index(i, tm, t_ctx, l_lat), 0, 0)
    o_spec = (pl.BlockSpec((d, tm), lambda i: (0, i)) if o_transposed
              else pl.BlockSpec((tm, d), lambda i: (i, 0)))
    if split_out:
        out_first_tiles = t_ctx // tm
        out_specs = [pl.BlockSpec((tm, d), lambda i: (jnp.minimum(i, out_first_tiles - 1), 0)),
                     pl.BlockSpec((tm, d), lambda i: (jnp.maximum(i - out_first_tiles, 0), 0))]
        out_shape = [jax.ShapeDtypeStruct((t_ctx, d), F32), jax.ShapeDtypeStruct((t - t_ctx, d), F32)]
    else:
        out_first_tiles = None
        out_specs = [pl.BlockSpec((tm, d), lambda i: (i, 0))]
        out_shape = [jax.ShapeDtypeStruct((t, d), F32)]
    if emit_h:
        out_specs.append(pl.BlockSpec((tm, d), lambda i: (i, 0)))
        out_shape.append(jax.ShapeDtypeStruct((t, d), BF16))
    if emit_ht:
        out_specs.append(pl.BlockSpec((d, tm), lambda i: (0, i)))
        out_shape.append(jax.ShapeDtypeStruct((d, t), BF16))
    return pl.pallas_call(
        functools.partial(_res_ln_body, alpha=alpha, gate=gate, nshift=nshift, nscale=nscale,
                          o_transposed=o_transposed, emit_h=emit_h, emit_ht=emit_ht,
                          x_first_tiles=x_first_tiles, out_first_tiles=out_first_tiles),
        grid=(t // tm,),
        in_specs=x_specs + [o_spec, pl.BlockSpec((1, 6, d), grp), pl.BlockSpec((1, 6, d), grp),
                            pl.BlockSpec((1, d), lambda i: (0, 0)), pl.BlockSpec((1, d), lambda i: (0, 0))],
        out_specs=out_specs,
        out_shape=out_shape,
        compiler_params=_params(1),
        name="residual_layernorm",
    )(*x_args, o, mod, mod_next, ln_g.reshape(1, d), ln_b.reshape(1, d))


def _two_part_specs(parts, tm, grid_rank):
    first, second = parts
    n_first = first.shape[0] // tm
    row = lambda idx: idx[-1] if grid_rank == 2 else idx[0]
    return ([pl.BlockSpec((tm, first.shape[1]), lambda *idx: (jnp.minimum(row(idx), n_first - 1), 0)),
             pl.BlockSpec((tm, second.shape[1]), lambda *idx: (jnp.maximum(row(idx) - n_first, 0), 0))],
            n_first)


def _linear_body(*refs, has_bias, act, first_tiles):
    n_src = 1 if first_tiles is None else 2
    srcs, w_ref, refs = refs[:n_src], refs[n_src], refs[n_src + 1:]
    if has_bias:
        b_ref, o_ref, wbf = refs
    else:
        o_ref, wbf = refs
    i = pl.program_id(1)

    @pl.when(i == 0)
    def _():
        wbf[...] = w_ref[...].astype(BF16)

    def project(h_ref):
        y = jnp.dot(h_ref[...], wbf[...], preferred_element_type=F32)
        if has_bias:
            y = y + b_ref[...]
        if act == "gelu":
            y = _gelu(y)
        o_ref[...] = y

    if first_tiles is None:
        project(srcs[0])
    else:
        pl.when(i < first_tiles)(lambda: project(srcs[0]))
        pl.when(i >= first_tiles)(lambda: project(srcs[1]))


def _linear(h, w_stack, layer, n_out, bias=None, act=None):
    tm, tn = MM_ROW_TILE, min(MM_COL_TILE, n_out)
    if isinstance(h, tuple):
        t, k = h[0].shape[0] + h[1].shape[0], h[0].shape[1]
        in_specs, first_tiles = _two_part_specs(h, tm, 2)
        args = list(h)
    else:
        t, k = h.shape
        in_specs, first_tiles = [pl.BlockSpec((tm, k), lambda n, i: (i, 0))], None
        args = [h]
    in_specs.append(pl.BlockSpec((None, k, tn), lambda n, i: (layer, 0, n)))
    args.append(w_stack)
    if bias is not None:
        in_specs.append(pl.BlockSpec((1, tn), lambda n, i: (0, n)))
        args.append(bias.reshape(1, n_out))
    return pl.pallas_call(
        functools.partial(_linear_body, has_bias=bias is not None, act=act, first_tiles=first_tiles),
        grid=(n_out // tn, t // tm),
        in_specs=in_specs,
        out_specs=pl.BlockSpec((tm, tn), lambda n, i: (i, n)),
        out_shape=jax.ShapeDtypeStruct((t, n_out), F32),
        scratch_shapes=[pltpu.VMEM((k, tn), BF16)],
        compiler_params=_params(2),
        name="linear",
    )(*args)


def _qkv_body(h_ref, w_ref, cos_ref, sa_ref, sb_ref, o_ref, kc_ref, vc_ref, wbf, *, n_ctx_tiles):
    n, i = pl.program_id(0), pl.program_id(1)

    @pl.when(i == 0)
    def _():
        wbf[...] = w_ref[...].astype(BF16)

    y = jnp.dot(h_ref[...], wbf[...], preferred_element_type=F32)
    rope = jnp.logical_and(n < 2, i >= n_ctx_tiles)
    half = ROPE_AXIS_DIM // 2

    @pl.when(jnp.logical_and(n == 1, i < n_ctx_tiles))
    def _():
        kc_ref[...] = y

    @pl.when(jnp.logical_and(n == 2, i < n_ctx_tiles))
    def _():
        vc_ref[...] = y

    @pl.when(rope)
    def _():
        w = cos_ref.shape[1]
        for c0 in range(0, y.shape[1], w):
            yg = y[:, c0:c0 + w]
            up = pltpu.roll(yg, w - half, 1)
            dn = pltpu.roll(yg, half, 1)
            o_ref[:, c0:c0 + w] = yg * cos_ref[...] + up * sa_ref[...] + dn * sb_ref[...]

    @pl.when(jnp.logical_not(rope))
    def _():
        o_ref[...] = y


def _qkv_rope(h, w_stack, layer, tables, t_ctx, l_lat):
    t, k = h.shape
    n_out = w_stack.shape[2]
    tm, tn = MM_ROW_TILE, MM_COL_TILE
    assert n_out == 3 * tn
    n_ctx_tiles = t_ctx // tm
    pos = lambda n, i: (jnp.maximum(i - n_ctx_tiles, 0) % (l_lat // tm), 0)
    last = n_ctx_tiles - 1
    k_rows = lambda n, i: (jnp.where(n == 1, jnp.minimum(i, last), jnp.where(n < 1, 0, last)), 0)
    v_rows = lambda n, i: (jnp.where(n == 2, jnp.minimum(i, last), 0), 0)
    ctx_shape = jax.ShapeDtypeStruct((t_ctx, tn), F32)
    return pl.pallas_call(
        functools.partial(_qkv_body, n_ctx_tiles=n_ctx_tiles),
        grid=(n_out // tn, t // tm),
        in_specs=[pl.BlockSpec((tm, k), lambda n, i: (i, 0)),
                  pl.BlockSpec((None, k, tn), lambda n, i: (layer, 0, n)),
                  *[pl.BlockSpec((tm, tab.shape[1]), pos) for tab in tables]],
        out_specs=[pl.BlockSpec((tm, tn), lambda n, i: (i, n)),
                   pl.BlockSpec((tm, tn), k_rows), pl.BlockSpec((tm, tn), v_rows)],
        out_shape=[jax.ShapeDtypeStruct((t, n_out), F32), ctx_shape, ctx_shape],
        scratch_shapes=[pltpu.VMEM((k, tn), BF16)],
        compiler_params=_params(2),
        name="qkv_rope",
    )(h, w_stack, *tables)


def _rope_tables(l_lat, width):
    rows = l_lat // GRID_W
    row_pos = jnp.repeat(jnp.arange(rows, dtype=F32), GRID_W)
    col_pos = jnp.tile(jnp.arange(GRID_W, dtype=F32), rows)
    inv_freq = ROPE_BASE ** (-jnp.arange(0, ROPE_AXIS_DIM, 2, dtype=F32) / ROPE_AXIS_DIM)

    def tab(p):
        ang = p[:, None] * inv_freq[None, :]
        ang = jnp.concatenate([ang, ang], -1)
        return jnp.cos(ang), jnp.sin(ang)

    cr, sr = tab(row_pos)
    cc, sc = tab(col_pos)
    cos = jnp.concatenate([cr, cc], -1)
    sin = jnp.concatenate([sr, sc], -1)
    first_half = (jnp.arange(ATTN_HEAD_DIM) % ROPE_AXIS_DIM) < (ROPE_AXIS_DIM // 2)
    sin_a = jnp.where(first_half, -sin, 0.0)
    sin_b = jnp.where(first_half, 0.0, sin)
    rep = width // ATTN_HEAD_DIM
    return tuple(jnp.tile(a, (1, rep)) for a in (cos, sin_a, sin_b))


def _attn_body(lam_ref, g_ref, q_ref, k_ref, v_ref, *refs, lam_init, past, tq):
    if past:
        kc_ref, vc_ref, o_ref, k_s, v_s = refs
        k_s[:past] = kc_ref[...].astype(BF16)
        v_s[:past] = vc_ref[...].astype(BF16)
    else:
        o_ref, k_s, v_s = refs
    k_s[past:] = k_ref[...].astype(BF16)
    v_s[past:] = v_ref[...].astype(BF16)
    lp = lam_ref[...]
    lam = (jnp.exp(jnp.sum(lp[0:1] * lp[1:2], axis=-1, keepdims=True))
           - jnp.exp(jnp.sum(lp[2:3] * lp[3:4], axis=-1, keepdims=True)) + lam_init)
    dh = ATTN_HEAD_DIM
    scale = dh ** -0.5

    def q_block(i, carry):
        rows = pl.ds(pl.multiple_of(i * tq, tq), tq)
        for c0 in range(0, q_ref.shape[1], 2 * dh):
            q = (q_ref[rows, c0:c0 + 2 * dh] * scale).astype(BF16)

            def softmax_map(m, q=q, c0=c0):
                s = _bdot_nt(q[:, m * dh:(m + 1) * dh], k_s[:, c0 + m * dh:c0 + (m + 1) * dh])
                e = jnp.exp(s - jnp.max(s, axis=-1, keepdims=True))
                return e / jnp.sum(e, axis=-1, keepdims=True)

            a = softmax_map(0) - lam * softmax_map(1)
            o = _bdot(a, v_s[:, c0:c0 + 2 * dh])
            o = o * lax.rsqrt(jnp.mean(o * o, axis=-1, keepdims=True) + LN_EPS) * g_ref[...]
            o_ref[rows, c0:c0 + 2 * dh] = (o * (1.0 - lam_init)).astype(o_ref.dtype)
        return carry

    lax.fori_loop(0, q_ref.shape[0] // tq, q_block, 0)


def _diff_attention(qkv, lam_p, subln_g, layer, lam_init, row0, n_batch, seq, cache=None):
    head_w = 2 * ATTN_HEAD_DIM
    hps = 1 if cache is not None else 4
    width, groups = hps * head_w, ATTN_HEADS // hps
    base = row0 // seq
    in_specs = [pl.BlockSpec((None, 4, ATTN_HEAD_DIM), lambda b, h: (layer, 0, 0)),
                pl.BlockSpec((None, 1, head_w), lambda b, h: (layer, 0, 0)),
                pl.BlockSpec((seq, width), lambda b, h: (base + b, h)),
                pl.BlockSpec((seq, width), lambda b, h: (base + b, groups + h)),
                pl.BlockSpec((seq, width), lambda b, h: (base + b, 2 * groups + h))]
    args = [lam_p, subln_g.reshape(subln_g.shape[0], 1, head_w), qkv, qkv, qkv]
    past = 0
    if cache is not None:
        ck, cv = cache
        past = ck.shape[2]
        spec = pl.BlockSpec((None, None, past, width), lambda b, h: (b, layer, 0, h))
        in_specs += [spec, spec]
        args += [ck, cv]
    return pl.pallas_call(
        functools.partial(_attn_body, lam_init=lam_init, past=past, tq=min(seq, 256)),
        grid=(n_batch, groups),
        in_specs=in_specs,
        out_specs=pl.BlockSpec((seq, width), lambda b, h: (b, h)),
        out_shape=jax.ShapeDtypeStruct((n_batch * seq, ATTN_HEADS * head_w), BF16),
        scratch_shapes=[pltpu.VMEM((past + seq, width), BF16)] * 2,
        compiler_params=_params(2),
        name="diff_attention",
    )(*args)


def _conv_silu(x, w, b):
    n = x.shape[0]
    row = lax.broadcasted_iota(jnp.int32, x.shape, 0)
    acc = x * w[SSM_CONV // 2:SSM_CONV // 2 + 1] + b
    for k in range(SSM_CONV):
        sh = k - SSM_CONV // 2
        if sh == 0:
            continue
        shifted = pltpu.roll(x, (-sh) % n, 0)
        valid = jnp.logical_and(row + sh >= 0, row + sh < n)
        acc = acc + jnp.where(valid, shifted, 0.0) * w[k:k + 1]
    return _silu(acc)


def _softplus(x):
    return jnp.maximum(x, 0.0) + jnp.log1p(jnp.exp(-jnp.abs(x)))


def _ssd_body(*refs, has_h0, emit_state, n_chunks):
    (xs_ref, bm_ref, cm_ref, dt_ref, wx_ref, bx_ref, wb_ref, bb_ref, wc_ref, bc_ref,
     dtb_ref, a_ref, d_ref) = refs[:13]
    refs = refs[13:]
    if has_h0:
        h0_ref, refs = refs[0], refs[1:]
    y_ref, refs = refs[0], refs[1:]
    if emit_state:
        st_ref, refs = refs[0], refs[1:]
    xs_s, b_s, c_s, dt_s, da_s, xt_s, dtt_s, dat_s, h_s = refs
    q = SSD_CHUNK
    p = SSM_HEAD_DIM
    dot = functools.partial(jnp.dot, preferred_element_type=F32)

    xs = _conv_silu(xs_ref[...], wx_ref[...], bx_ref[...])
    xs_s[...] = xs
    y_ref[...] = d_ref[...] * xs
    b_s[...] = _conv_silu(bm_ref[...], wb_ref[...], bb_ref[...])
    c_s[...] = _conv_silu(cm_ref[...], wc_ref[...], bc_ref[...])
    dt = _softplus(dt_ref[...] + dtb_ref[...])
    dt_s[...] = dt
    da_s[...] = dt * a_ref[...]

    def transposes(c, carry):
        rows = pl.ds(pl.multiple_of(c * q, q), q)
        xt_s[c] = xs_s[rows, :].T
        dtt_s[c] = dt_s[rows, :].T
        dat_s[c] = da_s[rows, :].T
        return carry

    lax.fori_loop(0, n_chunks, transposes, 0)

    row = lax.broadcasted_iota(jnp.int32, (q, q), 0)
    col = lax.broadcasted_iota(jnp.int32, (q, q), 1)
    lower, upper = row >= col, row <= col
    ones = [jnp.where(m, 1.0, 0.0).astype(BF16) for m in (lower, upper)]
    for direction in range(2):
        for r in range(SSM_HPG):
            if has_h0:
                h_s[direction, r] = h0_ref[direction, r]
            else:
                h_s[direction, r] = jnp.zeros((p, SSM_STATE), F32)

    def scan_chunk(c, direction):
        mask = lower if direction == 0 else upper
        end = q - 1 if direction == 0 else 0
        rows = pl.ds(pl.multiple_of(c * q, q), q)
        xc = xs_s[rows, :]
        bc = b_s[rows, :].astype(BF16)
        cc = c_s[rows, :].astype(BF16)
        dtc = dt_s[rows, :]
        dt_t = dtt_s[c]
        x_t = xt_s[c]
        d_hi, d_mid, d_lo = _split3(da_s[rows, :])
        tri = ones[direction]
        acum = dot(tri, d_hi) + (dot(tri, d_mid) + dot(tri, d_lo))
        t_hi, t_mid, t_lo = _split3(dat_s[c])
        tri_t = ones[1 - direction]
        acum_t = dot(t_hi, tri_t) + (dot(t_mid, tri_t) + dot(t_lo, tri_t))
        cb = _bdot_nt(cc, bc)
        a_end = acum[end:end + 1, :]
        ys = []
        for r in range(SSM_HPG):
            hl = direction * SSM_HPG + r
            a_col = acum[:, hl:hl + 1]
            a_row = acum_t[hl:hl + 1, :]
            decay = jnp.exp(jnp.where(mask, a_col - a_row, NEG_BIG))
            xdt = xc[:, r * p:(r + 1) * p] * dtc[:, hl:hl + 1]
            h = h_s[direction, r]
            y = _bdot(cb * decay, xdt) + jnp.exp(a_col) * _bdot_nt(cc, h)
            w = dt_t[hl:hl + 1, :] * jnp.exp(a_end[:, hl:hl + 1] - a_row)
            h_s[direction, r] = (jnp.exp(a_end[:, hl:hl + 1]) * h
                                 + _bdot(x_t[r * p:(r + 1) * p, :] * w, bc))
            ys.append(y)
        return rows, jnp.concatenate(ys, axis=1)

    def step(ci, carry):
        rows_f, y_f = scan_chunk(ci, 0)
        rows_b, y_b = scan_chunk(n_chunks - 1 - ci, 1)
        y_ref[rows_f, :] = y_ref[rows_f, :] + y_f
        y_ref[rows_b, :] = y_ref[rows_b, :] + y_b
        return carry

    lax.fori_loop(0, n_chunks, step, 0)
    if emit_state:
        for direction in range(2):
            for r in range(SSM_HPG):
                st_ref[direction, r] = h_s[direction, r]


def _ssd(zxbc, dtp, conv_w, conv_b, dt_bias_l, a_l, d_l, row0, n_batch, seq, h0=None, layer=0,
         emit_state=False):
    inner, gs, hp = SSM_INNER, SSM_GROUPS * SSM_STATE, SSM_HPG * SSM_HEAD_DIM
    base = row0 // seq
    g_of = lambda off, w: (lambda b, g: (0, off // w + g))
    row_blk = lambda off, w: (lambda b, g: (base + b, off // w + g))
    in_specs = [pl.BlockSpec((seq, hp), row_blk(inner, hp)),
                pl.BlockSpec((seq, SSM_STATE), row_blk(2 * inner, SSM_STATE)),
                pl.BlockSpec((seq, SSM_STATE), row_blk(2 * inner + gs, SSM_STATE)),
                pl.BlockSpec((seq, 128), row_blk(0, 128)),
                pl.BlockSpec((SSM_CONV, hp), g_of(0, hp)), pl.BlockSpec((1, hp), g_of(0, hp)),
                pl.BlockSpec((SSM_CONV, SSM_STATE), g_of(inner, SSM_STATE)),
                pl.BlockSpec((1, SSM_STATE), g_of(inner, SSM_STATE)),
                pl.BlockSpec((SSM_CONV, SSM_STATE), g_of(inner + gs, SSM_STATE)),
                pl.BlockSpec((1, SSM_STATE), g_of(inner + gs, SSM_STATE)),
                pl.BlockSpec((1, 128), g_of(0, 128)), pl.BlockSpec((1, 128), g_of(0, 128)),
                pl.BlockSpec((1, hp), g_of(0, hp))]
    args = [zxbc, zxbc, zxbc, dtp, conv_w, conv_b, conv_w, conv_b, conv_w, conv_b, dt_bias_l, a_l, d_l]
    if h0 is not None:
        in_specs.append(pl.BlockSpec((None, None, 2, SSM_HPG, SSM_HEAD_DIM, SSM_STATE),
                                     lambda b, g: (b, layer, 0, g, 0, 0)))
        args.append(h0)
    out_specs = [pl.BlockSpec((seq, hp), lambda b, g: (b, g))]
    out_shape = [jax.ShapeDtypeStruct((n_batch * seq, inner), F32)]
    if emit_state:
        out_specs.append(pl.BlockSpec((None, 2, SSM_HPG, SSM_HEAD_DIM, SSM_STATE),
                                      lambda b, g: (b, 0, g, 0, 0)))
        out_shape.append(jax.ShapeDtypeStruct(
            (n_batch, 2, SSM_GROUPS * SSM_HPG, SSM_HEAD_DIM, SSM_STATE), F32))
    return pl.pallas_call(
        functools.partial(_ssd_body, has_h0=h0 is not None, emit_state=emit_state,
                          n_chunks=seq // SSD_CHUNK),
        grid=(n_batch, SSM_GROUPS),
        in_specs=in_specs,
        out_specs=out_specs,
        out_shape=out_shape,
        scratch_shapes=[pltpu.VMEM((seq, hp), F32), pltpu.VMEM((seq, SSM_STATE), F32),
                        pltpu.VMEM((seq, SSM_STATE), F32), pltpu.VMEM((seq, 128), F32),
                        pltpu.VMEM((seq, 128), F32),
                        pltpu.VMEM((seq // SSD_CHUNK, hp, SSD_CHUNK), F32),
                        pltpu.VMEM((seq // SSD_CHUNK, 128, SSD_CHUNK), F32),
                        pltpu.VMEM((seq // SSD_CHUNK, 128, SSD_CHUNK), F32),
                        pltpu.VMEM((2, SSM_HPG, SSM_HEAD_DIM, SSM_STATE), F32)],
        compiler_params=_params(2),
        name="ssd_scan",
    )(*args)


def _ssm_out_body(ya_ref, yb_ref, z_ref, g_ref, w_ref, o_ref, wbf, *, first_tiles):
    i = pl.program_id(0)

    @pl.when(i == 0)
    def _():
        wbf[...] = w_ref[...].astype(BF16)

    def gate_norm_project(y_ref):
        y = y_ref[...] * _silu(z_ref[...])
        y = y * lax.rsqrt(jnp.mean(y * y, axis=-1, keepdims=True) + LN_EPS) * g_ref[...]
        o_ref[...] = jnp.dot(y.astype(BF16), wbf[...], preferred_element_type=F32)

    pl.when(i < first_tiles)(lambda: gate_norm_project(ya_ref))
    pl.when(i >= first_tiles)(lambda: gate_norm_project(yb_ref))


def _ssm_out(y_parts, zxbc, norm_g, w_out, layer):
    t, inner = y_parts[0].shape[0] + y_parts[1].shape[0], y_parts[0].shape[1]
    d = w_out.shape[2]
    tm = SSM_OUT_ROW_TILE
    y_specs, first_tiles = _two_part_specs(y_parts, tm, 1)
    return pl.pallas_call(
        functools.partial(_ssm_out_body, first_tiles=first_tiles),
        grid=(t // tm,),
        in_specs=y_specs + [pl.BlockSpec((tm, inner), lambda i: (i, 0)),
                            pl.BlockSpec((None, 1, inner), lambda i: (layer, 0, 0)),
                            pl.BlockSpec((None, inner, d), lambda i: (layer, 0, 0))],
        out_specs=pl.BlockSpec((tm, d), lambda i: (i, 0)),
        out_shape=jax.ShapeDtypeStruct((t, d), F32),
        scratch_shapes=[pltpu.VMEM((inner, d), BF16)],
        compiler_params=_params(1),
        name="ssm_gate_norm_out",
    )(*y_parts, zxbc, norm_g.reshape(norm_g.shape[0], 1, inner), w_out)


def _gmlp_body(u_ref, v_ref, g_ref, b_ref, ws_ref, bs_ref, w_ref, o_ref, wbf, t_s, *, n_chunks):
    @pl.when(pl.program_id(0) == 0)
    def _():
        wbf[...] = w_ref[...].astype(BF16)

    v = v_ref[...]
    mu = jnp.mean(v, axis=-1, keepdims=True)
    vc = v - mu
    var = jnp.mean(vc * vc, axis=-1, keepdims=True)
    vn = (vc * lax.rsqrt(var + LN_EPS) * g_ref[...] + b_ref[...]).astype(BF16)
    q, gd = GMLP_CHUNK, GMLP_GROUP_DIM
    for g in range(GMLP_GROUPS):
        ws = ws_ref[g].astype(BF16)
        bias = bs_ref[:, g:g + 1]
        for c in range(n_chunks):
            sv = jnp.dot(ws, vn[c * q:(c + 1) * q, g * gd:(g + 1) * gd],
                         preferred_element_type=F32) + bias
            t_s[c * q:(c + 1) * q, g * gd:(g + 1) * gd] = (
                u_ref[c * q:(c + 1) * q, g * gd:(g + 1) * gd] * sv).astype(BF16)
    o_ref[...] = jnp.dot(t_s[...], wbf[...], preferred_element_type=F32)


def _gmlp_mix_out(uv, ln_g, ln_b, w_s, b_s, w_out, layer):
    t = uv.shape[0]
    half, d = GMLP_HALF, w_out.shape[2]
    tm = 512
    return pl.pallas_call(
        functools.partial(_gmlp_body, n_chunks=tm // GMLP_CHUNK),
        grid=(t // tm,),
        in_specs=[pl.BlockSpec((tm, half), lambda i: (i, 0)),
                  pl.BlockSpec((tm, half), lambda i: (i, 1)),
                  pl.BlockSpec((None, 1, half), lambda i: (layer, 0, 0)),
                  pl.BlockSpec((None, 1, half), lambda i: (layer, 0, 0)),
                  pl.BlockSpec((None, GMLP_GROUPS, GMLP_CHUNK, GMLP_CHUNK), lambda i: (layer, 0, 0, 0)),
                  pl.BlockSpec((None, GMLP_CHUNK, GMLP_GROUPS), lambda i: (layer, 0, 0)),
                  pl.BlockSpec((None, half, d), lambda i: (layer, 0, 0))],
        out_specs=pl.BlockSpec((tm, d), lambda i: (i, 0)),
        out_shape=jax.ShapeDtypeStruct((t, d), F32),
        scratch_shapes=[pltpu.VMEM((half, d), BF16), pltpu.VMEM((tm, half), BF16)],
        compiler_params=_params(1),
        name="gmlp_mix_out",
    )(uv, uv, ln_g.reshape(-1, 1, half), ln_b.reshape(-1, 1, half), w_s,
      jnp.swapaxes(b_s, 1, 2), w_out)


def _sorting_network(lo, hi):
    def merge(lo, hi, r):
        step = r * 2
        if step < hi - lo:
            yield from merge(lo, hi, step)
            yield from merge(lo + r, hi, step)
            yield from [(i, i + r) for i in range(lo + r, hi - r, step)]
        else:
            yield (lo, lo + r)

    if hi - lo >= 1:
        mid = lo + (hi - lo) // 2
        yield from _sorting_network(lo, mid)
        yield from _sorting_network(mid + 1, hi)
        yield from merge(lo, hi, 1)


def _top_rows(s, n, n_out):
    sub = 8
    n_slabs = s.shape[0] // sub
    v = [s[i * sub:(i + 1) * sub] for i in range(n_slabs)]
    for i, j in _sorting_network(0, n_slabs - 1):
        v[i], v[j] = jnp.maximum(v[i], v[j]), jnp.minimum(v[i], v[j])
    slot = lax.broadcasted_iota(jnp.int32, (n_out, s.shape[1]), 0)
    out = jnp.full((n_out, s.shape[1]), NEG_BIG, F32)
    for k in range(n):
        m = jnp.max(v[0], axis=0, keepdims=True)
        out = jnp.where(slot == k, m, out)
        hit = v[0] >= m
        depth = min(n_slabs, n - 1 - k)
        for i in range(depth):
            v[i] = jnp.where(hit, v[i + 1] if i + 1 < n_slabs else NEG_BIG, v[i])
    return out


def _peer_score_body(ht_ref, wq_ref, keys_ref, thr_ref, e1_ref, s2_ref, e2_ref, wbf):
    @pl.when(pl.program_id(0) == 0)
    def _():
        wbf[...] = wq_ref[...].astype(BF16)

    ht = ht_ref[...]
    kk, kh, sub = PEER_TOPK, PEER_HALF, PEER_ROWS_PER_BLOCK
    assert kk == 16 and sub == 8
    n_top = 3 * sub
    rank8 = lax.broadcasted_iota(jnp.int32, (sub, ht.shape[1]), 0)
    for h in range(PEER_HEADS):
        qh = jnp.dot(wbf[2 * kh * h:2 * kh * (h + 1), :], ht, preferred_element_type=F32)
        s1 = _dot_x3(keys_ref[h, 0], qh[:kh])
        s2 = _dot_x3(keys_ref[h, 1], qh[kh:])
        t1 = _top_rows(s1, kk + 1, n_top)
        t2 = _top_rows(s2, kk + 1, n_top)
        low, one = t1[0:sub], t2[0:sub]
        cand = jnp.concatenate([
            t1[0:1] + one,
            t1[0:1] + t2[sub:kk],
            t1[1:2] + one,
            jnp.where(rank8 < 5, t1[2:3] + one, NEG_BIG),
            jnp.where(rank8 < 4, t1[3:4] + one, low + t2[1:2]),
            jnp.where(rank8 >= 4, low + t2[0:1], NEG_BIG),
            t1[sub:kk] + t2[0:1],
            jnp.where(rank8 == 0, t1[kk:kk + 1] + t2[0:1],
                      jnp.where(rank8 == 1, t1[0:1] + t2[kk:kk + 1],
                                jnp.where(rank8 == 2, t1[4:5] + t2[2:3], NEG_BIG)))], axis=0)
        best = _top_rows(cand, kk + 1, n_top)
        z = jnp.sum(jnp.exp(best[:kk] - best[0:1]), axis=0, keepdims=True)
        thr = 0.5 * (best[kk - 1:kk] + best[kk:kk + 1]) - s1
        e1 = jnp.exp(s1 - t1[0:1])
        for rb in range(PEER_KEYS // sub):
            thr_ref[h, rb] = thr[rb * sub:(rb + 1) * sub]
            e1_ref[h, rb] = e1[rb * sub:(rb + 1) * sub]
        s2_ref[h] = s2
        e2_ref[h] = jnp.exp(s2 - t2[0:1]) * (0.5 / z)


def _peer_scores(ht, wq_t, sub_keys, layer):
    d, t = ht.shape
    tt = PEER_TOKEN_TILE
    nq = wq_t.shape[1]
    sub = PEER_ROWS_PER_BLOCK
    row_spec = pl.BlockSpec((PEER_HEADS, PEER_KEYS // sub, sub, tt), lambda i: (0, 0, 0, i))
    row_shape = jax.ShapeDtypeStruct((PEER_HEADS, PEER_KEYS // sub, sub, t), F32)
    tok_spec = pl.BlockSpec((PEER_HEADS, PEER_KEYS, tt), lambda i: (0, 0, i))
    tok_shape = jax.ShapeDtypeStruct((PEER_HEADS, PEER_KEYS, t), F32)
    return pl.pallas_call(
        _peer_score_body,
        grid=(t // tt,),
        in_specs=[pl.BlockSpec((d, tt), lambda i: (0, i)),
                  pl.BlockSpec((None, nq, d), lambda i: (layer, 0, 0)),
                  pl.BlockSpec((None, PEER_HEADS, 2, PEER_KEYS, PEER_HALF), lambda i: (layer, 0, 0, 0, 0))],
        out_specs=[row_spec, row_spec, tok_spec, tok_spec],
        out_shape=[row_shape, row_shape, tok_shape, tok_shape],
        scratch_shapes=[pltpu.VMEM((nq, d), BF16)],
        compiler_params=_params(1),
        name="peer_scores",
    )(ht, wq_t, sub_keys)


def _peer_tables_body(u_ref, v_ref, ub_ref, vt_ref):
    ub_ref[...] = u_ref[...].astype(BF16)
    vt_ref[...] = v_ref[...].T.astype(BF16)


def _peer_tables(u_tab, v_tab, layer):
    n, d = u_tab.shape[1:]
    eb = MM_COL_TILE
    return pl.pallas_call(
        _peer_tables_body,
        grid=(n // eb,),
        in_specs=[pl.BlockSpec((None, eb, d), lambda e: (layer, e, 0)),
                  pl.BlockSpec((None, eb, d), lambda e: (layer, e, 0))],
        out_specs=[pl.BlockSpec((eb, d), lambda e: (e, 0)), pl.BlockSpec((d, eb), lambda e: (0, e))],
        out_shape=[jax.ShapeDtypeStruct((n, d), BF16), jax.ShapeDtypeStruct((d, n), BF16)],
        compiler_params=_params(1),
        name="peer_tables",
    )(u_tab, v_tab)


def _peer_mix_body(ht_ref, thr_ref, e1_ref, s2_ref, e2_ref, u_ref, vt_ref, o_ref, act_s, coef_s):
    @pl.when(pl.program_id(1) == 0)
    def _():
        o_ref[...] = jnp.zeros(o_ref.shape, F32)

    lanes, kq = 128, 32
    n_q = PEER_KEYS // kq
    act_s[...] = jnp.dot(u_ref[...], ht_ref[...], preferred_element_type=F32)
    c0 = math.sqrt(2.0 / math.pi)

    def gate_chunk(ci, carry):
        part, q = ci // n_q, ci % n_q
        c = pl.ds(pl.multiple_of(part * lanes, lanes), lanes)
        keys = pl.ds(pl.multiple_of(q * kq, kq), kq)
        gates = [jnp.zeros((kq, lanes), F32)] * PEER_ROWS_PER_BLOCK
        for h in range(PEER_HEADS):
            s2 = s2_ref[h, keys, c]
            e2 = e2_ref[h, keys, c]
            for r in range(PEER_ROWS_PER_BLOCK):
                chosen = s2 >= thr_ref[h, 0, r:r + 1, c]
                gates[r] = gates[r] + jnp.where(chosen, e2, 0.0) * e1_ref[h, 0, r:r + 1, c]
        for r in range(PEER_ROWS_PER_BLOCK):
            rows = pl.ds(pl.multiple_of(r * PEER_KEYS + q * kq, kq), kq)
            a = act_s[rows, c]
            twice_gelu = a * (1.0 + jnp.tanh(a * (c0 + (c0 * 0.044715) * (a * a))))
            coef_s[rows, c] = (twice_gelu * gates[r]).astype(BF16)
        return carry

    lax.fori_loop(0, (o_ref.shape[1] // lanes) * n_q, gate_chunk, 0)
    o_ref[...] = o_ref[...] + jnp.dot(vt_ref[...], coef_s[...], preferred_element_type=F32)


def _peer_mix(ht, thr, e1, s2, e2, u_bf, vt_bf):
    d, t = ht.shape
    ts, rb = PEER_SUPER_TILE, PEER_ROWS_PER_BLOCK
    eb = rb * PEER_KEYS
    n_blocks = u_bf.shape[0] // eb
    row_spec = pl.BlockSpec((PEER_HEADS, 1, rb, ts), lambda s, e: (0, e, 0, s))
    tok_spec = pl.BlockSpec((PEER_HEADS, PEER_KEYS, ts), lambda s, e: (0, 0, s))
    return pl.pallas_call(
        _peer_mix_body,
        grid=(t // ts, n_blocks),
        in_specs=[pl.BlockSpec((d, ts), lambda s, e: (0, s)), row_spec, row_spec, tok_spec, tok_spec,
                  pl.BlockSpec((eb, d), lambda s, e: (e, 0)), pl.BlockSpec((d, eb), lambda s, e: (0, e))],
        out_specs=pl.BlockSpec((d, ts), lambda s, e: (0, s)),
        out_shape=jax.ShapeDtypeStruct((d, t), F32),
        scratch_shapes=[pltpu.VMEM((eb, ts), F32), pltpu.VMEM((eb, ts), BF16)],
        compiler_params=_params(2),
        name="peer_mix",
    )(ht, thr, e1, s2, e2, u_bf, vt_bf)


def kernel(x_prompt, x_sample, cache_k, cache_v, state_ssm, c, c_ctx, w_mod, b_mod, ln_g, ln_b, attn_w_qkv, attn_w_o, attn_lambda, attn_subln_g, ssm_w_in, ssm_conv_w, ssm_conv_b, ssm_dt_bias, ssm_a_log, ssm_d, ssm_norm_g, ssm_w_out, gmlp_w_in, gmlp_b_in, gmlp_ln_g, gmlp_ln_b, gmlp_w_s, gmlp_b_s, gmlp_w_out, peer_w_q, peer_sub_keys, peer_u, peer_v):
    bc, lc, d = x_prompt.shape
    bl, ll, _ = x_sample.shape
    depth = w_mod.shape[0]
    t_ctx, t_lat = bc * lc, bl * ll
    alpha = (2 * depth) ** 0.25
    assert 1 + bl <= N_MOD_GROUPS and t_ctx % ll == 0

    x = (x_prompt.reshape(t_ctx, d), x_sample.reshape(t_lat, d))
    cond = jnp.zeros((N_MOD_GROUPS, d), F32).at[0].set(c_ctx).at[1:1 + bl].set(c)
    mods = _modulation(cond, w_mod, b_mod).reshape(depth, N_MOD_GROUPS, 6, d)
    rope = _rope_tables(ll, 2 * ATTN_HEAD_DIM)
    past = cache_k.shape[2]
    ck = cache_k.reshape(bl, cache_k.shape[1], past, -1)
    cv = cache_v.reshape(bl, cache_v.shape[1], past, -1)
    wq_t = jnp.swapaxes(peer_w_q, 1, 2)

    new_k, new_v, new_s = [], [], []
    h = _modulate(x, mods[0], t_ctx, ll, shift=0, scale=1)
    for i in range(depth):
        kind, j = i % 3, i // 3
        if kind == 0:
            lam_init = 0.8 - 0.6 * math.exp(-0.3 * i)
            qkv, k_ctx, v_ctx = _qkv_rope(h, attn_w_qkv, j, rope, t_ctx, ll)
            o_ctx = _diff_attention(qkv, attn_lambda, attn_subln_g, j, lam_init, 0, bc, lc)
            o_lat = _diff_attention(qkv, attn_lambda, attn_subln_g, j, lam_init, t_ctx, bl, ll,
                                    cache=(ck, cv))
            o = _linear((o_ctx, o_lat), attn_w_o, j, d)
            new_k.append(k_ctx.reshape(bc, lc, ATTN_HEADS, 2, ATTN_HEAD_DIM))
            new_v.append(v_ctx.reshape(bc, lc, ATTN_HEADS, 2 * ATTN_HEAD_DIM))
        elif kind == 1:
            n_main = SSM_INNER + SSM_INNER + 2 * SSM_GROUPS * SSM_STATE
            zxbc = _linear(h, ssm_w_in, j, n_main)
            def regroup(a):
                a = a.reshape(a.shape[:-1] + (2, SSM_GROUPS, SSM_HPG))
                a = jnp.moveaxis(a, -3, -2).reshape(a.shape[:-3] + (SSM_GROUPS, 2 * SSM_HPG))
                pad = [(0, 0)] * (a.ndim - 1) + [(0, 128 - 2 * SSM_HPG)]
                return jnp.pad(a, pad).reshape(a.shape[:-2] + (SSM_GROUPS * 128,))
            w_dt = regroup(ssm_w_in[j][:, n_main:])[None]
            dtp = _linear(h, w_dt, 0, SSM_GROUPS * 128)
            dtb = regroup(ssm_dt_bias[j].reshape(1, -1))
            a_l = regroup(-jnp.exp(ssm_a_log[j]).reshape(1, -1))
            d_l = jnp.repeat(ssm_d[j], SSM_HEAD_DIM).reshape(1, SSM_INNER)
            cw, cb = ssm_conv_w[j], ssm_conv_b[j].reshape(1, -1)
            y_ctx, st = _ssd(zxbc, dtp, cw, cb, dtb, a_l, d_l, 0, bc, lc, emit_state=True)
            (y_lat,) = _ssd(zxbc, dtp, cw, cb, dtb, a_l, d_l, t_ctx, bl, ll, h0=state_ssm, layer=j)
            o = _ssm_out((y_ctx, y_lat), zxbc, ssm_norm_g, ssm_w_out, j)
            new_s.append(st)
        else:
            uv = _linear(h, gmlp_w_in, j, 2 * GMLP_HALF, bias=gmlp_b_in[j], act="gelu")
            o = _gmlp_mix_out(uv, gmlp_ln_g, gmlp_ln_b, gmlp_w_s, gmlp_b_s, gmlp_w_out, j)
        x, ht = _res_ln(x, o, mods[i], mods[i], ln_g[i, 0], ln_b[i, 0], t_ctx, ll, alpha=alpha,
                        gate=2, nshift=3, nscale=4, emit_ht=True)
        thr, e1, s2, e2 = _peer_scores(ht, wq_t, peer_sub_keys, i)
        u_bf, vt_bf = _peer_tables(peer_u, peer_v, i)
        f_t = _peer_mix(ht, thr, e1, s2, e2, u_bf, vt_bf)
        if i + 1 < depth:
            x, h = _res_ln(x, f_t, mods[i], mods[i + 1], ln_g[i, 1], ln_b[i, 1], t_ctx, ll,
                           alpha=alpha, gate=5, nshift=0, nscale=1, o_transposed=True, emit_h=True)
        else:
            out_ctx, out_lat = _res_ln(x, f_t, mods[i], mods[i], ln_g[i, 1], ln_b[i, 1], t_ctx, ll,
                                       alpha=alpha, gate=5, o_transposed=True, split_out=True)
    return (out_ctx.reshape(bc, lc, d), out_lat.reshape(bl, ll, d),
            jnp.stack(new_k, axis=1), jnp.stack(new_v, axis=1), jnp.stack(new_s, axis=1))
```

```python
import functools
import math

import jax
import jax.numpy as jnp
from jax import lax
from jax.experimental import pallas as pl
from jax.experimental.pallas import tpu as pltpu

F32 = jnp.float32
BF16 = jnp.bfloat16

D_MODEL = 1024
LN_EPS = 1e-5
GRID_W = 64
ATTN_HEADS = 8
ATTN_HEAD_DIM = 64
ROPE_AXIS_DIM = ATTN_HEAD_DIM // 2
ROPE_BASE = 10000.0
SSM_INNER = 2 * D_MODEL
SSM_HEAD_DIM = 64
SSM_GROUPS = 8
SSM_HPG = 4
SSM_STATE = 128
SSM_CONV = 5
SSD_CHUNK = 128
GMLP_HALF = 2 * D_MODEL
GMLP_GROUPS = 8
GMLP_GROUP_DIM = GMLP_HALF // GMLP_GROUPS
GMLP_CHUNK = 128
PEER_HEADS = 8
PEER_KEYS = 128
PEER_HALF = 128
PEER_TOPK = 16
N_MOD_GROUPS = 8
NEG_BIG = -1e30

VMEM_LIMIT = 56 * 1024 * 1024
ROW_TILE = 512
SSM_OUT_ROW_TILE = 256
MM_ROW_TILE = 1024
MM_COL_TILE = 1024
PEER_TOKEN_TILE = 512
PEER_SUPER_TILE = 1024
PEER_ROWS_PER_BLOCK = 8


def _params(n_axes):
    return pltpu.CompilerParams(dimension_semantics=("arbitrary",) * n_axes,
                                vmem_limit_bytes=VMEM_LIMIT)


def _bdot(a, b):
    return jnp.dot(a.astype(BF16), b.astype(BF16), preferred_element_type=F32)


def _bdot_nt(a, b):
    return lax.dot_general(a.astype(BF16), b.astype(BF16), (((1,), (1,)), ((), ())),
                           preferred_element_type=F32)


def _split3(a):
    hi = a.astype(BF16)
    r = a - hi.astype(F32)
    mid = r.astype(BF16)
    lo = (r - mid.astype(F32)).astype(BF16)
    return hi, mid, lo


def _dot_x3(a, b):
    ah, am, _ = _split3(a)
    bh, bm, _ = _split3(b)
    d = functools.partial(jnp.dot, preferred_element_type=F32)
    return d(ah, bh) + (d(am, bh) + d(ah, bm))


def _silu(x):
    return x * jax.nn.sigmoid(x)


def _gelu(x):
    return 0.5 * x * (1.0 + jnp.tanh(math.sqrt(2.0 / math.pi) * (x + 0.044715 * (x * x * x))))


def _group_index(i, tm, t_ctx, l_lat):
    start = i * tm
    return jnp.where(start < t_ctx, 0, 1 + (start - t_ctx) // l_lat)


def _mod_body(c_ref, w_ref, b_ref, o_ref):
    o_ref[0] = _dot_x3(_silu(c_ref[...]), w_ref[0]) + b_ref[0]


def _modulation(cond, w_mod, b_mod):
    depth, d, n = w_mod.shape
    tn = 1536
    return pl.pallas_call(
        _mod_body,
        grid=(depth, n // tn),
        in_specs=[pl.BlockSpec((N_MOD_GROUPS, d), lambda l, j: (0, 0)),
                  pl.BlockSpec((1, d, tn), lambda l, j: (l, 0, j)),
                  pl.BlockSpec((1, 1, tn), lambda l, j: (l, 0, j))],
        out_specs=pl.BlockSpec((1, N_MOD_GROUPS, tn), lambda l, j: (l, 0, j)),
        out_shape=jax.ShapeDtypeStruct((depth, N_MOD_GROUPS, n), F32),
        compiler_params=_params(2),
        name="modulation",
    )(cond, w_mod, b_mod.reshape(depth, 1, n))


def _modulate_body(xa_ref, xb_ref, m_ref, h_ref, *, shift, scale, first_tiles):
    def modulate(x_ref):
        m = m_ref[0]
        h_ref[...] = (x_ref[...] * (1.0 + m[scale:scale + 1]) + m[shift:shift + 1]).astype(BF16)

    i = pl.program_id(0)
    pl.when(i < first_tiles)(lambda: modulate(xa_ref))
    pl.when(i >= first_tiles)(lambda: modulate(xb_ref))


def _modulate(x_parts, mod, t_ctx, l_lat, shift, scale):
    t, d = x_parts[0].shape[0] + x_parts[1].shape[0], x_parts[0].shape[1]
    tm = ROW_TILE
    x_specs, first_tiles = _two_part_specs(x_parts, tm, 1)
    return pl.pallas_call(
        functools.partial(_modulate_body, shift=shift, scale=scale, first_tiles=first_tiles),
        grid=(t // tm,),
        in_specs=x_specs + [pl.BlockSpec((1, 6, d), lambda i: (_group_index(i, tm, t_ctx, l_lat), 0, 0))],
        out_specs=pl.BlockSpec((tm, d), lambda i: (i, 0)),
        out_shape=jax.ShapeDtypeStruct((t, d), BF16),
        compiler_params=_params(1),
        name="modulate",
    )(*x_parts, mod)


def _res_ln_body(*refs, alpha, gate, nshift, nscale, o_transposed, emit_h, emit_ht,
                 x_first_tiles, out_first_tiles):
    n_x = 1 if x_first_tiles is None else 2
    x_refs, (o_ref, m_ref, mn_ref, g_ref, b_ref), outs = refs[:n_x], refs[n_x:n_x + 5], refs[n_x + 5:]
    n_xo = 1 if out_first_tiles is None else 2
    i = pl.program_id(0)

    def residual_norm(x_ref):
        o = o_ref[...]
        if o_transposed:
            o = o.T
        m = m_ref[0]
        y = alpha * x_ref[...] + m[gate:gate + 1] * o
        mu = jnp.mean(y, axis=-1, keepdims=True)
        yc = y - mu
        var = jnp.mean(yc * yc, axis=-1, keepdims=True)
        xn = yc * lax.rsqrt(var + LN_EPS) * g_ref[...] + b_ref[...]
        if out_first_tiles is None:
            outs[0][...] = xn
        else:
            @pl.when(i < out_first_tiles)
            def _():
                outs[0][...] = xn

            @pl.when(i >= out_first_tiles)
            def _():
                outs[1][...] = xn
        if emit_h or emit_ht:
            mn = mn_ref[0]
            h = xn * (1.0 + mn[nscale:nscale + 1]) + mn[nshift:nshift + 1]
            k = n_xo
            if emit_h:
                outs[k][...] = h.astype(BF16)
                k += 1
            if emit_ht:
                outs[k][...] = h.T.astype(BF16)

    if x_first_tiles is None:
        residual_norm(x_refs[0])
    else:
        pl.when(i < x_first_tiles)(lambda: residual_norm(x_refs[0]))
        pl.when(i >= x_first_tiles)(lambda: residual_norm(x_refs[1]))


def _res_ln(x, o, mod, mod_next, ln_g, ln_b, t_ctx, l_lat, *, alpha, gate, nshift=0, nscale=1,
            o_transposed=False, emit_h=False, emit_ht=False, split_out=False):
    tm = ROW_TILE
    if isinstance(x, tuple):
        t, d = x[0].shape[0] + x[1].shape[0], x[0].shape[1]
        x_specs, x_first_tiles = _two_part_specs(x, tm, 1)
        x_args = list(x)
    else:
        t, d = x.shape
        x_specs, x_first_tiles, x_args = [pl.BlockSpec((tm, d), lambda i: (i, 0))], None, [x]
    grp = lambda i: (_group_index(i, tm, t_ctx, l_lat), 0, 0)
    o_spec = (pl.BlockSpec((d, tm), lambda i: (0, i)) if o_transposed
              else pl.BlockSpec((tm, d), lambda i: (i, 0)))
    if split_out:
        out_first_tiles = t_ctx // tm
        out_specs = [pl.BlockSpec((tm, d), lambda i: (jnp.minimum(i, out_first_tiles - 1), 0)),
                     pl.BlockSpec((tm, d), lambda i: (jnp.maximum(i - out_first_tiles, 0), 0))]
        out_shape = [jax.ShapeDtypeStruct((t_ctx, d), F32), jax.ShapeDtypeStruct((t - t_ctx, d), F32)]
    else:
        out_first_tiles = None
        out_specs = [pl.BlockSpec((tm, d), lambda i: (i, 0))]
        out_shape = [jax.ShapeDtypeStruct((t, d), F32)]
    if emit_h:
        out_specs.append(pl.BlockSpec((tm, d), lambda i: (i, 0)))
        out_shape.append(jax.ShapeDtypeStruct((t, d), BF16))
    if emit_ht:
        out_specs.append(pl.BlockSpec((d, tm), lambda i: (0, i)))
        out_shape.append(jax.ShapeDtypeStruct((d, t), BF16))
    return pl.pallas_call(
        functools.partial(_res_ln_body, alpha=alpha, gate=gate, nshift=nshift, nscale=nscale,
                          o_transposed=o_transposed, emit_h=emit_h, emit_ht=emit_ht,
                          x_first_tiles=x_first_tiles, out_first_tiles=out_first_tiles),
        grid=(t // tm,),
        in_specs=x_specs + [o_spec, pl.BlockSpec((1, 6, d), grp), pl.BlockSpec((1, 6, d), grp),
                            pl.BlockSpec((1, d), lambda i: (0, 0)), pl.BlockSpec((1, d), lambda i: (0, 0))],
        out_specs=out_specs,
        out_shape=out_shape,
        compiler_params=_params(1),
        name="residual_layernorm",
    )(*x_args, o, mod, mod_next, ln_g.reshape(1, d), ln_b.reshape(1, d))


def _two_part_specs(parts, tm, grid_rank):
    first, second = parts
    n_first = first.shape[0] // tm
    row = lambda idx: idx[-1] if grid_rank == 2 else idx[0]
    return ([pl.BlockSpec((tm, first.shape[1]), lambda *idx: (jnp.minimum(row(idx), n_first - 1), 0)),
             pl.BlockSpec((tm, second.shape[1]), lambda *idx: (jnp.maximum(row(idx) - n_first, 0), 0))],
            n_first)


def _linear_body(*refs, has_bias, act, first_tiles):
    n_src = 1 if first_tiles is None else 2
    srcs, w_ref, refs = refs[:n_src], refs[n_src], refs[n_src + 1:]
    if has_bias:
        b_ref, o_ref, wbf = refs
    else:
        o_ref, wbf = refs
    i = pl.program_id(1)

    @pl.when(i == 0)
    def _():
        wbf[...] = w_ref[...].astype(BF16)

    def project(h_ref):
        y = jnp.dot(h_ref[...], wbf[...], preferred_element_type=F32)
        if has_bias:
            y = y + b_ref[...]
        if act == "gelu":
            y = _gelu(y)
        o_ref[...] = y

    if first_tiles is None:
        project(srcs[0])
    else:
        pl.when(i < first_tiles)(lambda: project(srcs[0]))
        pl.when(i >= first_tiles)(lambda: project(srcs[1]))


def _linear(h, w_stack, layer, n_out, bias=None, act=None):
    tm, tn = MM_ROW_TILE, min(MM_COL_TILE, n_out)
    if isinstance(h, tuple):
        t, k = h[0].shape[0] + h[1].shape[0], h[0].shape[1]
        in_specs, first_tiles = _two_part_specs(h, tm, 2)
        args = list(h)
    else:
        t, k = h.shape
        in_specs, first_tiles = [pl.BlockSpec((tm, k), lambda n, i: (i, 0))], None
        args = [h]
    in_specs.append(pl.BlockSpec((None, k, tn), lambda n, i: (layer, 0, n)))
    args.append(w_stack)
    if bias is not None:
        in_specs.append(pl.BlockSpec((1, tn), lambda n, i: (0, n)))
        args.append(bias.reshape(1, n_out))
    return pl.pallas_call(
        functools.partial(_linear_body, has_bias=bias is not None, act=act, first_tiles=first_tiles),
        grid=(n_out // tn, t // tm),
        in_specs=in_specs,
        out_specs=pl.BlockSpec((tm, tn), lambda n, i: (i, n)),
        out_shape=jax.ShapeDtypeStruct((t, n_out), F32),
        scratch_shapes=[pltpu.VMEM((k, tn), BF16)],
        compiler_params=_params(2),
        name="linear",
    )(*args)


def _qkv_body(h_ref, w_ref, cos_ref, sa_ref, sb_ref, o_ref, kc_ref, vc_ref, wbf, *, n_ctx_tiles):
    n, i = pl.program_id(0), pl.program_id(1)

    @pl.when(i == 0)
    def _():
        wbf[...] = w_ref[...].astype(BF16)

    y = jnp.dot(h_ref[...], wbf[...], preferred_element_type=F32)
    rope = jnp.logical_and(n < 2, i >= n_ctx_tiles)
    half = ROPE_AXIS_DIM // 2

    @pl.when(jnp.logical_and(n == 1, i < n_ctx_tiles))
    def _():
        kc_ref[...] = y

    @pl.when(jnp.logical_and(n == 2, i < n_ctx_tiles))
    def _():
        vc_ref[...] = y

    @pl.when(rope)
    def _():
        w = cos_ref.shape[1]
        for c0 in range(0, y.shape[1], w):
            yg = y[:, c0:c0 + w]
            up = pltpu.roll(yg, w - half, 1)
            dn = pltpu.roll(yg, half, 1)
            o_ref[:, c0:c0 + w] = yg * cos_ref[...] + up * sa_ref[...] + dn * sb_ref[...]

    @pl.when(jnp.logical_not(rope))
    def _():
        o_ref[...] = y


def _qkv_rope(h, w_stack, layer, tables, t_ctx, l_lat):
    t, k = h.shape
    n_out = w_stack.shape[2]
    tm, tn = MM_ROW_TILE, MM_COL_TILE
    assert n_out == 3 * tn
    n_ctx_tiles = t_ctx // tm
    pos = lambda n, i: (jnp.maximum(i - n_ctx_tiles, 0) % (l_lat // tm), 0)
    last = n_ctx_tiles - 1
    k_rows = lambda n, i: (jnp.where(n == 1, jnp.minimum(i, last), jnp.where(n < 1, 0, last)), 0)
    v_rows = lambda n, i: (jnp.where(n == 2, jnp.minimum(i, last), 0), 0)
    ctx_shape = jax.ShapeDtypeStruct((t_ctx, tn), F32)
    return pl.pallas_call(
        functools.partial(_qkv_body, n_ctx_tiles=n_ctx_tiles),
        grid=(n_out // tn, t // tm),
        in_specs=[pl.BlockSpec((tm, k), lambda n, i: (i, 0)),
                  pl.BlockSpec((None, k, tn), lambda n, i: (layer, 0, n)),
                  *[pl.BlockSpec((tm, tab.shape[1]), pos) for tab in tables]],
        out_specs=[pl.BlockSpec((tm, tn), lambda n, i: (i, n)),
                   pl.BlockSpec((tm, tn), k_rows), pl.BlockSpec((tm, tn), v_rows)],
        out_shape=[jax.ShapeDtypeStruct((t, n_out), F32), ctx_shape, ctx_shape],
        scratch_shapes=[pltpu.VMEM((k, tn), BF16)],
        compiler_params=_params(2),
        name="qkv_rope",
    )(h, w_stack, *tables)


def _rope_tables(l_lat, width):
    rows = l_lat // GRID_W
    row_pos = jnp.repeat(jnp.arange(rows, dtype=F32), GRID_W)
    col_pos = jnp.tile(jnp.arange(GRID_W, dtype=F32), rows)
    inv_freq = ROPE_BASE ** (-jnp.arange(0, ROPE_AXIS_DIM, 2, dtype=F32) / ROPE_AXIS_DIM)

    def tab(p):
        ang = p[:, None] * inv_freq[None, :]
        ang = jnp.concatenate([ang, ang], -1)
        return jnp.cos(ang), jnp.sin(ang)

    cr, sr = tab(row_pos)
    cc, sc = tab(col_pos)
    cos = jnp.concatenate([cr, cc], -1)
    sin = jnp.concatenate([sr, sc], -1)
    first_half = (jnp.arange(ATTN_HEAD_DIM) % ROPE_AXIS_DIM) < (ROPE_AXIS_DIM // 2)
    sin_a = jnp.where(first_half, -sin, 0.0)
    sin_b = jnp.where(first_half, 0.0, sin)
    rep = width // ATTN_HEAD_DIM
    return tuple(jnp.tile(a, (1, rep)) for a in (cos, sin_a, sin_b))


def _attn_body(lam_ref, g_ref, q_ref, k_ref, v_ref, *refs, lam_init, past, tq):
    if past:
        kc_ref, vc_ref, o_ref, k_s, v_s = refs
        k_s[:past] = kc_ref[...].astype(BF16)
        v_s[:past] = vc_ref[...].astype(BF16)
    else:
        o_ref, k_s, v_s = refs
    k_s[past:] = k_ref[...].astype(BF16)
    v_s[past:] = v_ref[...].astype(BF16)
    lp = lam_ref[...]
    lam = (jnp.exp(jnp.sum(lp[0:1] * lp[1:2], axis=-1, keepdims=True))
           - jnp.exp(jnp.sum(lp[2:3] * lp[3:4], axis=-1, keepdims=True)) + lam_init)
    dh = ATTN_HEAD_DIM
    scale = dh ** -0.5

    def q_block(i, carry):
        rows = pl.ds(pl.multiple_of(i * tq, tq), tq)
        for c0 in range(0, q_ref.shape[1], 2 * dh):
            q = (q_ref[rows, c0:c0 + 2 * dh] * scale).astype(BF16)

            def softmax_map(m, q=q, c0=c0):
                s = _bdot_nt(q[:, m * dh:(m + 1) * dh], k_s[:, c0 + m * dh:c0 + (m + 1) * dh])
                e = jnp.exp(s - jnp.max(s, axis=-1, keepdims=True))
                return e / jnp.sum(e, axis=-1, keepdims=True)

            a = softmax_map(0) - lam * softmax_map(1)
            o = _bdot(a, v_s[:, c0:c0 + 2 * dh])
            o = o * lax.rsqrt(jnp.mean(o * o, axis=-1, keepdims=True) + LN_EPS) * g_ref[...]
            o_ref[rows, c0:c0 + 2 * dh] = (o * (1.0 - lam_init)).astype(o_ref.dtype)
        return carry

    lax.fori_loop(0, q_ref.shape[0] // tq, q_block, 0)


def _diff_attention(qkv, lam_p, subln_g, layer, lam_init, row0, n_batch, seq, cache=None):
    head_w = 2 * ATTN_HEAD_DIM
    hps = 1 if cache is not None else 4
    width, groups = hps * head_w, ATTN_HEADS // hps
    base = row0 // seq
    in_specs = [pl.BlockSpec((None, 4, ATTN_HEAD_DIM), lambda b, h: (layer, 0, 0)),
                pl.BlockSpec((None, 1, head_w), lambda b, h: (layer, 0, 0)),
                pl.BlockSpec((seq, width), lambda b, h: (base + b, h)),
                pl.BlockSpec((seq, width), lambda b, h: (base + b, groups + h)),
                pl.BlockSpec((seq, width), lambda b, h: (base + b, 2 * groups + h))]
    args = [lam_p, subln_g.reshape(subln_g.shape[0], 1, head_w), qkv, qkv, qkv]
    past = 0
    if cache is not None:
        ck, cv = cache
        past = ck.shape[2]
        spec = pl.BlockSpec((None, None, past, width), lambda b, h: (b, layer, 0, h))
        in_specs += [spec, spec]
        args += [ck, cv]
    return pl.pallas_call(
        functools.partial(_attn_body, lam_init=lam_init, past=past, tq=min(seq, 256)),
        grid=(n_batch, groups),
        in_specs=in_specs,
        out_specs=pl.BlockSpec((seq, width), lambda b, h: (b, h)),
        out_shape=jax.ShapeDtypeStruct((n_batch * seq, ATTN_HEADS * head_w), BF16),
        scratch_shapes=[pltpu.VMEM((past + seq, width), BF16)] * 2,
        compiler_params=_params(2),
        name="diff_attention",
    )(*args)


def _conv_silu(x, w, b):
    n = x.shape[0]
    row = lax.broadcasted_iota(jnp.int32, x.shape, 0)
    acc = x * w[SSM_CONV // 2:SSM_CONV // 2 + 1] + b
    for k in range(SSM_CONV):
        sh = k - SSM_CONV // 2
        if sh == 0:
            continue
        shifted = pltpu.roll(x, (-sh) % n, 0)
        valid = jnp.logical_and(row + sh >= 0, row + sh < n)
        acc = acc + jnp.where(valid, shifted, 0.0) * w[k:k + 1]
    return _silu(acc)


def _softplus(x):
    return jnp.maximum(x, 0.0) + jnp.log1p(jnp.exp(-jnp.abs(x)))


def _ssd_body(*refs, has_h0, emit_state, n_chunks):
    (xs_ref, bm_ref, cm_ref, dt_ref, wx_ref, bx_ref, wb_ref, bb_ref, wc_ref, bc_ref,
     dtb_ref, a_ref, d_ref) = refs[:13]
    refs = refs[13:]
    if has_h0:
        h0_ref, refs = refs[0], refs[1:]
    y_ref, refs = refs[0], refs[1:]
    if emit_state:
        st_ref, refs = refs[0], refs[1:]
    xs_s, b_s, c_s, dt_s, da_s, xt_s, dtt_s, dat_s, h_s = refs
    q = SSD_CHUNK
    p = SSM_HEAD_DIM
    dot = functools.partial(jnp.dot, preferred_element_type=F32)

    xs = _conv_silu(xs_ref[...], wx_ref[...], bx_ref[...])
    xs_s[...] = xs
    y_ref[...] = d_ref[...] * xs
    b_s[...] = _conv_silu(bm_ref[...], wb_ref[...], bb_ref[...])
    c_s[...] = _conv_silu(cm_ref[...], wc_ref[...], bc_ref[...])
    dt = _softplus(dt_ref[...] + dtb_ref[...])
    dt_s[...] = dt
    da_s[...] = dt * a_ref[...]

    def transposes(c, carry):
        rows = pl.ds(pl.multiple_of(c * q, q), q)
        xt_s[c] = xs_s[rows, :].T
        dtt_s[c] = dt_s[rows, :].T
        dat_s[c] = da_s[rows, :].T
        return carry

    lax.fori_loop(0, n_chunks, transposes, 0)

    row = lax.broadcasted_iota(jnp.int32, (q, q), 0)
    col = lax.broadcasted_iota(jnp.int32, (q, q), 1)
    lower, upper = row >= col, row <= col
    ones = [jnp.where(m, 1.0, 0.0).astype(BF16) for m in (lower, upper)]
    for direction in range(2):
        for r in range(SSM_HPG):
            if has_h0:
                h_s[direction, r] = h0_ref[direction, r]
            else:
                h_s[direction, r] = jnp.zeros((p, SSM_STATE), F32)

    def scan_chunk(c, direction):
        mask = lower if direction == 0 else upper
        end = q - 1 if direction == 0 else 0
        rows = pl.ds(pl.multiple_of(c * q, q), q)
        xc = xs_s[rows, :]
        bc = b_s[rows, :].astype(BF16)
        cc = c_s[rows, :].astype(BF16)
        dtc = dt_s[rows, :]
        dt_t = dtt_s[c]
        x_t = xt_s[c]
        d_hi, d_mid, d_lo = _split3(da_s[rows, :])
        tri = ones[direction]
        acum = dot(tri, d_hi) + (dot(tri, d_mid) + dot(tri, d_lo))
        t_hi, t_mid, t_lo = _split3(dat_s[c])
        tri_t = ones[1 - direction]
        acum_t = dot(t_hi, tri_t) + (dot(t_mid, tri_t) + dot(t_lo, tri_t))
        cb = _bdot_nt(cc, bc)
        a_end = acum[end:end + 1, :]
        ys = []
        for r in range(SSM_HPG):
            hl = direction * SSM_HPG + r
            a_col = acum[:, hl:hl + 1]
            a_row = acum_t[hl:hl + 1, :]
            decay = jnp.exp(jnp.where(mask, a_col - a_row, NEG_BIG))
            xdt = xc[:, r * p:(r + 1) * p] * dtc[:, hl:hl + 1]
            h = h_s[direction, r]
            y = _bdot(cb * decay, xdt) + jnp.exp(a_col) * _bdot_nt(cc, h)
            w = dt_t[hl:hl + 1, :] * jnp.exp(a_end[:, hl:hl + 1] - a_row)
            h_s[direction, r] = (jnp.exp(a_end[:, hl:hl + 1]) * h
                                 + _bdot(x_t[r * p:(r + 1) * p, :] * w, bc))
            ys.append(y)
        return rows, jnp.concatenate(ys, axis=1)

    def step(ci, carry):
        rows_f, y_f = scan_chunk(ci, 0)
        rows_b, y_b = scan_chunk(n_chunks - 1 - ci, 1)
        y_ref[rows_f, :] = y_ref[rows_f, :] + y_f
        y_ref[rows_b, :] = y_ref[rows_b, :] + y_b
        return carry

    lax.fori_loop(0, n_chunks, step, 0)
    if emit_state:
        for direction in range(2):
            for r in range(SSM_HPG):
                st_ref[direction, r] = h_s[direction, r]


def _ssd(zxbc, dtp, conv_w, conv_b, dt_bias_l, a_l, d_l, row0, n_batch, seq, h0=None, layer=0,
         emit_state=False):
    inner, gs, hp = SSM_INNER, SSM_GROUPS * SSM_STATE, SSM_HPG * SSM_HEAD_DIM
    base = row0 // seq
    g_of = lambda off, w: (lambda b, g: (0, off // w + g))
    row_blk = lambda off, w: (lambda b, g: (base + b, off // w + g))
    in_specs = [pl.BlockSpec((seq, hp), row_blk(inner, hp)),
                pl.BlockSpec((seq, SSM_STATE), row_blk(2 * inner, SSM_STATE)),
                pl.BlockSpec((seq, SSM_STATE), row_blk(2 * inner + gs, SSM_STATE)),
                pl.BlockSpec((seq, 128), row_blk(0, 128)),
                pl.BlockSpec((SSM_CONV, hp), g_of(0, hp)), pl.BlockSpec((1, hp), g_of(0, hp)),
                pl.BlockSpec((SSM_CONV, SSM_STATE), g_of(inner, SSM_STATE)),
                pl.BlockSpec((1, SSM_STATE), g_of(inner, SSM_STATE)),
                pl.BlockSpec((SSM_CONV, SSM_STATE), g_of(inner + gs, SSM_STATE)),
                pl.BlockSpec((1, SSM_STATE), g_of(inner + gs, SSM_STATE)),
                pl.BlockSpec((1, 128), g_of(0, 128)), pl.BlockSpec((1, 128), g_of(0, 128)),
                pl.BlockSpec((1, hp), g_of(0, hp))]
    args = [zxbc, zxbc, zxbc, dtp, conv_w, conv_b, conv_w, conv_b, conv_w, conv_b, dt_bias_l, a_l, d_l]
    if h0 is not None:
        in_specs.append(pl.BlockSpec((None, None, 2, SSM_HPG, SSM_HEAD_DIM, SSM_STATE),
                                     lambda b, g: (b, layer, 0, g, 0, 0)))
        args.append(h0)
    out_specs = [pl.BlockSpec((seq, hp), lambda b, g: (b, g))]
    out_shape = [jax.ShapeDtypeStruct((n_batch * seq, inner), F32)]
    if emit_state:
        out_specs.append(pl.BlockSpec((None, 2, SSM_HPG, SSM_HEAD_DIM, SSM_STATE),
                                      lambda b, g: (b, 0, g, 0, 0)))
        out_shape.append(jax.ShapeDtypeStruct(
            (n_batch, 2, SSM_GROUPS * SSM_HPG, SSM_HEAD_DIM, SSM_STATE), F32))
    return pl.pallas_call(
        functools.partial(_ssd_body, has_h0=h0 is not None, emit_state=emit_state,
                          n_chunks=seq // SSD_CHUNK),
        grid=(n_batch, SSM_GROUPS),
        in_specs=in_specs,
        out_specs=out_specs,
        out_shape=out_shape,
        scratch_shapes=[pltpu.VMEM((seq, hp), F32), pltpu.VMEM((seq, SSM_STATE), F32),
                        pltpu.VMEM((seq, SSM_STATE), F32), pltpu.VMEM((seq, 128), F32),
                        pltpu.VMEM((seq, 128), F32),
                        pltpu.VMEM((seq // SSD_CHUNK, hp, SSD_CHUNK), F32),
                        pltpu.VMEM((seq // SSD_CHUNK, 128, SSD_CHUNK), F32),
                        pltpu.VMEM((seq // SSD_CHUNK, 128, SSD_CHUNK), F32),
                        pltpu.VMEM((2, SSM_HPG, SSM_HEAD_DIM, SSM_STATE), F32)],
        compiler_params=_params(2),
        name="ssd_scan",
    )(*args)


def _ssm_out_body(ya_ref, yb_ref, z_ref, g_ref, w_ref, o_ref, wbf, *, first_tiles):
    i = pl.program_id(0)

    @pl.when(i == 0)
    def _():
        wbf[...] = w_ref[...].astype(BF16)

    def gate_norm_project(y_ref):
        y = y_ref[...] * _silu(z_ref[...])
        y = y * lax.rsqrt(jnp.mean(y * y, axis=-1, keepdims=True) + LN_EPS) * g_ref[...]
        o_ref[...] = jnp.dot(y.astype(BF16), wbf[...], preferred_element_type=F32)

    pl.when(i < first_tiles)(lambda: gate_norm_project(ya_ref))
    pl.when(i >= first_tiles)(lambda: gate_norm_project(yb_ref))


def _ssm_out(y_parts, zxbc, norm_g, w_out, layer):
    t, inner = y_parts[0].shape[0] + y_parts[1].shape[0], y_parts[0].shape[1]
    d = w_out.shape[2]
    tm = SSM_OUT_ROW_TILE
    y_specs, first_tiles = _two_part_specs(y_parts, tm, 1)
    return pl.pallas_call(
        functools.partial(_ssm_out_body, first_tiles=first_tiles),
        grid=(t // tm,),
        in_specs=y_specs + [pl.BlockSpec((tm, inner), lambda i: (i, 0)),
                            pl.BlockSpec((None, 1, inner), lambda i: (layer, 0, 0)),
                            pl.BlockSpec((None, inner, d), lambda i: (layer, 0, 0))],
        out_specs=pl.BlockSpec((tm, d), lambda i: (i, 0)),
        out_shape=jax.ShapeDtypeStruct((t, d), F32),
        scratch_shapes=[pltpu.VMEM((inner, d), BF16)],
        compiler_params=_params(1),
        name="ssm_gate_norm_out",
    )(*y_parts, zxbc, norm_g.reshape(norm_g.shape[0], 1, inner), w_out)


def _gmlp_body(u_ref, v_ref, g_ref, b_ref, ws_ref, bs_ref, w_ref, o_ref, wbf, t_s, *, n_chunks):
    @pl.when(pl.program_id(0) == 0)
    def _():
        wbf[...] = w_ref[...].astype(BF16)

    v = v_ref[...]
    mu = jnp.mean(v, axis=-1, keepdims=True)
    vc = v - mu
    var = jnp.mean(vc * vc, axis=-1, keepdims=True)
    vn = (vc * lax.rsqrt(var + LN_EPS) * g_ref[...] + b_ref[...]).astype(BF16)
    q, gd = GMLP_CHUNK, GMLP_GROUP_DIM
    for g in range(GMLP_GROUPS):
        ws = ws_ref[g].astype(BF16)
        bias = bs_ref[:, g:g + 1]
        for c in range(n_chunks):
            sv = jnp.dot(ws, vn[c * q:(c + 1) * q, g * gd:(g + 1) * gd],
                         preferred_element_type=F32) + bias
            t_s[c * q:(c + 1) * q, g * gd:(g + 1) * gd] = (
                u_ref[c * q:(c + 1) * q, g * gd:(g + 1) * gd] * sv).astype(BF16)
    o_ref[...] = jnp.dot(t_s[...], wbf[...], preferred_element_type=F32)


def _gmlp_mix_out(uv, ln_g, ln_b, w_s, b_s, w_out, layer):
    t = uv.shape[0]
    half, d = GMLP_HALF, w_out.shape[2]
    tm = 512
    return pl.pallas_call(
        functools.partial(_gmlp_body, n_chunks=tm // GMLP_CHUNK),
        grid=(t // tm,),
        in_specs=[pl.BlockSpec((tm, half), lambda i: (i, 0)),
                  pl.BlockSpec((tm, half), lambda i: (i, 1)),
                  pl.BlockSpec((None, 1, half), lambda i: (layer, 0, 0)),
                  pl.BlockSpec((None, 1, half), lambda i: (layer, 0, 0)),
                  pl.BlockSpec((None, GMLP_GROUPS, GMLP_CHUNK, GMLP_CHUNK), lambda i: (layer, 0, 0, 0)),
                  pl.BlockSpec((None, GMLP_CHUNK, GMLP_GROUPS), lambda i: (layer, 0, 0)),
                  pl.BlockSpec((None, half, d), lambda i: (layer, 0, 0))],
        out_specs=pl.BlockSpec((tm, d), lambda i: (i, 0)),
        out_shape=jax.ShapeDtypeStruct((t, d), F32),
        scratch_shapes=[pltpu.VMEM((half, d), BF16), pltpu.VMEM((tm, half), BF16)],
        compiler_params=_params(1),
        name="gmlp_mix_out",
    )(uv, uv, ln_g.reshape(-1, 1, half), ln_b.reshape(-1, 1, half), w_s,
      jnp.swapaxes(b_s, 1, 2), w_out)


def _sorting_network(lo, hi):
    def merge(lo, hi, r):
        step = r * 2
        if step < hi - lo:
            yield from merge(lo, hi, step)
            yield from merge(lo + r, hi, step)
            yield from [(i, i + r) for i in range(lo + r, hi - r, step)]
        else:
            yield (lo, lo + r)

    if hi - lo >= 1:
        mid = lo + (hi - lo) // 2
        yield from _sorting_network(lo, mid)
        yield from _sorting_network(mid + 1, hi)
        yield from merge(lo, hi, 1)


def _top_rows(s, n, n_out):
    sub = 8
    n_slabs = s.shape[0] // sub
    v = [s[i * sub:(i + 1) * sub] for i in range(n_slabs)]
    for i, j in _sorting_network(0, n_slabs - 1):
        v[i], v[j] = jnp.maximum(v[i], v[j]), jnp.minimum(v[i], v[j])
    slot = lax.broadcasted_iota(jnp.int32, (n_out, s.shape[1]), 0)
    out = jnp.full((n_out, s.shape[1]), NEG_BIG, F32)
    for k in range(n):
        m = jnp.max(v[0], axis=0, keepdims=True)
        out = jnp.where(slot == k, m, out)
        hit = v[0] >= m
        depth = min(n_slabs, n - 1 - k)
        for i in range(depth):
            v[i] = jnp.where(hit, v[i + 1] if i + 1 < n_slabs else NEG_BIG, v[i])
    return out


def _peer_score_body(ht_ref, wq_ref, keys_ref, thr_ref, e1_ref, s2_ref, e2_ref, wbf):
    @pl.when(pl.program_id(0) == 0)
    def _():
        wbf[...] = wq_ref[...].astype(BF16)

    ht = ht_ref[...]
    kk, kh, sub = PEER_TOPK, PEER_HALF, PEER_ROWS_PER_BLOCK
    assert kk == 16 and sub == 8
    n_top = 3 * sub
    rank8 = lax.broadcasted_iota(jnp.int32, (sub, ht.shape[1]), 0)
    for h in range(PEER_HEADS):
        qh = jnp.dot(wbf[2 * kh * h:2 * kh * (h + 1), :], ht, preferred_element_type=F32)
        s1 = _dot_x3(keys_ref[h, 0], qh[:kh])
        s2 = _dot_x3(keys_ref[h, 1], qh[kh:])
        t1 = _top_rows(s1, kk + 1, n_top)
        t2 = _top_rows(s2, kk + 1, n_top)
        low, one = t1[0:sub], t2[0:sub]
        cand = jnp.concatenate([
            t1[0:1] + one,
            t1[0:1] + t2[sub:kk],
            t1[1:2] + one,
            jnp.where(rank8 < 5, t1[2:3] + one, NEG_BIG),
            jnp.where(rank8 < 4, t1[3:4] + one, low + t2[1:2]),
            jnp.where(rank8 >= 4, low + t2[0:1], NEG_BIG),
            t1[sub:kk] + t2[0:1],
            jnp.where(rank8 == 0, t1[kk:kk + 1] + t2[0:1],
                      jnp.where(rank8 == 1, t1[0:1] + t2[kk:kk + 1],
                                jnp.where(rank8 == 2, t1[4:5] + t2[2:3], NEG_BIG)))], axis=0)
        best = _top_rows(cand, kk + 1, n_top)
        z = jnp.sum(jnp.exp(best[:kk] - best[0:1]), axis=0, keepdims=True)
        thr = 0.5 * (best[kk - 1:kk] + best[kk:kk + 1]) - s1
        e1 = jnp.exp(s1 - t1[0:1])
        for rb in range(PEER_KEYS // sub):
            thr_ref[h, rb] = thr[rb * sub:(rb + 1) * sub]
            e1_ref[h, rb] = e1[rb * sub:(rb + 1) * sub]
        s2_ref[h] = s2
        e2_ref[h] = jnp.exp(s2 - t2[0:1]) * (0.5 / z)


def _peer_scores(ht, wq_t, sub_keys, layer):
    d, t = ht.shape
    tt = PEER_TOKEN_TILE
    nq = wq_t.shape[1]
    sub = PEER_ROWS_PER_BLOCK
    row_spec = pl.BlockSpec((PEER_HEADS, PEER_KEYS // sub, sub, tt), lambda i: (0, 0, 0, i))
    row_shape = jax.ShapeDtypeStruct((PEER_HEADS, PEER_KEYS // sub, sub, t), F32)
    tok_spec = pl.BlockSpec((PEER_HEADS, PEER_KEYS, tt), lambda i: (0, 0, i))
    tok_shape = jax.ShapeDtypeStruct((PEER_HEADS, PEER_KEYS, t), F32)
    return pl.pallas_call(
        _peer_score_body,
        grid=(t // tt,),
        in_specs=[pl.BlockSpec((d, tt), lambda i: (0, i)),
                  pl.BlockSpec((None, nq, d), lambda i: (layer, 0, 0)),
                  pl.BlockSpec((None, PEER_HEADS, 2, PEER_KEYS, PEER_HALF), lambda i: (layer, 0, 0, 0, 0))],
        out_specs=[row_spec, row_spec, tok_spec, tok_spec],
        out_shape=[row_shape, row_shape, tok_shape, tok_shape],
        scratch_shapes=[pltpu.VMEM((nq, d), BF16)],
        compiler_params=_params(1),
        name="peer_scores",
    )(ht, wq_t, sub_keys)


def _peer_tables_body(u_ref, v_ref, ub_ref, vt_ref):
    ub_ref[...] = u_ref[...].astype(BF16)
    vt_ref[...] = v_ref[...].T.astype(BF16)


def _peer_tables(u_tab, v_tab, layer):
    n, d = u_tab.shape[1:]
    eb = MM_COL_TILE
    return pl.pallas_call(
        _peer_tables_body,
        grid=(n // eb,),
        in_specs=[pl.BlockSpec((None, eb, d), lambda e: (layer, e, 0)),
                  pl.BlockSpec((None, eb, d), lambda e: (layer, e, 0))],
        out_specs=[pl.BlockSpec((eb, d), lambda e: (e, 0)), pl.BlockSpec((d, eb), lambda e: (0, e))],
        out_shape=[jax.ShapeDtypeStruct((n, d), BF16), jax.ShapeDtypeStruct((d, n), BF16)],
        compiler_params=_params(1),
        name="peer_tables",
    )(u_tab, v_tab)


def _peer_mix_body(ht_ref, thr_ref, e1_ref, s2_ref, e2_ref, u_ref, vt_ref, o_ref, act_s, coef_s):
    @pl.when(pl.program_id(1) == 0)
    def _():
        o_ref[...] = jnp.zeros(o_ref.shape, F32)

    lanes, kq = 128, 32
    n_q = PEER_KEYS // kq
    act_s[...] = jnp.dot(u_ref[...], ht_ref[...], preferred_element_type=F32)
    c0 = math.sqrt(2.0 / math.pi)

    def gate_chunk(ci, carry):
        part, q = ci // n_q, ci % n_q
        c = pl.ds(pl.multiple_of(part * lanes, lanes), lanes)
        keys = pl.ds(pl.multiple_of(q * kq, kq), kq)
        gates = [jnp.zeros((kq, lanes), F32)] * PEER_ROWS_PER_BLOCK
        for h in range(PEER_HEADS):
            s2 = s2_ref[h, keys, c]
            e2 = e2_ref[h, keys, c]
            for r in range(PEER_ROWS_PER_BLOCK):
                chosen = s2 >= thr_ref[h, 0, r:r + 1, c]
                gates[r] = gates[r] + jnp.where(chosen, e2, 0.0) * e1_ref[h, 0, r:r + 1, c]
        for r in range(PEER_ROWS_PER_BLOCK):
            rows = pl.ds(pl.multiple_of(r * PEER_KEYS + q * kq, kq), kq)
            a = act_s[rows, c]
            twice_gelu = a * (1.0 + jnp.tanh(a * (c0 + (c0 * 0.044715) * (a * a))))
            coef_s[rows, c] = (twice_gelu * gates[r]).astype(BF16)
        return carry

    lax.fori_loop(0, (o_ref.shape[1] // lanes) * n_q, gate_chunk, 0)
    o_ref[...] = o_ref[...] + jnp.dot(vt_ref[...], coef_s[...], preferred_element_type=F32)


def _peer_mix(ht, thr, e1, s2, e2, u_bf, vt_bf):
    d, t = ht.shape
    ts, rb = PEER_SUPER_TILE, PEER_ROWS_PER_BLOCK
    eb = rb * PEER_KEYS
    n_blocks = u_bf.shape[0] // eb
    row_spec = pl.BlockSpec((PEER_HEADS, 1, rb, ts), lambda s, e: (0, e, 0, s))
    tok_spec = pl.BlockSpec((PEER_HEADS, PEER_KEYS, ts), lambda s, e: (0, 0, s))
    return pl.pallas_call(
        _peer_mix_body,
        grid=(t // ts, n_blocks),
        in_specs=[pl.BlockSpec((d, ts), lambda s, e: (0, s)), row_spec, row_spec, tok_spec, tok_spec,
                  pl.BlockSpec((eb, d), lambda s, e: (e, 0)), pl.BlockSpec((d, eb), lambda s, e: (0, e))],
        out_specs=pl.BlockSpec((d, ts), lambda s, e: (0, s)),
        out_shape=jax.ShapeDtypeStruct((d, t), F32),
        scratch_shapes=[pltpu.VMEM((eb, ts), F32), pltpu.VMEM((eb, ts), BF16)],
        compiler_params=_params(2),
        name="peer_mix",
    )(ht, thr, e1, s2, e2, u_bf, vt_bf)


def kernel(x_prompt, x_sample, cache_k, cache_v, state_ssm, c, c_ctx, w_mod, b_mod, ln_g, ln_b, attn_w_qkv, attn_w_o, attn_lambda, attn_subln_g, ssm_w_in, ssm_conv_w, ssm_conv_b, ssm_dt_bias, ssm_a_log, ssm_d, ssm_norm_g, ssm_w_out, gmlp_w_in, gmlp_b_in, gmlp_ln_g, gmlp_ln_b, gmlp_w_s, gmlp_b_s, gmlp_w_out, peer_w_q, peer_sub_keys, peer_u, peer_v):
    bc, lc, d = x_prompt.shape
    bl, ll, _ = x_sample.shape
    depth = w_mod.shape[0]
    t_ctx, t_lat = bc * lc, bl * ll
    alpha = (2 * depth) ** 0.25
    assert 1 + bl <= N_MOD_GROUPS and t_ctx % ll == 0

    x = (x_prompt.reshape(t_ctx, d), x_sample.reshape(t_lat, d))
    cond = jnp.zeros((N_MOD_GROUPS, d), F32).at[0].set(c_ctx).at[1:1 + bl].set(c)
    mods = _modulation(cond, w_mod, b_mod).reshape(depth, N_MOD_GROUPS, 6, d)
    rope = _rope_tables(ll, 2 * ATTN_HEAD_DIM)
    past = cache_k.shape[2]
    ck = cache_k.reshape(bl, cache_k.shape[1], past, -1)
    cv = cache_v.reshape(bl, cache_v.shape[1], past, -1)
    wq_t = jnp.swapaxes(peer_w_q, 1, 2)

    new_k, new_v, new_s = [], [], []
    h = _modulate(x, mods[0], t_ctx, ll, shift=0, scale=1)
    for i in range(depth):
        kind, j = i % 3, i // 3
        if kind == 0:
            lam_init = 0.8 - 0.6 * math.exp(-0.3 * i)
            qkv, k_ctx, v_ctx = _qkv_rope(h, attn_w_qkv, j, rope, t_ctx, ll)
            o_ctx = _diff_attention(qkv, attn_lambda, attn_subln_g, j, lam_init, 0, bc, lc)
            o_lat = _diff_attention(qkv, attn_lambda, attn_subln_g, j, lam_init, t_ctx, bl, ll,
                                    cache=(ck, cv))
            o = _linear((o_ctx, o_lat), attn_w_o, j, d)
            new_k.append(k_ctx.reshape(bc, lc, ATTN_HEADS, 2, ATTN_HEAD_DIM))
            new_v.append(v_ctx.reshape(bc, lc, ATTN_HEADS, 2 * ATTN_HEAD_DIM))
        elif kind == 1:
            n_main = SSM_INNER + SSM_INNER + 2 * SSM_GROUPS * SSM_STATE
            zxbc = _linear(h, ssm_w_in, j, n_main)
            def regroup(a):
                a = a.reshape(a.shape[:-1] + (2, SSM_GROUPS, SSM_HPG))
                a = jnp.moveaxis(a, -3, -2).reshape(a.shape[:-3] + (SSM_GROUPS, 2 * SSM_HPG))
                pad = [(0, 0)] * (a.ndim - 1) + [(0, 128 - 2 * SSM_HPG)]
                return jnp.pad(a, pad).reshape(a.shape[:-2] + (SSM_GROUPS * 128,))
            w_dt = regroup(ssm_w_in[j][:, n_main:])[None]
            dtp = _linear(h, w_dt, 0, SSM_GROUPS * 128)
            dtb = regroup(ssm_dt_bias[j].reshape(1, -1))
            a_l = regroup(-jnp.exp(ssm_a_log[j]).reshape(1, -1))
            d_l = jnp.repeat(ssm_d[j], SSM_HEAD_DIM).reshape(1, SSM_INNER)
            cw, cb = ssm_conv_w[j], ssm_conv_b[j].reshape(1, -1)
            y_ctx, st = _ssd(zxbc, dtp, cw, cb, dtb, a_l, d_l, 0, bc, lc, emit_state=True)
            (y_lat,) = _ssd(zxbc, dtp, cw, cb, dtb, a_l, d_l, t_ctx, bl, ll, h0=state_ssm, layer=j)
            o = _ssm_out((y_ctx, y_lat), zxbc, ssm_norm_g, ssm_w_out, j)
            new_s.append(st)
        else:
            uv = _linear(h, gmlp_w_in, j, 2 * GMLP_HALF, bias=gmlp_b_in[j], act="gelu")
            o = _gmlp_mix_out(uv, gmlp_ln_g, gmlp_ln_b, gmlp_w_s, gmlp_b_s, gmlp_w_out, j)
        x, ht = _res_ln(x, o, mods[i], mods[i], ln_g[i, 0], ln_b[i, 0], t_ctx, ll, alpha=alpha,
                        gate=2, nshift=3, nscale=4, emit_ht=True)
        thr, e1, s2, e2 = _peer_scores(ht, wq_t, peer_sub_keys, i)
        u_bf, vt_bf = _peer_tables(peer_u, peer_v, i)
        f_t = _peer_mix(ht, thr, e1, s2, e2, u_bf, vt_bf)
        if i + 1 < depth:
            x, h = _res_ln(x, f_t, mods[i], mods[i + 1], ln_g[i, 1], ln_b[i, 1], t_ctx, ll,
                           alpha=alpha, gate=5, nshift=0, nscale=1, o_transposed=True, emit_h=True)
        else:
            out_ctx, out_lat = _res_ln(x, f_t, mods[i], mods[i], ln_g[i, 1], ln_b[i, 1], t_ctx, ll,
                                       alpha=alpha, gate=5, o_transposed=True, split_out=True)
    return (out_ctx.reshape(bc, lc, d), out_lat.reshape(bl, ll, d),
            jnp.stack(new_k, axis=1), jnp.stack(new_v, axis=1), jnp.stack(new_s, axis=1))
```

```python
import functools
import math

import jax
import jax.numpy as jnp
from jax import lax
from jax.experimental import pallas as pl
from jax.experimental.pallas import tpu as pltpu

F32 = jnp.float32
BF16 = jnp.bfloat16

D_MODEL = 1024
LN_EPS = 1e-5
GRID_W = 64
ATTN_HEADS = 8
ATTN_HEAD_DIM = 64
ROPE_AXIS_DIM = ATTN_HEAD_DIM // 2
ROPE_BASE = 10000.0
SSM_INNER = 2 * D_MODEL
SSM_HEAD_DIM = 64
SSM_GROUPS = 8
SSM_HPG = 4
SSM_STATE = 128
SSM_CONV = 5
SSD_CHUNK = 128
GMLP_HALF = 2 * D_MODEL
GMLP_GROUPS = 8
GMLP_GROUP_DIM = GMLP_HALF // GMLP_GROUPS
GMLP_CHUNK = 128
PEER_HEADS = 8
PEER_KEYS = 128
PEER_HALF = 128
PEER_TOPK = 16
N_MOD_GROUPS = 8
NEG_BIG = -1e30

VMEM_LIMIT = 56 * 1024 * 1024
ROW_TILE = 512
SSM_OUT_ROW_TILE = 256
MM_ROW_TILE = 1024
MM_COL_TILE = 1024
PEER_TOKEN_TILE = 512
PEER_SUPER_TILE = 1024
PEER_ROWS_PER_BLOCK = 8


def _params(n_axes):
    return pltpu.CompilerParams(dimension_semantics=("arbitrary",) * n_axes,
                                vmem_limit_bytes=VMEM_LIMIT)


def _bdot(a, b):
    return jnp.dot(a.astype(BF16), b.astype(BF16), preferred_element_type=F32)


def _bdot_nt(a, b):
    return lax.dot_general(a.astype(BF16), b.astype(BF16), (((1,), (1,)), ((), ())),
                           preferred_element_type=F32)


def _split3(a):
    hi = a.astype(BF16)
    r = a - hi.astype(F32)
    mid = r.astype(BF16)
    lo = (r - mid.astype(F32)).astype(BF16)
    return hi, mid, lo


def _dot_x3(a, b):
    ah, am, _ = _split3(a)
    bh, bm, _ = _split3(b)
    d = functools.partial(jnp.dot, preferred_element_type=F32)
    return d(ah, bh) + (d(am, bh) + d(ah, bm))


def _silu(x):
    return x * jax.nn.sigmoid(x)


def _gelu(x):
    return 0.5 * x * (1.0 + jnp.tanh(math.sqrt(2.0 / math.pi) * (x + 0.044715 * (x * x * x))))


def _group_index(i, tm, t_ctx, l_lat):
    start = i * tm
    return jnp.where(start < t_ctx, 0, 1 + (start - t_ctx) // l_lat)


def _mod_body(c_ref, w_ref, b_ref, o_ref):
    o_ref[0] = _dot_x3(_silu(c_ref[...]), w_ref[0]) + b_ref[0]


def _modulation(cond, w_mod, b_mod):
    depth, d, n = w_mod.shape
    tn = 1536
    return pl.pallas_call(
        _mod_body,
        grid=(depth, n // tn),
        in_specs=[pl.BlockSpec((N_MOD_GROUPS, d), lambda l, j: (0, 0)),
                  pl.BlockSpec((1, d, tn), lambda l, j: (l, 0, j)),
                  pl.BlockSpec((1, 1, tn), lambda l, j: (l, 0, j))],
        out_specs=pl.BlockSpec((1, N_MOD_GROUPS, tn), lambda l, j: (l, 0, j)),
        out_shape=jax.ShapeDtypeStruct((depth, N_MOD_GROUPS, n), F32),
        compiler_params=_params(2),
        name="modulation",
    )(cond, w_mod, b_mod.reshape(depth, 1, n))


def _modulate_body(xa_ref, xb_ref, m_ref, h_ref, *, shift, scale, first_tiles):
    def modulate(x_ref):
        m = m_ref[0]
        h_ref[...] = (x_ref[...] * (1.0 + m[scale:scale + 1]) + m[shift:shift + 1]).astype(BF16)

    i = pl.program_id(0)
    pl.when(i < first_tiles)(lambda: modulate(xa_ref))
    pl.when(i >= first_tiles)(lambda: modulate(xb_ref))


def _modulate(x_parts, mod, t_ctx, l_lat, shift, scale):
    t, d = x_parts[0].shape[0] + x_parts[1].shape[0], x_parts[0].shape[1]
    tm = ROW_TILE
    x_specs, first_tiles = _two_part_specs(x_parts, tm, 1)
    return pl.pallas_call(
        functools.partial(_modulate_body, shift=shift, scale=scale, first_tiles=first_tiles),
        grid=(t // tm,),
        in_specs=x_specs + [pl.BlockSpec((1, 6, d), lambda i: (_group_index(i, tm, t_ctx, l_lat), 0, 0))],
        out_specs=pl.BlockSpec((tm, d), lambda i: (i, 0)),
        out_shape=jax.ShapeDtypeStruct((t, d), BF16),
        compiler_params=_params(1),
        name="modulate",
    )(*x_parts, mod)


def _res_ln_body(*refs, alpha, gate, nshift, nscale, o_transposed, emit_h, emit_ht,
                 x_first_tiles, out_first_tiles):
    n_x = 1 if x_first_tiles is None else 2
    x_refs, (o_ref, m_ref, mn_ref, g_ref, b_ref), outs = refs[:n_x], refs[n_x:n_x + 5], refs[n_x + 5:]
    n_xo = 1 if out_first_tiles is None else 2
    i = pl.program_id(0)

    def residual_norm(x_ref):
        o = o_ref[...]
        if o_transposed:
            o = o.T
        m = m_ref[0]
        y = alpha * x_ref[...] + m[gate:gate + 1] * o
        mu = jnp.mean(y, axis=-1, keepdims=True)
        yc = y - mu
        var = jnp.mean(yc * yc, axis=-1, keepdims=True)
        xn = yc * lax.rsqrt(var + LN_EPS) * g_ref[...] + b_ref[...]
        if out_first_tiles is None:
            outs[0][...] = xn
        else:
            @pl.when(i < out_first_tiles)
            def _():
                outs[0][...] = xn

            @pl.when(i >= out_first_tiles)
            def _():
                outs[1][...] = xn
        if emit_h or emit_ht:
            mn = mn_ref[0]
            h = xn * (1.0 + mn[nscale:nscale + 1]) + mn[nshift:nshift + 1]
            k = n_xo
            if emit_h:
                outs[k][...] = h.astype(BF16)
                k += 1
            if emit_ht:
                outs[k][...] = h.T.astype(BF16)

    if x_first_tiles is None:
        residual_norm(x_refs[0])
    else:
        pl.when(i < x_first_tiles)(lambda: residual_norm(x_refs[0]))
        pl.when(i >= x_first_tiles)(lambda: residual_norm(x_refs[1]))


def _res_ln(x, o, mod, mod_next, ln_g, ln_b, t_ctx, l_lat, *, alpha, gate, nshift=0, nscale=1,
            o_transposed=False, emit_h=False, emit_ht=False, split_out=False):
    tm = ROW_TILE
    if isinstance(x, tuple):
        t, d = x[0].shape[0] + x[1].shape[0], x[0].shape[1]
        x_specs, x_first_tiles = _two_part_specs(x, tm, 1)
        x_args = list(x)
    else:
        t, d = x.shape
        x_specs, x_first_tiles, x_args = [pl.BlockSpec((tm, d), lambda i: (i, 0))], None, [x]
    grp = lambda i: (_group_index(i, tm, t_ctx, l_lat), 0, 0)
    o_spec = (pl.BlockSpec((d, tm), lambda i: (0, i)) if o_transposed
              else pl.BlockSpec((tm, d), lambda i: (i, 0)))
    if split_out:
        out_first_tiles = t_ctx // tm
        out_specs = [pl.BlockSpec((tm, d), lambda i: (jnp.minimum(i, out_first_tiles - 1), 0)),
                     pl.BlockSpec((tm, d), lambda i: (jnp.maximum(i - out_first_tiles, 0), 0))]
        out_shape = [jax.ShapeDtypeStruct((t_ctx, d), F32), jax.ShapeDtypeStruct((t - t_ctx, d), F32)]
    else:
        out_first_tiles = None
        out_specs = [pl.BlockSpec((tm, d), lambda i: (i, 0))]
        out_shape = [jax.ShapeDtypeStruct((t, d), F32)]
    if emit_h:
        out_specs.append(pl.BlockSpec((tm, d), lambda i: (i, 0)))
        out_shape.append(jax.ShapeDtypeStruct((t, d), BF16))
    if emit_ht:
        out_specs.append(pl.BlockSpec((d, tm), lambda i: (0, i)))
        out_shape.append(jax.ShapeDtypeStruct((d, t), BF16))
    return pl.pallas_call(
        functools.partial(_res_ln_body, alpha=alpha, gate=gate, nshift=nshift, nscale=nscale,
                          o_transposed=o_transposed, emit_h=emit_h, emit_ht=emit_ht,
                          x_first_tiles=x_first_tiles, out_first_tiles=out_first_tiles),
        grid=(t // tm,),
        in_specs=x_specs + [o_spec, pl.BlockSpec((1, 6, d), grp), pl.BlockSpec((1, 6, d), grp),
                            pl.BlockSpec((1, d), lambda i: (0, 0)), pl.BlockSpec((1, d), lambda i: (0, 0))],
        out_specs=out_specs,
        out_shape=out_shape,
        compiler_params=_params(1),
        name="residual_layernorm",
    )(*x_args, o, mod, mod_next, ln_g.reshape(1, d), ln_b.reshape(1, d))


def _two_part_specs(parts, tm, grid_rank):
    first, second = parts
    n_first = first.shape[0] // tm
    row = lambda idx: idx[-1] if grid_rank == 2 else idx[0]
    return ([pl.BlockSpec((tm, first.shape[1]), lambda *idx: (jnp.minimum(row(idx), n_first - 1), 0)),
             pl.BlockSpec((tm, second.shape[1]), lambda *idx: (jnp.maximum(row(idx) - n_first, 0), 0))],
            n_first)


def _linear_body(*refs, has_bias, act, first_tiles):
    n_src = 1 if first_tiles is None else 2
    srcs, w_ref, refs = refs[:n_src], refs[n_src], refs[n_src + 1:]
    if has_bias:
        b_ref, o_ref, wbf = refs
    else:
        o_ref, wbf = refs
    i = pl.program_id(1)

    @pl.when(i == 0)
    def _():
        wbf[...] = w_ref[...].astype(BF16)

    def project(h_ref):
        y = jnp.dot(h_ref[...], wbf[...], preferred_element_type=F32)
        if has_bias:
            y = y + b_ref[...]
        if act == "gelu":
            y = _gelu(y)
        o_ref[...] = y

    if first_tiles is None:
        project(srcs[0])
    else:
        pl.when(i < first_tiles)(lambda: project(srcs[0]))
        pl.when(i >= first_tiles)(lambda: project(srcs[1]))


def _linear(h, w_stack, layer, n_out, bias=None, act=None):
    tm, tn = MM_ROW_TILE, min(MM_COL_TILE, n_out)
    if isinstance(h, tuple):
        t, k = h[0].shape[0] + h[1].shape[0], h[0].shape[1]
        in_specs, first_tiles = _two_part_specs(h, tm, 2)
        args = list(h)
    else:
        t, k = h.shape
        in_specs, first_tiles = [pl.BlockSpec((tm, k), lambda n, i: (i, 0))], None
        args = [h]
    in_specs.append(pl.BlockSpec((None, k, tn), lambda n, i: (layer, 0, n)))
    args.append(w_stack)
    if bias is not None:
        in_specs.append(pl.BlockSpec((1, tn), lambda n, i: (0, n)))
        args.append(bias.reshape(1, n_out))
    return pl.pallas_call(
        functools.partial(_linear_body, has_bias=bias is not None, act=act, first_tiles=first_tiles),
        grid=(n_out // tn, t // tm),
        in_specs=in_specs,
        out_specs=pl.BlockSpec((tm, tn), lambda n, i: (i, n)),
        out_shape=jax.ShapeDtypeStruct((t, n_out), F32),
        scratch_shapes=[pltpu.VMEM((k, tn), BF16)],
        compiler_params=_params(2),
        name="linear",
    )(*args)


def _qkv_body(h_ref, w_ref, cos_ref, sa_ref, sb_ref, o_ref, kc_ref, vc_ref, wbf, *, n_ctx_tiles):
    n, i = pl.program_id(0), pl.program_id(1)

    @pl.when(i == 0)
    def _():
        wbf[...] = w_ref[...].astype(BF16)

    y = jnp.dot(h_ref[...], wbf[...], preferred_element_type=F32)
    rope = jnp.logical_and(n < 2, i >= n_ctx_tiles)
    half = ROPE_AXIS_DIM // 2

    @pl.when(jnp.logical_and(n == 1, i < n_ctx_tiles))
    def _():
        kc_ref[...] = y

    @pl.when(jnp.logical_and(n == 2, i < n_ctx_tiles))
    def _():
        vc_ref[...] = y

    @pl.when(rope)
    def _():
        w = cos_ref.shape[1]
        for c0 in range(0, y.shape[1], w):
            yg = y[:, c0:c0 + w]
            up = pltpu.roll(yg, w - half, 1)
            dn = pltpu.roll(yg, half, 1)
            o_ref[:, c0:c0 + w] = yg * cos_ref[...] + up * sa_ref[...] + dn * sb_ref[...]

    @pl.when(jnp.logical_not(rope))
    def _():
        o_ref[...] = y


def _qkv_rope(h, w_stack, layer, tables, t_ctx, l_lat):
    t, k = h.shape
    n_out = w_stack.shape[2]
    tm, tn = MM_ROW_TILE, MM_COL_TILE
    assert n_out == 3 * tn
    n_ctx_tiles = t_ctx // tm
    pos = lambda n, i: (jnp.maximum(i - n_ctx_tiles, 0) % (l_lat // tm), 0)
    last = n_ctx_tiles - 1
    k_rows = lambda n, i: (jnp.where(n == 1, jnp.minimum(i, last), jnp.where(n < 1, 0, last)), 0)
    v_rows = lambda n, i: (jnp.where(n == 2, jnp.minimum(i, last), 0), 0)
    ctx_shape = jax.ShapeDtypeStruct((t_ctx, tn), F32)
    return pl.pallas_call(
        functools.partial(_qkv_body, n_ctx_tiles=n_ctx_tiles),
        grid=(n_out // tn, t // tm),
        in_specs=[pl.BlockSpec((tm, k), lambda n, i: (i, 0)),
                  pl.BlockSpec((None, k, tn), lambda n, i: (layer, 0, n)),
                  *[pl.BlockSpec((tm, tab.shape[1]), pos) for tab in tables]],
        out_specs=[pl.BlockSpec((tm, tn), lambda n, i: (i, n)),
                   pl.BlockSpec((tm, tn), k_rows), pl.BlockSpec((tm, tn), v_rows)],
        out_shape=[jax.ShapeDtypeStruct((t, n_out), F32), ctx_shape, ctx_shape],
        scratch_shapes=[pltpu.VMEM((k, tn), BF16)],
        compiler_params=_params(2),
        name="qkv_rope",
    )(h, w_stack, *tables)


def _rope_tables(l_lat, width):
    rows = l_lat // GRID_W
    row_pos = jnp.repeat(jnp.arange(rows, dtype=F32), GRID_W)
    col_pos = jnp.tile(jnp.arange(GRID_W, dtype=F32), rows)
    inv_freq = ROPE_BASE ** (-jnp.arange(0, ROPE_AXIS_DIM, 2, dtype=F32) / ROPE_AXIS_DIM)

    def tab(p):
        ang = p[:, None] * inv_freq[None, :]
        ang = jnp.concatenate([ang, ang], -1)
        return jnp.cos(ang), jnp.sin(ang)

    cr, sr = tab(row_pos)
    cc, sc = tab(col_pos)
    cos = jnp.concatenate([cr, cc], -1)
    sin = jnp.concatenate([sr, sc], -1)
    first_half = (jnp.arange(ATTN_HEAD_DIM) % ROPE_AXIS_DIM) < (ROPE_AXIS_DIM // 2)
    sin_a = jnp.where(first_half, -sin, 0.0)
    sin_b = jnp.where(first_half, 0.0, sin)
    rep = width // ATTN_HEAD_DIM
    return tuple(jnp.tile(a, (1, rep)) for a in (cos, sin_a, sin_b))


def _attn_body(lam_ref, g_ref, q_ref, k_ref, v_ref, *refs, lam_init, past, tq):
    if past:
        kc_ref, vc_ref, o_ref, k_s, v_s = refs
        k_s[:past] = kc_ref[...].astype(BF16)
        v_s[:past] = vc_ref[...].astype(BF16)
    else:
        o_ref, k_s, v_s = refs
    k_s[past:] = k_ref[...].astype(BF16)
    v_s[past:] = v_ref[...].astype(BF16)
    lp = lam_ref[...]
    lam = (jnp.exp(jnp.sum(lp[0:1] * lp[1:2], axis=-1, keepdims=True))
           - jnp.exp(jnp.sum(lp[2:3] * lp[3:4], axis=-1, keepdims=True)) + lam_init)
    dh = ATTN_HEAD_DIM
    scale = dh ** -0.5

    def q_block(i, carry):
        rows = pl.ds(pl.multiple_of(i * tq, tq), tq)
        for c0 in range(0, q_ref.shape[1], 2 * dh):
            q = (q_ref[rows, c0:c0 + 2 * dh] * scale).astype(BF16)

            def softmax_map(m, q=q, c0=c0):
                s = _bdot_nt(q[:, m * dh:(m + 1) * dh], k_s[:, c0 + m * dh:c0 + (m + 1) * dh])
                e = jnp.exp(s - jnp.max(s, axis=-1, keepdims=True))
                return e / jnp.sum(e, axis=-1, keepdims=True)

            a = softmax_map(0) - lam * softmax_map(1)
            o = _bdot(a, v_s[:, c0:c0 + 2 * dh])
            o = o * lax.rsqrt(jnp.mean(o * o, axis=-1, keepdims=True) + LN_EPS) * g_ref[...]
            o_ref[rows, c0:c0 + 2 * dh] = (o * (1.0 - lam_init)).astype(o_ref.dtype)
        return carry

    lax.fori_loop(0, q_ref.shape[0] // tq, q_block, 0)


def _diff_attention(qkv, lam_p, subln_g, layer, lam_init, row0, n_batch, seq, cache=None):
    head_w = 2 * ATTN_HEAD_DIM
    hps = 1 if cache is not None else 4
    width, groups = hps * head_w, ATTN_HEADS // hps
    base = row0 // seq
    in_specs = [pl.BlockSpec((None, 4, ATTN_HEAD_DIM), lambda b, h: (layer, 0, 0)),
                pl.BlockSpec((None, 1, head_w), lambda b, h: (layer, 0, 0)),
                pl.BlockSpec((seq, width), lambda b, h: (base + b, h)),
                pl.BlockSpec((seq, width), lambda b, h: (base + b, groups + h)),
                pl.BlockSpec((seq, width), lambda b, h: (base + b, 2 * groups + h))]
    args = [lam_p, subln_g.reshape(subln_g.shape[0], 1, head_w), qkv, qkv, qkv]
    past = 0
    if cache is not None:
        ck, cv = cache
        past = ck.shape[2]
        spec = pl.BlockSpec((None, None, past, width), lambda b, h: (b, layer, 0, h))
        in_specs += [spec, spec]
        args += [ck, cv]
    return pl.pallas_call(
        functools.partial(_attn_body, lam_init=lam_init, past=past, tq=min(seq, 256)),
        grid=(n_batch, groups),
        in_specs=in_specs,
        out_specs=pl.BlockSpec((seq, width), lambda b, h: (b, h)),
        out_shape=jax.ShapeDtypeStruct((n_batch * seq, ATTN_HEADS * head_w), BF16),
        scratch_shapes=[pltpu.VMEM((past + seq, width), BF16)] * 2,
        compiler_params=_params(2),
        name="diff_attention",
    )(*args)


def _conv_silu(x, w, b):
    n = x.shape[0]
    row = lax.broadcasted_iota(jnp.int32, x.shape, 0)
    acc = x * w[SSM_CONV // 2:SSM_CONV // 2 + 1] + b
    for k in range(SSM_CONV):
        sh = k - SSM_CONV // 2
        if sh == 0:
            continue
        shifted = pltpu.roll(x, (-sh) % n, 0)
        valid = jnp.logical_and(row + sh >= 0, row + sh < n)
        acc = acc + jnp.where(valid, shifted, 0.0) * w[k:k + 1]
    return _silu(acc)


def _softplus(x):
    return jnp.maximum(x, 0.0) + jnp.log1p(jnp.exp(-jnp.abs(x)))


def _ssd_body(*refs, has_h0, emit_state, n_chunks):
    (xs_ref, bm_ref, cm_ref, dt_ref, wx_ref, bx_ref, wb_ref, bb_ref, wc_ref, bc_ref,
     dtb_ref, a_ref, d_ref) = refs[:13]
    refs = refs[13:]
    if has_h0:
        h0_ref, refs = refs[0], refs[1:]
    y_ref, refs = refs[0], refs[1:]
    if emit_state:
        st_ref, refs = refs[0], refs[1:]
    xs_s, b_s, c_s, dt_s, da_s, xt_s, dtt_s, dat_s, h_s = refs
    q = SSD_CHUNK
    p = SSM_HEAD_DIM
    dot = functools.partial(jnp.dot, preferred_element_type=F32)

    xs = _conv_silu(xs_ref[...], wx_ref[...], bx_ref[...])
    xs_s[...] = xs
    y_ref[...] = d_ref[...] * xs
    b_s[...] = _conv_silu(bm_ref[...], wb_ref[...], bb_ref[...])
    c_s[...] = _conv_silu(cm_ref[...], wc_ref[...], bc_ref[...])
    dt = _softplus(dt_ref[...] + dtb_ref[...])
    dt_s[...] = dt
    da_s[...] = dt * a_ref[...]

    def transposes(c, carry):
        rows = pl.ds(pl.multiple_of(c * q, q), q)
        xt_s[c] = xs_s[rows, :].T
        dtt_s[c] = dt_s[rows, :].T
        dat_s[c] = da_s[rows, :].T
        return carry

    lax.fori_loop(0, n_chunks, transposes, 0)

    row = lax.broadcasted_iota(jnp.int32, (q, q), 0)
    col = lax.broadcasted_iota(jnp.int32, (q, q), 1)
    lower, upper = row >= col, row <= col
    ones = [jnp.where(m, 1.0, 0.0).astype(BF16) for m in (lower, upper)]
    for direction in range(2):
        for r in range(SSM_HPG):
            if has_h0:
                h_s[direction, r] = h0_ref[direction, r]
            else:
                h_s[direction, r] = jnp.zeros((p, SSM_STATE), F32)

    def scan_chunk(c, direction):
        mask = lower if direction == 0 else upper
        end = q - 1 if direction == 0 else 0
        rows = pl.ds(pl.multiple_of(c * q, q), q)
        xc = xs_s[rows, :]
        bc = b_s[rows, :].astype(BF16)
        cc = c_s[rows, :].astype(BF16)
        dtc = dt_s[rows, :]
        dt_t = dtt_s[c]
        x_t = xt_s[c]
        d_hi, d_mid, d_lo = _split3(da_s[rows, :])
        tri = ones[direction]
        acum = dot(tri, d_hi) + (dot(tri, d_mid) + dot(tri, d_lo))
        t_hi, t_mid, t_lo = _split3(dat_s[c])
        tri_t = ones[1 - direction]
        acum_t = dot(t_hi, tri_t) + (dot(t_mid, tri_t) + dot(t_lo, tri_t))
        cb = _bdot_nt(cc, bc)
        a_end = acum[end:end + 1, :]
        ys = []
        for r in range(SSM_HPG):
            hl = direction * SSM_HPG + r
            a_col = acum[:, hl:hl + 1]
            a_row = acum_t[hl:hl + 1, :]
            decay = jnp.exp(jnp.where(mask, a_col - a_row, NEG_BIG))
            xdt = xc[:, r * p:(r + 1) * p] * dtc[:, hl:hl + 1]
            h = h_s[direction, r]
            y = _bdot(cb * decay, xdt) + jnp.exp(a_col) * _bdot_nt(cc, h)
            w = dt_t[hl:hl + 1, :] * jnp.exp(a_end[:, hl:hl + 1] - a_row)
            h_s[direction, r] = (jnp.exp(a_end[:, hl:hl + 1]) * h
                                 + _bdot(x_t[r * p:(r + 1) * p, :] * w, bc))
            ys.append(y)
        return rows, jnp.concatenate(ys, axis=1)

    def step(ci, carry):
        rows_f, y_f = scan_chunk(ci, 0)
        rows_b, y_b = scan_chunk(n_chunks - 1 - ci, 1)
        y_ref[rows_f, :] = y_ref[rows_f, :] + y_f
        y_ref[rows_b, :] = y_ref[rows_b, :] + y_b
        return carry

    lax.fori_loop(0, n_chunks, step, 0)
    if emit_state:
        for direction in range(2):
            for r in range(SSM_HPG):
                st_ref[direction, r] = h_s[direction, r]


def _ssd(zxbc, dtp, conv_w, conv_b, dt_bias_l, a_l, d_l, row0, n_batch, seq, h0=None, layer=0,
         emit_state=False):
    inner, gs, hp = SSM_INNER, SSM_GROUPS * SSM_STATE, SSM_HPG * SSM_HEAD_DIM
    base = row0 // seq
    g_of = lambda off, w: (lambda b, g: (0, off // w + g))
    row_blk = lambda off, w: (lambda b, g: (base + b, off // w + g))
    in_specs = [pl.BlockSpec((seq, hp), row_blk(inner, hp)),
                pl.BlockSpec((seq, SSM_STATE), row_blk(2 * inner, SSM_STATE)),
                pl.BlockSpec((seq, SSM_STATE), row_blk(2 * inner + gs, SSM_STATE)),
                pl.BlockSpec((seq, 128), row_blk(0, 128)),
                pl.BlockSpec((SSM_CONV, hp), g_of(0, hp)), pl.BlockSpec((1, hp), g_of(0, hp)),
                pl.BlockSpec((SSM_CONV, SSM_STATE), g_of(inner, SSM_STATE)),
                pl.BlockSpec((1, SSM_STATE), g_of(inner, SSM_STATE)),
                pl.BlockSpec((SSM_CONV, SSM_STATE), g_of(inner + gs, SSM_STATE)),
                pl.BlockSpec((1, SSM_STATE), g_of(inner + gs, SSM_STATE)),
                pl.BlockSpec((1, 128), g_of(0, 128)), pl.BlockSpec((1, 128), g_of(0, 128)),
                pl.BlockSpec((1, hp), g_of(0, hp))]
    args = [zxbc, zxbc, zxbc, dtp, conv_w, conv_b, conv_w, conv_b, conv_w, conv_b, dt_bias_l, a_l, d_l]
    if h0 is not None:
        in_specs.append(pl.BlockSpec((None, None, 2, SSM_HPG, SSM_HEAD_DIM, SSM_STATE),
                                     lambda b, g: (b, layer, 0, g, 0, 0)))
        args.append(h0)
    out_specs = [pl.BlockSpec((seq, hp), lambda b, g: (b, g))]
    out_shape = [jax.ShapeDtypeStruct((n_batch * seq, inner), F32)]
    if emit_state:
        out_specs.append(pl.BlockSpec((None, 2, SSM_HPG, SSM_HEAD_DIM, SSM_STATE),
                                      lambda b, g: (b, 0, g, 0, 0)))
        out_shape.append(jax.ShapeDtypeStruct(
            (n_batch, 2, SSM_GROUPS * SSM_HPG, SSM_HEAD_DIM, SSM_STATE), F32))
    return pl.pallas_call(
        functools.partial(_ssd_body, has_h0=h0 is not None, emit_state=emit_state,
                          n_chunks=seq // SSD_CHUNK),
        grid=(n_batch, SSM_GROUPS),
        in_specs=in_specs,
        out_specs=out_specs,
        out_shape=out_shape,
        scratch_shapes=[pltpu.VMEM((seq, hp), F32), pltpu.VMEM((seq, SSM_STATE), F32),
                        pltpu.VMEM((seq, SSM_STATE), F32), pltpu.VMEM((seq, 128), F32),
                        pltpu.VMEM((seq, 128), F32),
                        pltpu.VMEM((seq // SSD_CHUNK, hp, SSD_CHUNK), F32),
                        pltpu.VMEM((seq // SSD_CHUNK, 128, SSD_CHUNK), F32),
                        pltpu.VMEM((seq // SSD_CHUNK, 128, SSD_CHUNK), F32),
                        pltpu.VMEM((2, SSM_HPG, SSM_HEAD_DIM, SSM_STATE), F32)],
        compiler_params=_params(2),
        name="ssd_scan",
    )(*args)


def _ssm_out_body(ya_ref, yb_ref, z_ref, g_ref, w_ref, o_ref, wbf, *, first_tiles):
    i = pl.program_id(0)

    @pl.when(i == 0)
    def _():
        wbf[...] = w_ref[...].astype(BF16)

    def gate_norm_project(y_ref):
        y = y_ref[...] * _silu(z_ref[...])
        y = y * lax.rsqrt(jnp.mean(y * y, axis=-1, keepdims=True) + LN_EPS) * g_ref[...]
        o_ref[...] = jnp.dot(y.astype(BF16), wbf[...], preferred_element_type=F32)

    pl.when(i < first_tiles)(lambda: gate_norm_project(ya_ref))
    pl.when(i >= first_tiles)(lambda: gate_norm_project(yb_ref))


def _ssm_out(y_parts, zxbc, norm_g, w_out, layer):
    t, inner = y_parts[0].shape[0] + y_parts[1].shape[0], y_parts[0].shape[1]
    d = w_out.shape[2]
    tm = SSM_OUT_ROW_TILE
    y_specs, first_tiles = _two_part_specs(y_parts, tm, 1)
    return pl.pallas_call(
        functools.partial(_ssm_out_body, first_tiles=first_tiles),
        grid=(t // tm,),
        in_specs=y_specs + [pl.BlockSpec((tm, inner), lambda i: (i, 0)),
                            pl.BlockSpec((None, 1, inner), lambda i: (layer, 0, 0)),
                            pl.BlockSpec((None, inner, d), lambda i: (layer, 0, 0))],
        out_specs=pl.BlockSpec((tm, d), lambda i: (i, 0)),
        out_shape=jax.ShapeDtypeStruct((t, d), F32),
        scratch_shapes=[pltpu.VMEM((inner, d), BF16)],
        compiler_params=_params(1),
        name="ssm_gate_norm_out",
    )(*y_parts, zxbc, norm_g.reshape(norm_g.shape[0], 1, inner), w_out)


def _gmlp_body(u_ref, v_ref, g_ref, b_ref, ws_ref, bs_ref, w_ref, o_ref, wbf, t_s, *, n_chunks):
    @pl.when(pl.program_id(0) == 0)
    def _():
        wbf[...] = w_ref[...].astype(BF16)

    v = v_ref[...]
    mu = jnp.mean(v, axis=-1, keepdims=True)
    vc = v - mu
    var = jnp.mean(vc * vc, axis=-1, keepdims=True)
    vn = (vc * lax.rsqrt(var + LN_EPS) * g_ref[...] + b_ref[...]).astype(BF16)
    q, gd = GMLP_CHUNK, GMLP_GROUP_DIM
    for g in range(GMLP_GROUPS):
        ws = ws_ref[g].astype(BF16)
        bias = bs_ref[:, g:g + 1]
        for c in range(n_chunks):
            sv = jnp.dot(ws, vn[c * q:(c + 1) * q, g * gd:(g + 1) * gd],
                         preferred_element_type=F32) + bias
            t_s[c * q:(c + 1) * q, g * gd:(g + 1) * gd] = (
                u_ref[c * q:(c + 1) * q, g * gd:(g + 1) * gd] * sv).astype(BF16)
    o_ref[...] = jnp.dot(t_s[...], wbf[...], preferred_element_type=F32)


def _gmlp_mix_out(uv, ln_g, ln_b, w_s, b_s, w_out, layer):
    t = uv.shape[0]
    half, d = GMLP_HALF, w_out.shape[2]
    tm = 512
    return pl.pallas_call(
        functools.partial(_gmlp_body, n_chunks=tm // GMLP_CHUNK),
        grid=(t // tm,),
        in_specs=[pl.BlockSpec((tm, half), lambda i: (i, 0)),
                  pl.BlockSpec((tm, half), lambda i: (i, 1)),
                  pl.BlockSpec((None, 1, half), lambda i: (layer, 0, 0)),
                  pl.BlockSpec((None, 1, half), lambda i: (layer, 0, 0)),
                  pl.BlockSpec((None, GMLP_GROUPS, GMLP_CHUNK, GMLP_CHUNK), lambda i: (layer, 0, 0, 0)),
                  pl.BlockSpec((None, GMLP_CHUNK, GMLP_GROUPS), lambda i: (layer, 0, 0)),
                  pl.BlockSpec((None, half, d), lambda i: (layer, 0, 0))],
        out_specs=pl.BlockSpec((tm, d), lambda i: (i, 0)),
        out_shape=jax.ShapeDtypeStruct((t, d), F32),
        scratch_shapes=[pltpu.VMEM((half, d), BF16), pltpu.VMEM((tm, half), BF16)],
        compiler_params=_params(1),
        name="gmlp_mix_out",
    )(uv, uv, ln_g.reshape(-1, 1, half), ln_b.reshape(-1, 1, half), w_s,
      jnp.swapaxes(b_s, 1, 2), w_out)


def _sorting_network(lo, hi):
    def merge(lo, hi, r):
        step = r * 2
        if step < hi - lo:
            yield from merge(lo, hi, step)
            yield from merge(lo + r, hi, step)
            yield from [(i, i + r) for i in range(lo + r, hi - r, step)]
        else:
            yield (lo, lo + r)

    if hi - lo >= 1:
        mid = lo + (hi - lo) // 2
        yield from _sorting_network(lo, mid)
        yield from _sorting_network(mid + 1, hi)
        yield from merge(lo, hi, 1)


def _top_rows(s, n, n_out):
    sub = 8
    n_slabs = s.shape[0] // sub
    v = [s[i * sub:(i + 1) * sub] for i in range(n_slabs)]
    for i, j in _sorting_network(0, n_slabs - 1):
        v[i], v[j] = jnp.maximum(v[i], v[j]), jnp.minimum(v[i], v[j])
    slot = lax.broadcasted_iota(jnp.int32, (n_out, s.shape[1]), 0)
    out = jnp.full((n_out, s.shape[1]), NEG_BIG, F32)
    for k in range(n):
        m = jnp.max(v[0], axis=0, keepdims=True)
        out = jnp.where(slot == k, m, out)
        hit = v[0] >= m
        depth = min(n_slabs, n - 1 - k)
        for i in range(depth):
            v[i] = jnp.where(hit, v[i + 1] if i + 1 < n_slabs else NEG_BIG, v[i])
    return out


def _peer_score_body(ht_ref, wq_ref, keys_ref, thr_ref, e1_ref, s2_ref, e2_ref, wbf):
    @pl.when(pl.program_id(0) == 0)
    def _():
        wbf[...] = wq_ref[...].astype(BF16)

    ht = ht_ref[...]
    kk, kh, sub = PEER_TOPK, PEER_HALF, PEER_ROWS_PER_BLOCK
    assert kk == 16 and sub == 8
    n_top = 3 * sub
    rank8 = lax.broadcasted_iota(jnp.int32, (sub, ht.shape[1]), 0)
    for h in range(PEER_HEADS):
        qh = jnp.dot(wbf[2 * kh * h:2 * kh * (h + 1), :], ht, preferred_element_type=F32)
        s1 = _dot_x3(keys_ref[h, 0], qh[:kh])
        s2 = _dot_x3(keys_ref[h, 1], qh[kh:])
        t1 = _top_rows(s1, kk + 1, n_top)
        t2 = _top_rows(s2, kk + 1, n_top)
        low, one = t1[0:sub], t2[0:sub]
        cand = jnp.concatenate([
            t1[0:1] + one,
            t1[0:1] + t2[sub:kk],
            t1[1:2] + one,
            jnp.where(rank8 < 5, t1[2:3] + one, NEG_BIG),
            jnp.where(rank8 < 4, t1[3:4] + one, low + t2[1:2]),
            jnp.where(rank8 >= 4, low + t2[0:1], NEG_BIG),
            t1[sub:kk] + t2[0:1],
            jnp.where(rank8 == 0, t1[kk:kk + 1] + t2[0:1],
                      jnp.where(rank8 == 1, t1[0:1] + t2[kk:kk + 1],
                                jnp.where(rank8 == 2, t1[4:5] + t2[2:3], NEG_BIG)))], axis=0)
        best = _top_rows(cand, kk + 1, n_top)
        z = jnp.sum(jnp.exp(best[:kk] - best[0:1]), axis=0, keepdims=True)
        thr = 0.5 * (best[kk - 1:kk] + best[kk:kk + 1]) - s1
        e1 = jnp.exp(s1 - t1[0:1])
        for rb in range(PEER_KEYS // sub):
            thr_ref[h, rb] = thr[rb * sub:(rb + 1) * sub]
            e1_ref[h, rb] = e1[rb * sub:(rb + 1) * sub]
        s2_ref[h] = s2
        e2_ref[h] = jnp.exp(s2 - t2[0:1]) * (0.5 / z)


def _peer_scores(ht, wq_t, sub_keys, layer):
    d, t = ht.shape
    tt = PEER_TOKEN_TILE
    nq = wq_t.shape[1]
    sub = PEER_ROWS_PER_BLOCK
    row_spec = pl.BlockSpec((PEER_HEADS, PEER_KEYS // sub, sub, tt), lambda i: (0, 0, 0, i))
    row_shape = jax.ShapeDtypeStruct((PEER_HEADS, PEER_KEYS // sub, sub, t), F32)
    tok_spec = pl.BlockSpec((PEER_HEADS, PEER_KEYS, tt), lambda i: (0, 0, i))
    tok_shape = jax.ShapeDtypeStruct((PEER_HEADS, PEER_KEYS, t), F32)
    return pl.pallas_call(
        _peer_score_body,
        grid=(t // tt,),
        in_specs=[pl.BlockSpec((d, tt), lambda i: (0, i)),
                  pl.BlockSpec((None, nq, d), lambda i: (layer, 0, 0)),
                  pl.BlockSpec((None, PEER_HEADS, 2, PEER_KEYS, PEER_HALF), lambda i: (layer, 0, 0, 0, 0))],
        out_specs=[row_spec, row_spec, tok_spec, tok_spec],
        out_shape=[row_shape, row_shape, tok_shape, tok_shape],
        scratch_shapes=[pltpu.VMEM((nq, d), BF16)],
        compiler_params=_params(1),
        name="peer_scores",
    )(ht, wq_t, sub_keys)


def _peer_tables_body(u_ref, v_ref, ub_ref, vt_ref):
    ub_ref[...] = u_ref[...].astype(BF16)
    vt_ref[...] = v_ref[...].T.astype(BF16)


def _peer_tables(u_tab, v_tab, layer):
    n, d = u_tab.shape[1:]
    eb = MM_COL_TILE
    return pl.pallas_call(
        _peer_tables_body,
        grid=(n // eb,),
        in_specs=[pl.BlockSpec((None, eb, d), lambda e: (layer, e, 0)),
                  pl.BlockSpec((None, eb, d), lambda e: (layer, e, 0))],
        out_specs=[pl.BlockSpec((eb, d), lambda e: (e, 0)), pl.BlockSpec((d, eb), lambda e: (0, e))],
        out_shape=[jax.ShapeDtypeStruct((n, d), BF16), jax.ShapeDtypeStruct((d, n), BF16)],
        compiler_params=_params(1),
        name="peer_tables",
    )(u_tab, v_tab)


def _peer_mix_body(ht_ref, thr_ref, e1_ref, s2_ref, e2_ref, u_ref, vt_ref, o_ref, act_s, coef_s):
    @pl.when(pl.program_id(1) == 0)
    def _():
        o_ref[...] = jnp.zeros(o_ref.shape, F32)

    lanes, kq = 128, 32
    n_q = PEER_KEYS // kq
    act_s[...] = jnp.dot(u_ref[...], ht_ref[...], preferred_element_type=F32)
    c0 = math.sqrt(2.0 / math.pi)

    def gate_chunk(ci, carry):
        part, q = ci // n_q, ci % n_q
        c = pl.ds(pl.multiple_of(part * lanes, lanes), lanes)
        keys = pl.ds(pl.multiple_of(q * kq, kq), kq)
        gates = [None] * PEER_ROWS_PER_BLOCK
        for h in range(PEER_HEADS):
            s2 = s2_ref[h, keys, c]
            e2 = e2_ref[h, keys, c]
            for r in range(PEER_ROWS_PER_BLOCK):
                chosen = s2 >= thr_ref[h, 0, r:r + 1, c]
                term = jnp.where(chosen, e2, 0.0) * e1_ref[h, 0, r:r + 1, c]
                gates[r] = term if h == 0 else gates[r] + term
        for r in range(PEER_ROWS_PER_BLOCK):
            rows = pl.ds(pl.multiple_of(r * PEER_KEYS + q * kq, kq), kq)
            a = act_s[rows, c]
            twice_gelu = a * (1.0 + jnp.tanh(a * (c0 + (c0 * 0.044715) * (a * a))))
            coef_s[rows, c] = (twice_gelu * gates[r]).astype(BF16)
        return carry

    lax.fori_loop(0, (o_ref.shape[1] // lanes) * n_q, gate_chunk, 0)
    o_ref[...] = o_ref[...] + jnp.dot(vt_ref[...], coef_s[...], preferred_element_type=F32)


def _peer_mix(ht, thr, e1, s2, e2, u_bf, vt_bf):
    d, t = ht.shape
    ts, rb = PEER_SUPER_TILE, PEER_ROWS_PER_BLOCK
    eb = rb * PEER_KEYS
    n_blocks = u_bf.shape[0] // eb
    row_spec = pl.BlockSpec((PEER_HEADS, 1, rb, ts), lambda s, e: (0, e, 0, s))
    tok_spec = pl.BlockSpec((PEER_HEADS, PEER_KEYS, ts), lambda s, e: (0, 0, s))
    return pl.pallas_call(
        _peer_mix_body,
        grid=(t // ts, n_blocks),
        in_specs=[pl.BlockSpec((d, ts), lambda s, e: (0, s)), row_spec, row_spec, tok_spec, tok_spec,
                  pl.BlockSpec((eb, d), lambda s, e: (e, 0)), pl.BlockSpec((d, eb), lambda s, e: (0, e))],
        out_specs=pl.BlockSpec((d, ts), lambda s, e: (0, s)),
        out_shape=jax.ShapeDtypeStruct((d, t), F32),
        scratch_shapes=[pltpu.VMEM((eb, ts), F32), pltpu.VMEM((eb, ts), BF16)],
        compiler_params=_params(2),
        name="peer_mix",
    )(ht, thr, e1, s2, e2, u_bf, vt_bf)


def kernel(x_prompt, x_sample, cache_k, cache_v, state_ssm, c, c_ctx, w_mod, b_mod, ln_g, ln_b, attn_w_qkv, attn_w_o, attn_lambda, attn_subln_g, ssm_w_in, ssm_conv_w, ssm_conv_b, ssm_dt_bias, ssm_a_log, ssm_d, ssm_norm_g, ssm_w_out, gmlp_w_in, gmlp_b_in, gmlp_ln_g, gmlp_ln_b, gmlp_w_s, gmlp_b_s, gmlp_w_out, peer_w_q, peer_sub_keys, peer_u, peer_v):
    bc, lc, d = x_prompt.shape
    bl, ll, _ = x_sample.shape
    depth = w_mod.shape[0]
    t_ctx, t_lat = bc * lc, bl * ll
    alpha = (2 * depth) ** 0.25
    assert 1 + bl <= N_MOD_GROUPS and t_ctx % ll == 0

    x = (x_prompt.reshape(t_ctx, d), x_sample.reshape(t_lat, d))
    cond = jnp.zeros((N_MOD_GROUPS, d), F32).at[0].set(c_ctx).at[1:1 + bl].set(c)
    mods = _modulation(cond, w_mod, b_mod).reshape(depth, N_MOD_GROUPS, 6, d)
    rope = _rope_tables(ll, 2 * ATTN_HEAD_DIM)
    past = cache_k.shape[2]
    ck = cache_k.reshape(bl, cache_k.shape[1], past, -1)
    cv = cache_v.reshape(bl, cache_v.shape[1], past, -1)
    wq_t = jnp.swapaxes(peer_w_q, 1, 2)

    new_k, new_v, new_s = [], [], []
    h = _modulate(x, mods[0], t_ctx, ll, shift=0, scale=1)
    for i in range(depth):
        kind, j = i % 3, i // 3
        if kind == 0:
            lam_init = 0.8 - 0.6 * math.exp(-0.3 * i)
            qkv, k_ctx, v_ctx = _qkv_rope(h, attn_w_qkv, j, rope, t_ctx, ll)
            o_ctx = _diff_attention(qkv, attn_lambda, attn_subln_g, j, lam_init, 0, bc, lc)
            o_lat = _diff_attention(qkv, attn_lambda, attn_subln_g, j, lam_init, t_ctx, bl, ll,
                                    cache=(ck, cv))
            o = _linear((o_ctx, o_lat), attn_w_o, j, d)
            new_k.append(k_ctx.reshape(bc, lc, ATTN_HEADS, 2, ATTN_HEAD_DIM))
            new_v.append(v_ctx.reshape(bc, lc, ATTN_HEADS, 2 * ATTN_HEAD_DIM))
        elif kind == 1:
            n_main = SSM_INNER + SSM_INNER + 2 * SSM_GROUPS * SSM_STATE
            zxbc = _linear(h, ssm_w_in, j, n_main)
            def regroup(a):
                a = a.reshape(a.shape[:-1] + (2, SSM_GROUPS, SSM_HPG))
                a = jnp.moveaxis(a, -3, -2).reshape(a.shape[:-3] + (SSM_GROUPS, 2 * SSM_HPG))
                pad = [(0, 0)] * (a.ndim - 1) + [(0, 128 - 2 * SSM_HPG)]
                return jnp.pad(a, pad).reshape(a.shape[:-2] + (SSM_GROUPS * 128,))
            w_dt = regroup(ssm_w_in[j][:, n_main:])[None]
            dtp = _linear(h, w_dt, 0, SSM_GROUPS * 128)
            dtb = regroup(ssm_dt_bias[j].reshape(1, -1))
            a_l = regroup(-jnp.exp(ssm_a_log[j]).reshape(1, -1))
            d_l = jnp.repeat(ssm_d[j], SSM_HEAD_DIM).reshape(1, SSM_INNER)
            cw, cb = ssm_conv_w[j], ssm_conv_b[j].reshape(1, -1)
            y_ctx, st = _ssd(zxbc, dtp, cw, cb, dtb, a_l, d_l, 0, bc, lc, emit_state=True)
            (y_lat,) = _ssd(zxbc, dtp, cw, cb, dtb, a_l, d_l, t_ctx, bl, ll, h0=state_ssm, layer=j)
            o = _ssm_out((y_ctx, y_lat), zxbc, ssm_norm_g, ssm_w_out, j)
            new_s.append(st)
        else:
            uv = _linear(h, gmlp_w_in, j, 2 * GMLP_HALF, bias=gmlp_b_in[j], act="gelu")
            o = _gmlp_mix_out(uv, gmlp_ln_g, gmlp_ln_b, gmlp_w_s, gmlp_b_s, gmlp_w_out, j)
        x, ht = _res_ln(x, o, mods[i], mods[i], ln_g[i, 0], ln_b[i, 0], t_ctx, ll, alpha=alpha,
                        gate=2, nshift=3, nscale=4, emit_ht=True)
        thr, e1, s2, e2 = _peer_scores(ht, wq_t, peer_sub_keys, i)
        u_bf, vt_bf = _peer_tables(peer_u, peer_v, i)
        f_t = _peer_mix(ht, thr, e1, s2, e2, u_bf, vt_bf)
        if i + 1 < depth:
            x, h = _res_ln(x, f_t, mods[i], mods[i + 1], ln_g[i, 1], ln_b[i, 1], t_ctx, ll,
                           alpha=alpha, gate=5, nshift=0, nscale=1, o_transposed=True, emit_h=True)
        else:
            out_ctx, out_lat = _res_ln(x, f_t, mods[i], mods[i], ln_g[i, 1], ln_b[i, 1], t_ctx, ll,
                                       alpha=alpha, gate=5, o_transposed=True, split_out=True)
    return (out_ctx.reshape(bc, lc, d), out_lat.reshape(bl, ll, d),
            jnp.stack(new_k, axis=1), jnp.stack(new_v, axis=1), jnp.stack(new_s, axis=1))
```
